```python
import jax, jax.numpy as jnp
from jax import lax
import numpy as np

D_MODEL = 1024
BATCH = 8
SEQ = 4096
DEPTH = 4

GRID_W = 64
CTX_LEN = 256
N_MIXERS = 2
N_HGRN_LAYERS = (DEPTH + 1) // 2
N_MLA_LAYERS = DEPTH // 2
FFN_HIDDEN = 2816
N_MOD = 9
RMS_EPS = 1e-6
HG_HEADS = 8
HG_DK = 128
HG_DV = D_MODEL // HG_HEADS
HG_CHUNK = 32
HG_HK = HG_HEADS * HG_DK
HG_HV = HG_HEADS * HG_DV
HG_IN = 3 * HG_HK + HG_HV + D_MODEL
F_MIN = 1e-6
MLA_HEADS = 8
MLA_NOPE = 128
MLA_ROPE = 64
MLA_V = 128
Q_LORA = 384
KV_LORA = 256
MLA_QK = MLA_NOPE + MLA_ROPE
MLA_DOWN = Q_LORA + KV_LORA + MLA_ROPE
MLA_SCALE = MLA_QK ** -0.5
Q_BLOCK = 128
ROPE_BASE = 10000.0

kernel_name = "hybrid_hgrn2_mla_macaron_prefix_dit"


def rms_norm(x, g):
    xf = x.astype(jnp.float32)
    y = xf * lax.rsqrt(jnp.mean(xf * xf, axis=-1, keepdims=True) + RMS_EPS)
    return (y * g.astype(jnp.float32)).astype(x.dtype)


def modulate(h, shift, scale):
    return h * (1 + scale) + shift


def swiglu(h, w_gate, w_up, w_down):
    return (jax.nn.silu(h @ w_gate) * (h @ w_up)) @ w_down


def ffn_half(h, mod, base, g, w_gate, w_up, w_down):
    a = modulate(rms_norm(h, g), mod[:, base], mod[:, base + 1])
    return h + 0.5 * mod[:, base + 2] * swiglu(a, w_gate, w_up, w_down)


def axial_rope_tables(n_tokens):
    rows = n_tokens // GRID_W
    row = jnp.repeat(jnp.arange(rows), GRID_W).astype(jnp.float32)
    col = jnp.tile(jnp.arange(GRID_W), rows).astype(jnp.float32)
    axis_dims = MLA_ROPE // 2
    inv = 1.0 / (ROPE_BASE ** (jnp.arange(0, axis_dims, 2, dtype=jnp.float32) / axis_dims))
    ang = jnp.stack([row[:, None] * inv, col[:, None] * inv], axis=1)
    return jnp.cos(ang), jnp.sin(ang)


def apply_axial_rope(x, cos, sin):
    xr = x.reshape(x.shape[:-1] + (2, 2, MLA_ROPE // 4))
    a, b = xr[..., 0, :], xr[..., 1, :]
    cos = cos.astype(x.dtype)
    sin = sin.astype(x.dtype)
    out = jnp.stack([a * cos - b * sin, a * sin + b * cos], axis=-2)
    return out.reshape(x.shape)


def _heads(a, d):
    b_, t_, _ = a.shape
    return a.reshape(b_, t_, -1, d).transpose(0, 2, 1, 3)


def hgrn2_chunk_scan(q, k, v, logf, s0, with_output):
    b_, h_, t_, dk = q.shape
    n = t_ // HG_CHUNK

    def chunks(a):
        return jnp.moveaxis(a.reshape(b_, h_, n, HG_CHUNK, a.shape[-1]), 2, 0)

    causal = jnp.tril(jnp.ones((HG_CHUNK, HG_CHUNK), dtype=bool))[:, :, None]
    causal_f = causal.astype(jnp.float32)

    def step(s, inp):
        qb, kb, vb, gb = inp
        cum = jnp.cumsum(gb, axis=2)
        cum_last = cum[:, :, -1:, :]
        k_end = kb * jnp.exp(cum_last - cum)
        s_new = jnp.exp(cum_last)[:, :, 0, :, None] * s + jnp.einsum('bhck,bhcv->bhkv', k_end, vb)
        if not with_output:
            return s_new, None
        diff = cum[:, :, :, None, :] - cum[:, :, None, :, :]
        decay = jnp.exp(jnp.where(causal, diff, 0.0)) * causal_f
        scores = jnp.einsum('bhtk,bhtsk,bhsk->bhts', qb, decay, kb)
        o = jnp.einsum('bhtk,bhkv->bhtv', qb * jnp.exp(cum), s) + jnp.einsum('bhts,bhsv->bhtv', scores, vb)
        return s_new, o

    s_fin, o = lax.scan(step, s0, (chunks(q), chunks(k), chunks(v), chunks(logf)))
    if with_output:
        o = jnp.moveaxis(o, 0, 2).reshape(b_, h_, t_, v.shape[-1])
    return s_fin, o


def hgrn2_inputs(a, w_in, lb):
    p = a @ w_in
    q, z_f, z_b, i, g = jnp.split(p, [HG_HK, 2 * HG_HK, 3 * HG_HK, 3 * HG_HK + HG_HV], axis=-1)
    q = _heads(q, HG_DK).astype(jnp.float32)
    v = _heads(i, HG_DV).astype(jnp.float32)
    dirs = []
    for d, z in enumerate((z_f, z_b)):
        zf = _heads(z, HG_DK).astype(jnp.float32)
        lbd = lb[d].reshape(HG_HEADS, 1, HG_DK)
        f = lbd + (1 - lbd) * jax.nn.sigmoid(zf)
        logf = jnp.log(jnp.maximum(f, F_MIN))
        k = (1 - lbd) * jax.nn.sigmoid(-zf)
        dirs.append((k, logf))
    return q, v, dirs[0], dirs[1], g


def hgrn2_mixer(u, uc, w_in, w_out, gn, lb, with_ctx_out):
    q, v, (kf, gf), (kb, gb), g = hgrn2_inputs(u, w_in, lb)
    qc, vc, (kcf, gcf), (kcb, gcb), gc = hgrn2_inputs(uc, w_in, lb)
    zero = jnp.zeros((u.shape[0], HG_HEADS, HG_DK, HG_DV), jnp.float32)
    flip = lambda a: jnp.flip(a, axis=2)
    s_cf, o_cf = hgrn2_chunk_scan(qc, kcf, vc, gcf, zero, with_ctx_out)
    s_cb, o_cb = hgrn2_chunk_scan(flip(qc), flip(kcb), flip(vc), flip(gcb), zero, with_ctx_out)
    _, o_f = hgrn2_chunk_scan(q, kf, v, gf, s_cf, True)
    _, o_b = hgrn2_chunk_scan(flip(q), flip(kb), flip(v), flip(gb), s_cb, True)

    def readout(o, gate):
        o = rms_norm(o, gn)
        b_, h_, t_, dv = o.shape
        o = o.transpose(0, 2, 1, 3).reshape(b_, t_, h_ * dv).astype(gate.dtype)
        return (o * jax.nn.silu(gate)) @ w_out

    y = readout(o_f + flip(o_b), g)
    yc = readout(o_cf + flip(o_cb), gc) if with_ctx_out else None
    return y, yc


def mla_project(a, w_down, q_norm, w_uq, kv_norm, w_ukv, rope):
    b_, t_, _ = a.shape
    dproj = a @ w_down
    cq, ckv, kr = jnp.split(dproj, [Q_LORA, Q_LORA + KV_LORA], axis=-1)
    q = (rms_norm(cq, q_norm) @ w_uq).reshape(b_, t_, MLA_HEADS, MLA_QK)
    kv = (rms_norm(ckv, kv_norm) @ w_ukv).reshape(b_, t_, MLA_HEADS, MLA_NOPE + MLA_V)
    q_nope, q_rope = jnp.split(q, [MLA_NOPE], axis=-1)
    k_nope, v = jnp.split(kv, [MLA_NOPE], axis=-1)
    if rope is not None:
        cos, sin = rope
        q_rope = apply_axial_rope(q_rope, cos[:, None], sin[:, None])
        kr = apply_axial_rope(kr, cos, sin)
    k_rope = jnp.broadcast_to(kr[:, :, None, :], (b_, t_, MLA_HEADS, MLA_ROPE))
    q = jnp.concatenate([q_nope, q_rope], axis=-1)
    k = jnp.concatenate([k_nope, k_rope], axis=-1)
    return q, k, v


def attend(q, k, v):
    s = jnp.einsum('bqhd,bkhd->bhqk', q, k).astype(jnp.float32) * MLA_SCALE
    p = jax.nn.softmax(s, axis=-1).astype(v.dtype)
    return jnp.einsum('bhqk,bkhd->bqhd', p, v)


def mla_mixer(u, uc, w_down, q_norm, w_uq, kv_norm, w_ukv, w_o, cos, sin, with_ctx_out):
    b_, t_, _ = u.shape
    q, k, v = mla_project(u, w_down, q_norm, w_uq, kv_norm, w_ukv, (cos, sin))
    qc, kc, vc = mla_project(uc, w_down, q_norm, w_uq, kv_norm, w_ukv, None)
    k_all = jnp.concatenate([k, kc], axis=1)
    v_all = jnp.concatenate([v, vc], axis=1)
    nb = t_ // Q_BLOCK
    qb = jnp.moveaxis(q.reshape(b_, nb, Q_BLOCK, MLA_HEADS, MLA_QK), 1, 0)
    ob = lax.map(lambda blk: attend(blk, k_all, v_all), qb)
    o = jnp.moveaxis(ob, 0, 1).reshape(b_, t_, MLA_HEADS * MLA_V)
    y = o @ w_o
    yc = attend(qc, kc, vc).reshape(b_, uc.shape[1], MLA_HEADS * MLA_V) @ w_o if with_ctx_out else None
    return y, yc


def setup_inputs(seed: int = 0) -> dict:
    key = jax.random.key(seed)
    ks = jax.random.split(key, 24)
    nrm = lambda k, shape, s: jax.random.normal(k, shape, jnp.float32) * s
    D = D_MODEL
    return {
        'x': nrm(ks[0], (BATCH, SEQ, D), 1.0),
        'c': nrm(ks[1], (BATCH, D), 1.0),
        'ctx': nrm(ks[2], (BATCH, CTX_LEN, D), 1.0),
        'c_ctx': nrm(ks[3], (D,), 1.0),
        'mod_w': nrm(ks[4], (DEPTH, D, N_MOD * D), 0.5 * D ** -0.5),
        'mod_b': nrm(ks[5], (DEPTH, N_MOD * D), 0.01),
        'norm_g': 1.0 + nrm(ks[6], (DEPTH, 3, D), 0.02),
        'ffn_w_gate': nrm(ks[7], (DEPTH, 2, D, FFN_HIDDEN), D ** -0.5),
        'ffn_w_up': nrm(ks[8], (DEPTH, 2, D, FFN_HIDDEN), D ** -0.5),
        'ffn_w_down': nrm(ks[9], (DEPTH, 2, FFN_HIDDEN, D), FFN_HIDDEN ** -0.5),
        'hg_w_in': nrm(ks[10], (N_HGRN_LAYERS, D, HG_IN), D ** -0.5),
        'hg_w_out': nrm(ks[11], (N_HGRN_LAYERS, HG_HV, D), HG_HV ** -0.5),
        'hg_gn': 1.0 + nrm(ks[12], (N_HGRN_LAYERS, HG_DV), 0.02),
        'hg_lb_logits': nrm(ks[13], (N_HGRN_LAYERS, 2, HG_HK), 0.5),
        'mla_w_down': nrm(ks[14], (N_MLA_LAYERS, D, MLA_DOWN), D ** -0.5),
        'mla_q_norm': 1.0 + nrm(ks[15], (N_MLA_LAYERS, Q_LORA), 0.02),
        'mla_w_uq': nrm(ks[16], (N_MLA_LAYERS, Q_LORA, MLA_HEADS * MLA_QK), Q_LORA ** -0.5),
        'mla_kv_norm': 1.0 + nrm(ks[17], (N_MLA_LAYERS, KV_LORA), 0.02),
        'mla_w_ukv': nrm(ks[18], (N_MLA_LAYERS, KV_LORA, MLA_HEADS * (MLA_NOPE + MLA_V)), KV_LORA ** -0.5),
        'mla_w_o': nrm(ks[19], (N_MLA_LAYERS, MLA_HEADS * MLA_V, D), (MLA_HEADS * MLA_V) ** -0.5),
        'final_g': 1.0 + nrm(ks[20], (D,), 0.02),
    }


def reference(x, c, ctx, c_ctx, mod_w, mod_b, norm_g, ffn_w_gate, ffn_w_up, ffn_w_down,
              hg_w_in, hg_w_out, hg_gn, hg_lb_logits, mla_w_down, mla_q_norm, mla_w_uq,
              mla_kv_norm, mla_w_ukv, mla_w_o, final_g):
    n_tok = x.shape[1]
    cos, sin = axial_rope_tables(n_tok)
    p_lb = jax.nn.softmax(hg_lb_logits.astype(jnp.float32), axis=0)
    lbs = jnp.maximum(jnp.cumsum(p_lb, axis=0) - p_lb[0:1], 0.0)
    sc = jax.nn.silu(c)
    scc = jax.nn.silu(c_ctx)[None]
    h, hc = x, ctx
    for i in range(DEPTH):
        last = i == DEPTH - 1
        mod = (sc @ mod_w[i] + mod_b[i]).reshape(-1, N_MOD, 1, D_MODEL)
        modc = (scc @ mod_w[i] + mod_b[i]).reshape(-1, N_MOD, 1, D_MODEL)
        h = ffn_half(h, mod, 0, norm_g[i, 0], ffn_w_gate[i, 0], ffn_w_up[i, 0], ffn_w_down[i, 0])
        hc = ffn_half(hc, modc, 0, norm_g[i, 0], ffn_w_gate[i, 0], ffn_w_up[i, 0], ffn_w_down[i, 0])
        u = modulate(rms_norm(h, norm_g[i, 1]), mod[:, 3], mod[:, 4])
        uc = modulate(rms_norm(hc, norm_g[i, 1]), modc[:, 3], modc[:, 4])
        j = i // N_MIXERS
        if i % N_MIXERS == 0:
            y, yc = hgrn2_mixer(u, uc, hg_w_in[j], hg_w_out[j], hg_gn[j], lbs[j], not last)
        else:
            y, yc = mla_mixer(u, uc, mla_w_down[j], mla_q_norm[j], mla_w_uq[j], mla_kv_norm[j],
                              mla_w_ukv[j], mla_w_o[j], cos, sin, not last)
        h = h + mod[:, 5] * y
        h = ffn_half(h, mod, 6, norm_g[i, 2], ffn_w_gate[i, 1], ffn_w_up[i, 1], ffn_w_down[i, 1])
        if not last:
            hc = hc + modc[:, 5] * yc
            hc = ffn_half(hc, modc, 6, norm_g[i, 2], ffn_w_gate[i, 1], ffn_w_up[i, 1], ffn_w_down[i, 1])
    return rms_norm(h, final_g)
```

```python
import functools

import jax
import jax.numpy as jnp
import numpy as np
from jax import lax
from jax.experimental import pallas as pl
from jax.experimental.pallas import tpu as pltpu

F32 = jnp.float32
BF16 = jnp.bfloat16

D_MODEL = 1024
DEPTH = 4
GRID_W = 64
N_MOD = 9
FFN_HIDDEN = 2816
RMS_EPS = 1e-6
HG_HEADS = 8
HG_DK = 128
HG_DV = 128
HG_HK = HG_HEADS * HG_DK
HG_HV = HG_HEADS * HG_DV
HG_IN = 3 * HG_HK + HG_HV + D_MODEL
F_MIN = 1e-6
MLA_HEADS = 8
MLA_NOPE = 128
MLA_ROPE = 64
MLA_V = 128
Q_LORA = 384
KV_LORA = 256
MLA_QK = MLA_NOPE + MLA_ROPE
MLA_SCALE = MLA_QK ** -0.5
ROPE_BASE = 10000.0

LANES = 128
SUBLANES = 8
MXU_DIM = 256
VMEM_LIMIT = 56 * 1024 * 1024

TM = 512
FFN_CHUNK = MXU_DIM
HG_C = 128
TQ = 256
MLA_HW = 2 * LANES
MLA_DOWN_P = Q_LORA + KV_LORA + 2 * LANES


def _cparams(sem):
    return pltpu.CompilerParams(dimension_semantics=sem, vmem_limit_bytes=VMEM_LIMIT)


def _resident(shape):
    nd = len(shape)
    return pl.BlockSpec(shape, lambda *_: (0,) * nd, pipeline_mode=pl.Buffered(1))


def _dot(a, b):
    return jnp.dot(a, b, preferred_element_type=F32)


def _dot_nt(a, b):
    return lax.dot_general(a, b, (((1,), (1,)), ((), ())), preferred_element_type=F32)


def _dot_tn(a, b):
    return lax.dot_general(a, b, (((0,), (0,)), ((), ())), preferred_element_type=F32)


def _split3(x):
    hi = x.astype(BF16)
    r1 = x - hi.astype(F32)
    mid = r1.astype(BF16)
    lo = (r1 - mid.astype(F32)).astype(BF16)
    return hi, mid, lo


def _rms_modulate(h, g, shift, scale):
    ms = jnp.mean(h * h, axis=-1, keepdims=True)
    return (h * lax.rsqrt(ms + RMS_EPS) * g) * (1.0 + scale) + shift


def _mod_slices(mod_ref, base, n):
    m = mod_ref[0]
    return [m[:, (base + j) * D_MODEL:(base + j + 1) * D_MODEL] for j in range(n)]


def _mod_kernel(c_ref, w_ref, b_ref, o_ref):
    c = c_ref[...]
    s = c * jax.nn.sigmoid(c)
    s_hi = s.astype(BF16)
    s_lo = (s - s_hi.astype(F32)).astype(BF16)
    w = w_ref[0]
    w_hi = w.astype(BF16)
    w_lo = (w - w_hi.astype(F32)).astype(BF16)
    acc = _dot(s_hi, w_hi) + (_dot(s_lo, w_hi) + _dot(s_hi, w_lo))
    o_ref[0] = acc + b_ref[0]


def _mod_call(cond, mod_w, mod_b):
    rows = cond.shape[0]
    tn = D_MODEL
    n_out = N_MOD * D_MODEL
    return pl.pallas_call(
        _mod_kernel,
        grid=(DEPTH, n_out // tn),
        in_specs=[
            pl.BlockSpec((rows, D_MODEL), lambda l, n: (0, 0)),
            pl.BlockSpec((1, D_MODEL, tn), lambda l, n: (l, 0, n)),
            pl.BlockSpec((1, 1, tn), lambda l, n: (l, 0, n)),
        ],
        out_specs=pl.BlockSpec((1, rows, tn), lambda l, n: (l, 0, n)),
        out_shape=jax.ShapeDtypeStruct((DEPTH, rows, n_out), F32),
        compiler_params=_cparams(("arbitrary", "arbitrary")),
        name="mod_vectors",
    )(cond, mod_w, mod_b.reshape(DEPTH, 1, n_out))


class _Rows:
    def __init__(self, batch, seq, ctx_len):
        self.batch, self.seq, self.ctx_len = batch, seq, ctx_len
        self.n_lat = batch * seq
        self.n_ctx = batch * ctx_len
        self.n_tok = self.n_lat + self.n_ctx
        assert seq % TM == 0 and self.n_ctx % TM == 0
        self.lat_tiles = self.n_lat // TM
        self.tiles_per_seq = seq // TM
        self.all_tiles = self.n_tok // TM

    def mod_spec(self):
        lat_tiles, tps, ctx_row = self.lat_tiles, self.tiles_per_seq, self.batch
        return pl.BlockSpec(
            (1, 1, N_MOD * D_MODEL),
            lambda i: (jnp.where(i < lat_tiles, i // tps, ctx_row), 0, 0))


def _tile_spec(width, col=0):
    return pl.BlockSpec((TM, width), lambda i: (i, col))


def _ffn_kernel(h_ref, mod_ref, g_ref, wg_ref, wu_ref, wd_ref, *rest, base, final):
    o_ref = rest[-1]
    h = h_ref[...]
    shift, scale, gate = _mod_slices(mod_ref, base, 3)
    a = _rms_modulate(h, g_ref[...], shift, scale).astype(BF16)
    acc = jnp.zeros((TM, D_MODEL), F32)
    for j in range(FFN_HIDDEN // FFN_CHUNK):
        sl = slice(j * FFN_CHUNK, (j + 1) * FFN_CHUNK)
        gg = _dot(a, wg_ref[:, sl])
        uu = _dot(a, wu_ref[:, sl])
        hm = (gg * jax.nn.sigmoid(gg) * uu).astype(BF16)
        acc = acc + _dot(hm, wd_ref[sl, :])
    out = h + (0.5 * gate) * acc
    if final:
        fg_ref = rest[0]
        ms = jnp.mean(out * out, axis=-1, keepdims=True)
        out = out * lax.rsqrt(ms + RMS_EPS) * fg_ref[...]
    o_ref[...] = out


def _ffn_call(rows, hs, mod_l, g, wg, wu, wd, base, n_tiles, final_g=None):
    final = final_g is not None
    in_specs = [
        _tile_spec(D_MODEL), rows.mod_spec(), _resident((1, D_MODEL)),
        _resident((D_MODEL, FFN_HIDDEN)), _resident((D_MODEL, FFN_HIDDEN)),
        _resident((FFN_HIDDEN, D_MODEL)),
    ]
    args = [hs, mod_l, g, wg, wu, wd]
    if final:
        in_specs.append(_resident((1, D_MODEL)))
        args.append(final_g)
    return pl.pallas_call(
        functools.partial(_ffn_kernel, base=base, final=final),
        grid=(n_tiles,),
        in_specs=in_specs,
        out_specs=_tile_spec(D_MODEL),
        out_shape=jax.ShapeDtypeStruct((n_tiles * TM, D_MODEL), F32),
        compiler_params=_cparams(("arbitrary",)),
        name="ffn_half",
    )(*args)


def _hgrn_in_kernel(h_ref, mod_ref, g_ref, w_ref, o_ref):
    shift, scale = _mod_slices(mod_ref, 3, 2)
    a = _rms_modulate(h_ref[...], g_ref[...], shift, scale).astype(BF16)
    for j in range(HG_IN // D_MODEL):
        sl = slice(j * D_MODEL, (j + 1) * D_MODEL)
        o_ref[:, sl] = _dot(a, w_ref[:, sl])


def _hgrn_in_call(rows, hs, mod_l, g, w_in):
    return pl.pallas_call(
        _hgrn_in_kernel,
        grid=(rows.all_tiles,),
        in_specs=[_tile_spec(D_MODEL), rows.mod_spec(), _resident((1, D_MODEL)),
                  _resident((D_MODEL, HG_IN))],
        out_specs=_tile_spec(HG_IN),
        out_shape=jax.ShapeDtypeStruct((rows.n_tok, HG_IN), F32),
        compiler_params=_cparams(("arbitrary",)),
        name="hgrn_in_proj",
    )(hs, mod_l, g, w_in)


def _hgrn_out_kernel(h_ref, mod_ref, of_ref, ob_ref, gate_ref, gn_ref, bd_ref, w_ref, o_ref):
    (res_gate,) = _mod_slices(mod_ref, 5, 1)
    o = of_ref[...] + ob_ref[...]
    sq = o * o
    sq_hi = sq.astype(BF16)
    sq_lo = (sq - sq_hi.astype(F32)).astype(BF16)
    ss = _dot(sq_hi, bd_ref[...]) + _dot(sq_lo, bd_ref[...])
    on = o * lax.rsqrt(ss * (1.0 / HG_DV) + RMS_EPS) * gn_ref[...]
    gt = gate_ref[...]
    y = _dot((on * (gt * jax.nn.sigmoid(gt))).astype(BF16), w_ref[...])
    o_ref[...] = h_ref[...] + res_gate * y


def _hgrn_out_call(rows, hs, mod_l, o_f, o_b, proj, gn, bd, w_out, n_tiles):
    return pl.pallas_call(
        _hgrn_out_kernel,
        grid=(n_tiles,),
        in_specs=[_tile_spec(D_MODEL), rows.mod_spec(), _tile_spec(HG_HV), _tile_spec(HG_HV),
                  _tile_spec(D_MODEL, col=(3 * HG_HK + HG_HV) // D_MODEL),
                  _resident((1, HG_HV)), _resident((HG_HV, HG_HV)), _resident((HG_HV, D_MODEL))],
        out_specs=_tile_spec(D_MODEL),
        out_shape=jax.ShapeDtypeStruct((n_tiles * TM, D_MODEL), F32),
        compiler_params=_cparams(("arbitrary",)),
        name="hgrn_readout",
    )(hs, mod_l, o_f, o_b, proj, gn, bd, w_out)


def _mla_proj_kernel(h_ref, mod_ref, g_ref, wd_ref, qn_ref, kvn_ref, wq_ref, wqs_ref, wkv_ref,
                     cos_ref, sin_ref, q_ref, k_ref, v_ref):
    shift, scale = _mod_slices(mod_ref, 3, 2)
    a = _rms_modulate(h_ref[...], g_ref[...], shift, scale).astype(BF16)
    dp = _dot(a, wd_ref[...])
    cq = dp[:, :Q_LORA]
    ckv = dp[:, Q_LORA:Q_LORA + KV_LORA]
    kr_a = dp[:, Q_LORA + KV_LORA:Q_LORA + KV_LORA + LANES]
    kr_b = dp[:, Q_LORA + KV_LORA + LANES:]
    cqn = (cq * lax.rsqrt(jnp.mean(cq * cq, axis=-1, keepdims=True) + RMS_EPS) * qn_ref[...]).astype(BF16)
    ckvn = (ckv * lax.rsqrt(jnp.mean(ckv * ckv, axis=-1, keepdims=True) + RMS_EPS) * kvn_ref[...]).astype(BF16)
    cos = cos_ref[...]
    sin = sin_ref[...]
    cos_q = jnp.concatenate([cos * MLA_SCALE] * MLA_HEADS, axis=1)
    sin_q = jnp.concatenate([sin * MLA_SCALE] * MLA_HEADS, axis=1)
    q_ref[...] = (_dot(cqn, wq_ref[...]) * cos_q + _dot(cqn, wqs_ref[...]) * sin_q).astype(BF16)
    kr = (kr_a * cos[:, LANES:] + kr_b * sin[:, LANES:]).astype(BF16)
    kv = _dot(ckvn, wkv_ref[...])
    kn = kv[:, :MLA_HEADS * MLA_NOPE].astype(BF16)
    pieces = []
    for hh in range(MLA_HEADS):
        pieces += [kn[:, hh * MLA_NOPE:(hh + 1) * MLA_NOPE], kr]
    k_ref[...] = jnp.concatenate(pieces, axis=1)
    v_ref[...] = kv[:, MLA_HEADS * MLA_NOPE:].astype(BF16)


def _mla_proj_call(rows, hs, mod_l, g, wd, qn, kvn, wq, wqs, wkv, cos_t, sin_t):
    lat_tiles, tps = rows.lat_tiles, rows.tiles_per_seq
    rope_spec = pl.BlockSpec((TM, MLA_HW), lambda i: (jnp.where(i < lat_tiles, i % tps, tps), 0))
    qk_w = MLA_HEADS * MLA_HW
    return pl.pallas_call(
        _mla_proj_kernel,
        grid=(rows.all_tiles,),
        in_specs=[_tile_spec(D_MODEL), rows.mod_spec(), _resident((1, D_MODEL)),
                  _resident((D_MODEL, MLA_DOWN_P)), _resident((1, Q_LORA)), _resident((1, KV_LORA)),
                  _resident((Q_LORA, qk_w)), _resident((Q_LORA, qk_w)),
                  _resident((KV_LORA, MLA_HEADS * (MLA_NOPE + MLA_V))), rope_spec, rope_spec],
        out_specs=[_tile_spec(qk_w), _tile_spec(qk_w), _tile_spec(MLA_HEADS * MLA_V)],
        out_shape=[jax.ShapeDtypeStruct((rows.n_tok, qk_w), BF16),
                   jax.ShapeDtypeStruct((rows.n_tok, qk_w), BF16),
                   jax.ShapeDtypeStruct((rows.n_tok, MLA_HEADS * MLA_V), BF16)],
        compiler_params=_cparams(("arbitrary",)),
        name="mla_proj",
    )(hs, mod_l, g, wd, qn, kvn, wq, wqs, wkv, cos_t, sin_t)


def _mla_out_kernel(h_ref, mod_ref, o_ref_in, w_ref, o_ref):
    (res_gate,) = _mod_slices(mod_ref, 5, 1)
    o_ref[...] = h_ref[...] + res_gate * _dot(o_ref_in[...], w_ref[...])


def _mla_out_call(rows, hs, mod_l, attn, w_o, n_tiles):
    return pl.pallas_call(
        _mla_out_kernel,
        grid=(n_tiles,),
        in_specs=[_tile_spec(D_MODEL), rows.mod_spec(), _tile_spec(MLA_HEADS * MLA_V),
                  _resident((MLA_HEADS * MLA_V, D_MODEL))],
        out_specs=_tile_spec(D_MODEL),
        out_shape=jax.ShapeDtypeStruct((n_tiles * TM, D_MODEL), F32),
        compiler_params=_cparams(("arbitrary",)),
        name="mla_out_proj",
    )(hs, mod_l, attn, w_o)


def _hgrn_chunk(q, z, v, lb, tri, st_ref, rev):
    c = HG_C
    sg = jax.nn.sigmoid(z)
    f = lb + (1.0 - lb) * sg
    logf = jnp.log(jnp.maximum(f, F_MIN))
    k = (1.0 - lb) * (1.0 - sg)
    c3 = _dot(tri, jnp.concatenate(_split3(logf), axis=1))
    cum = c3[:, :LANES] + c3[:, LANES:2 * LANES] + c3[:, 2 * LANES:]
    edge = cum[0:1, :] if rev else cum[c - 1:c, :]
    vb = v.astype(BF16)

    st = st_ref[...]
    o = _dot_nt((q * jnp.exp(cum)).astype(BF16), st.astype(BF16))
    k_end = (k * jnp.exp(edge - cum)).astype(BF16)
    st_ref[...] = st * jnp.exp(edge) + _dot_tn(vb, k_end)

    def pair_scores(expo_q, expo_k):
        return _dot_nt((q * jnp.exp(expo_q)).astype(BF16), (k * jnp.exp(expo_k)).astype(BF16))

    mid_row = SUBLANES // 2 if rev else SUBLANES // 2 - 1
    ref8 = jnp.concatenate(
        [jnp.broadcast_to(cum[b * SUBLANES + mid_row:b * SUBLANES + mid_row + 1, :], (SUBLANES, LANES))
         for b in range(c // SUBLANES)], axis=0)
    d8 = cum - ref8
    scores = [pair_scores(d8, -d8)]
    lvl = 3
    while (1 << lvl) < c:
        half = 1 << lvl
        parts = []
        for b in range(c // (2 * half)):
            lo = cum[2 * b * half:(2 * b + 1) * half, :]
            hi = cum[(2 * b + 1) * half:(2 * b + 2) * half, :]
            if rev:
                ref = hi[0:1, :]
                parts += [lo - ref, ref - hi]
            else:
                ref = lo[half - 1:half, :]
                parts += [ref - lo, hi - ref]
        e = jnp.concatenate(parts, axis=0)
        scores.append(pair_scores(e, e))
        lvl += 1

    row = lax.broadcasted_iota(jnp.int32, (c, c), 0)
    col = lax.broadcasted_iota(jnp.int32, (c, c), 1)
    x = row ^ col
    a = scores[-1]
    for j in range(len(scores) - 2, -1, -1):
        a = jnp.where(x < (SUBLANES << j), scores[j], a)
    a = jnp.where((row <= col) if rev else (row >= col), a, 0.0)
    return o + _dot(a.astype(BF16), vb)


def _hgrn_scan_kernel(qf_ref, zf_ref, vf_ref, qb_ref, zb_ref, vb_ref, lb_ref, tri_ref,
                      of_ref, ob_ref, st_ref):
    @pl.when(pl.program_id(1) == 0)
    def _():
        st_ref[...] = jnp.zeros_like(st_ref)

    def head(hd, carry):
        cols = pl.ds(pl.multiple_of(hd * LANES, LANES), LANES)
        of_ref[:, cols] = _hgrn_chunk(qf_ref[:, cols], zf_ref[:, cols], vf_ref[:, cols],
                                      lb_ref[0:1, cols], tri_ref[0], st_ref.at[0, hd], False)
        ob_ref[:, cols] = _hgrn_chunk(qb_ref[:, cols], zb_ref[:, cols], vb_ref[:, cols],
                                      lb_ref[1:2, cols], tri_ref[1], st_ref.at[1, hd], True)
        return carry

    lax.fori_loop(0, HG_HEADS, head, 0)


def _hgrn_scan_call(rows, proj, lb, tri):
    c_chunks = rows.ctx_len // HG_C
    l_chunks = rows.seq // HG_C
    steps = c_chunks + l_chunks
    ctx0 = rows.n_lat // HG_C

    def fwd_blk(b, s):
        return jnp.where(s < c_chunks, ctx0 + b * c_chunks + s, b * l_chunks + s - c_chunks)

    def bwd_blk(b, s):
        return jnp.where(s < c_chunks, ctx0 + b * c_chunks + (c_chunks - 1 - s),
                         b * l_chunks + (l_chunks - 1 - (s - c_chunks)))

    def in_spec(blk, col):
        return pl.BlockSpec((HG_C, HG_HK), lambda b, s: (blk(b, s), col))

    out_sd = jax.ShapeDtypeStruct((rows.n_tok, HG_HV), F32)
    return pl.pallas_call(
        _hgrn_scan_kernel,
        grid=(rows.batch, steps),
        in_specs=[in_spec(fwd_blk, 0), in_spec(fwd_blk, 1), in_spec(fwd_blk, 3),
                  in_spec(bwd_blk, 0), in_spec(bwd_blk, 2), in_spec(bwd_blk, 3),
                  pl.BlockSpec((2, HG_HK), lambda b, s: (0, 0)),
                  pl.BlockSpec((2, HG_C, HG_C), lambda b, s: (0, 0, 0))],
        out_specs=[pl.BlockSpec((HG_C, HG_HV), lambda b, s: (fwd_blk(b, s), 0)),
                   pl.BlockSpec((HG_C, HG_HV), lambda b, s: (bwd_blk(b, s), 0))],
        out_shape=[out_sd, out_sd],
        scratch_shapes=[pltpu.VMEM((2, HG_HEADS, HG_DV, HG_DK), F32)],
        compiler_params=_cparams(("arbitrary", "arbitrary")),
        name="hgrn_scan",
    )(proj, proj, proj, proj, proj, proj, lb, tri)


def _attn_kernel(q_ref, kl_ref, kc_ref, vl_ref, vc_ref, o_ref, *, n_q):
    q = q_ref[...]
    s_c = _dot_nt(q, kc_ref[...])
    m_c = jnp.max(s_c, axis=-1, keepdims=True)

    @pl.when(pl.program_id(2) < n_q)
    def _():
        s_l = _dot_nt(q, kl_ref[...])
        m = jnp.maximum(jnp.max(s_l, axis=-1, keepdims=True), m_c)
        p_l = jnp.exp(s_l - m)
        p_c = jnp.exp(s_c - m)
        den = jnp.sum(p_l, axis=-1, keepdims=True) + jnp.sum(p_c, axis=-1, keepdims=True)
        acc = _dot(p_l.astype(BF16), vl_ref[...]) + _dot(p_c.astype(BF16), vc_ref[...])
        o_ref[...] = (acc / den).astype(o_ref.dtype)

    @pl.when(pl.program_id(2) == n_q)
    def _():
        p_c = jnp.exp(s_c - m_c)
        den = jnp.sum(p_c, axis=-1, keepdims=True)
        o_ref[...] = (_dot(p_c.astype(BF16), vc_ref[...]) / den).astype(o_ref.dtype)


def _attn_call(rows, q, k, v):
    assert rows.ctx_len == TQ and rows.seq % TQ == 0
    n_q = rows.seq // TQ
    ctx0 = rows.n_lat // TQ

    def q_blk(b, qi):
        return jnp.where(qi < n_q, b * n_q + qi, ctx0 + b)

    return pl.pallas_call(
        functools.partial(_attn_kernel, n_q=n_q),
        grid=(rows.batch, MLA_HEADS, n_q + 1),
        in_specs=[pl.BlockSpec((TQ, MLA_HW), lambda b, h, qi: (q_blk(b, qi), h)),
                  pl.BlockSpec((rows.seq, MLA_HW), lambda b, h, qi: (b, h)),
                  pl.BlockSpec((TQ, MLA_HW), lambda b, h, qi: (ctx0 + b, h)),
                  pl.BlockSpec((rows.seq, MLA_V), lambda b, h, qi: (b, h)),
                  pl.BlockSpec((TQ, MLA_V), lambda b, h, qi: (ctx0 + b, h))],
        out_specs=pl.BlockSpec((TQ, MLA_V), lambda b, h, qi: (q_blk(b, qi), h)),
        out_shape=jax.ShapeDtypeStruct((rows.n_tok, MLA_HEADS * MLA_V), BF16),
        compiler_params=_cparams(("arbitrary", "arbitrary", "arbitrary")),
        name="mla_attention",
    )(q, k, k, v, v)


def _rope_swap_cols():
    idx = np.arange(MLA_ROPE).reshape(2, 2, MLA_ROPE // 4)
    return idx[:, ::-1, :].reshape(-1)


def _mla_weights(w_down, w_uq, w_ukv):
    swap = _rope_swap_cols()
    kr = w_down[:, Q_LORA + KV_LORA:]
    zpad = jnp.zeros((D_MODEL, LANES - MLA_ROPE), w_down.dtype)
    wd = jnp.concatenate([w_down[:, :Q_LORA + KV_LORA], kr, zpad, kr[:, swap], zpad], axis=1)
    uq = w_uq.reshape(Q_LORA, MLA_HEADS, MLA_QK)
    nope, rope = uq[..., :MLA_NOPE], uq[..., MLA_NOPE:]
    z64 = jnp.zeros((Q_LORA, MLA_HEADS, MLA_HW - MLA_QK), w_uq.dtype)
    z128 = jnp.zeros((Q_LORA, MLA_HEADS, MLA_NOPE), w_uq.dtype)
    wq = jnp.concatenate([nope, rope, z64], axis=-1).reshape(Q_LORA, MLA_HEADS * MLA_HW)
    wqs = jnp.concatenate([z128, rope[..., swap], z64], axis=-1).reshape(Q_LORA, MLA_HEADS * MLA_HW)
    ukv = w_ukv.reshape(KV_LORA, MLA_HEADS, MLA_NOPE + MLA_V)
    wkv = jnp.concatenate([ukv[..., :MLA_NOPE].reshape(KV_LORA, -1),
                           ukv[..., MLA_NOPE:].reshape(KV_LORA, -1)], axis=1)
    return wd.astype(BF16), wq.astype(BF16), wqs.astype(BF16), wkv.astype(BF16)


def _rope_tables(seq):
    n_rows = seq // GRID_W
    row = jnp.repeat(jnp.arange(n_rows), GRID_W).astype(F32)
    col = jnp.tile(jnp.arange(GRID_W), n_rows).astype(F32)
    axis_dims = MLA_ROPE // 2
    inv = 1.0 / (ROPE_BASE ** (jnp.arange(0, axis_dims, 2, dtype=F32) / axis_dims))
    ang_r, ang_c = row[:, None] * inv, col[:, None] * inv
    cos64 = jnp.concatenate([jnp.cos(ang_r)] * 2 + [jnp.cos(ang_c)] * 2, axis=1)
    sin64 = jnp.concatenate([-jnp.sin(ang_r), jnp.sin(ang_r), -jnp.sin(ang_c), jnp.sin(ang_c)], axis=1)
    cos64 = jnp.concatenate([cos64, jnp.ones((TM, MLA_ROPE), F32)], axis=0)
    sin64 = jnp.concatenate([sin64, jnp.zeros((TM, MLA_ROPE), F32)], axis=0)
    n = seq + TM
    cos_t = jnp.concatenate([jnp.ones((n, MLA_NOPE), F32), cos64, jnp.zeros((n, MLA_HW - MLA_QK), F32)], axis=1)
    sin_t = jnp.concatenate([jnp.zeros((n, MLA_NOPE), F32), sin64, jnp.zeros((n, MLA_HW - MLA_QK), F32)], axis=1)
    return cos_t, sin_t


def kernel(x, c, ctx, c_ctx, mod_w, mod_b, norm_g, ffn_w_gate, ffn_w_up, ffn_w_down, hg_w_in, hg_w_out,
           hg_gn, hg_lb_logits, mla_w_down, mla_q_norm, mla_w_uq, mla_kv_norm, mla_w_ukv, mla_w_o, final_g):
    batch, seq, _ = x.shape
    rows = _Rows(batch, seq, ctx.shape[1])
    hs = jnp.concatenate([x.reshape(rows.n_lat, D_MODEL), ctx.reshape(rows.n_ctx, D_MODEL)], axis=0)

    mod_rows = -(-(batch + 1) // SUBLANES) * SUBLANES
    cond = jnp.concatenate([c, c_ctx[None], jnp.zeros((mod_rows - batch - 1, D_MODEL), F32)], axis=0)
    mod = _mod_call(cond, mod_w, mod_b)

    p_lb = jax.nn.softmax(hg_lb_logits.astype(F32), axis=0)
    lbs = jnp.maximum(jnp.cumsum(p_lb, axis=0) - p_lb[0:1], 0.0)
    tri = jnp.stack([jnp.tril(jnp.ones((HG_C, HG_C), F32)), jnp.triu(jnp.ones((HG_C, HG_C), F32))]).astype(BF16)
    head_id = np.arange(HG_HV) // HG_DV
    bd = jnp.asarray(head_id[:, None] == head_id[None, :], BF16)
    cos_t, sin_t = _rope_tables(seq)

    wg, wu, wd = ffn_w_gate.astype(BF16), ffn_w_up.astype(BF16), ffn_w_down.astype(BF16)

    for i in range(DEPTH):
        last = i == DEPTH - 1
        j = i // 2
        mod_l = mod[i].reshape(mod_rows, 1, N_MOD * D_MODEL)
        g = norm_g[i].reshape(3, 1, D_MODEL)
        hs = _ffn_call(rows, hs, mod_l, g[0], wg[i, 0], wu[i, 0], wd[i, 0], 0, rows.all_tiles)
        n_out = rows.lat_tiles if last else rows.all_tiles
        if i % 2 == 0:
            proj = _hgrn_in_call(rows, hs, mod_l, g[1], hg_w_in[j].astype(BF16))
            o_f, o_b = _hgrn_scan_call(rows, proj, lbs[j], tri)
            gn = jnp.tile(hg_gn[j], HG_HEADS).reshape(1, HG_HV)
            hs = _hgrn_out_call(rows, hs, mod_l, o_f, o_b, proj, gn, bd, hg_w_out[j].astype(BF16), n_out)
        else:
            w_dn, w_q, w_qs, w_kv = _mla_weights(mla_w_down[j], mla_w_uq[j], mla_w_ukv[j])
            q, k, v = _mla_proj_call(rows, hs, mod_l, g[1], w_dn, mla_q_norm[j].reshape(1, Q_LORA),
                                     mla_kv_norm[j].reshape(1, KV_LORA), w_q, w_qs, w_kv, cos_t, sin_t)
            attn = _attn_call(rows, q, k, v)
            hs = _mla_out_call(rows, hs, mod_l, attn, mla_w_o[j].astype(BF16), n_out)
        hs = _ffn_call(rows, hs, mod_l, g[2], wg[i, 1], wu[i, 1], wd[i, 1], 6, n_out,
                       final_g=final_g.reshape(1, D_MODEL) if last else None)
    return hs.reshape(batch, seq, D_MODEL)
```

```python
import functools

import jax
import jax.numpy as jnp
import numpy as np
from jax import lax
from jax.experimental import pallas as pl
from jax.experimental.pallas import tpu as pltpu

F32 = jnp.float32
BF16 = jnp.bfloat16

D_MODEL = 1024
DEPTH = 4
GRID_W = 64
N_MOD = 9
FFN_HIDDEN = 2816
RMS_EPS = 1e-6
HG_HEADS = 8
HG_DK = 128
HG_DV = 128
HG_HK = HG_HEADS * HG_DK
HG_HV = HG_HEADS * HG_DV
HG_IN = 3 * HG_HK + HG_HV + D_MODEL
F_MIN = 1e-6
MLA_HEADS = 8
MLA_NOPE = 128
MLA_ROPE = 64
MLA_V = 128
Q_LORA = 384
KV_LORA = 256
MLA_QK = MLA_NOPE + MLA_ROPE
MLA_SCALE = MLA_QK ** -0.5
ROPE_BASE = 10000.0
LOG2_E = 1.4426950408889634

LANES = 128
SUBLANES = 8
MXU_DIM = 256
VMEM_LIMIT = 56 * 1024 * 1024

TM = 512
FFN_CHUNK = MXU_DIM
HG_C = 128
TQ = 256
Q_SUB = 4
KEY_CHUNK = 512
MLA_HW = 2 * LANES
MLA_DOWN_P = Q_LORA + KV_LORA + 2 * LANES


def _cparams(sem):
    return pltpu.CompilerParams(dimension_semantics=sem, vmem_limit_bytes=VMEM_LIMIT)


def _resident(shape):
    nd = len(shape)
    return pl.BlockSpec(shape, lambda *_: (0,) * nd, pipeline_mode=pl.Buffered(1))


def _dot(a, b):
    return jnp.dot(a, b, preferred_element_type=F32)


def _dot_nt(a, b):
    return lax.dot_general(a, b, (((1,), (1,)), ((), ())), preferred_element_type=F32)


def _dot_tn(a, b):
    return lax.dot_general(a, b, (((0,), (0,)), ((), ())), preferred_element_type=F32)


def _split3(x):
    hi = x.astype(BF16)
    r1 = x - hi.astype(F32)
    mid = r1.astype(BF16)
    lo = (r1 - mid.astype(F32)).astype(BF16)
    return hi, mid, lo


def _rms_modulate(h, g, shift, scale):
    ms = jnp.mean(h * h, axis=-1, keepdims=True)
    return (h * lax.rsqrt(ms + RMS_EPS) * g) * (1.0 + scale) + shift


def _mod_slices(mod_ref, base, n):
    m = mod_ref[0]
    return [m[:, (base + j) * D_MODEL:(base + j + 1) * D_MODEL] for j in range(n)]


def _mod_kernel(c_ref, w_ref, b_ref, o_ref):
    c = c_ref[...]
    s = c * jax.nn.sigmoid(c)
    s_hi = s.astype(BF16)
    s_lo = (s - s_hi.astype(F32)).astype(BF16)
    w = w_ref[0]
    w_hi = w.astype(BF16)
    w_lo = (w - w_hi.astype(F32)).astype(BF16)
    acc = _dot(s_hi, w_hi) + (_dot(s_lo, w_hi) + _dot(s_hi, w_lo))
    o_ref[0] = acc + b_ref[0]


def _mod_call(cond, mod_w, mod_b):
    rows = cond.shape[0]
    tn = D_MODEL
    n_out = N_MOD * D_MODEL
    return pl.pallas_call(
        _mod_kernel,
        grid=(DEPTH, n_out // tn),
        in_specs=[
            pl.BlockSpec((rows, D_MODEL), lambda l, n: (0, 0)),
            pl.BlockSpec((1, D_MODEL, tn), lambda l, n: (l, 0, n)),
            pl.BlockSpec((1, 1, tn), lambda l, n: (l, 0, n)),
        ],
        out_specs=pl.BlockSpec((1, rows, tn), lambda l, n: (l, 0, n)),
        out_shape=jax.ShapeDtypeStruct((DEPTH, rows, n_out), F32),
        compiler_params=_cparams(("arbitrary", "arbitrary")),
        name="mod_vectors",
    )(cond, mod_w, mod_b.reshape(DEPTH, 1, n_out))


class _Rows:
    def __init__(self, batch, seq, ctx_len):
        self.batch, self.seq, self.ctx_len = batch, seq, ctx_len
        self.n_lat = batch * seq
        self.n_ctx = batch * ctx_len
        self.n_tok = self.n_lat + self.n_ctx
        assert seq % TM == 0 and self.n_ctx % TM == 0
        self.lat_tiles = self.n_lat // TM
        self.tiles_per_seq = seq // TM
        self.all_tiles = self.n_tok // TM

    def mod_spec(self):
        lat_tiles, tps, ctx_row = self.lat_tiles, self.tiles_per_seq, self.batch
        return pl.BlockSpec(
            (1, 1, N_MOD * D_MODEL),
            lambda i: (jnp.where(i < lat_tiles, i // tps, ctx_row), 0, 0))


def _tile_spec(width, col=0):
    return pl.BlockSpec((TM, width), lambda i: (i, col))


def _ffn_kernel(h_ref, mod_ref, g_ref, wg_ref, wu_ref, wd_ref, *rest, base, final):
    o_ref = rest[-1]
    h = h_ref[...]
    shift, scale, gate = _mod_slices(mod_ref, base, 3)
    a = _rms_modulate(h, g_ref[...], shift, scale).astype(BF16)
    acc = jnp.zeros((TM, D_MODEL), F32)
    for j in range(FFN_HIDDEN // FFN_CHUNK):
        sl = slice(j * FFN_CHUNK, (j + 1) * FFN_CHUNK)
        gg = _dot(a, wg_ref[:, sl])
        uu = _dot(a, wu_ref[:, sl])
        hm = (gg * jax.nn.sigmoid(gg) * uu).astype(BF16)
        acc = acc + _dot(hm, wd_ref[sl, :])
    out = h + (0.5 * gate) * acc
    if final:
        fg_ref = rest[0]
        ms = jnp.mean(out * out, axis=-1, keepdims=True)
        out = out * lax.rsqrt(ms + RMS_EPS) * fg_ref[...]
    o_ref[...] = out


def _ffn_call(rows, hs, mod_l, g, wg, wu, wd, base, n_tiles, final_g=None):
    final = final_g is not None
    in_specs = [
        _tile_spec(D_MODEL), rows.mod_spec(), _resident((1, D_MODEL)),
        _resident((D_MODEL, FFN_HIDDEN)), _resident((D_MODEL, FFN_HIDDEN)),
        _resident((FFN_HIDDEN, D_MODEL)),
    ]
    args = [hs, mod_l, g, wg, wu, wd]
    if final:
        in_specs.append(_resident((1, D_MODEL)))
        args.append(final_g)
    return pl.pallas_call(
        functools.partial(_ffn_kernel, base=base, final=final),
        grid=(n_tiles,),
        in_specs=in_specs,
        out_specs=_tile_spec(D_MODEL),
        out_shape=jax.ShapeDtypeStruct((n_tiles * TM, D_MODEL), F32),
        compiler_params=_cparams(("arbitrary",)),
        name="ffn_half",
    )(*args)


def _hgrn_in_kernel(h_ref, mod_ref, g_ref, w_ref, o_ref):
    shift, scale = _mod_slices(mod_ref, 3, 2)
    a = _rms_modulate(h_ref[...], g_ref[...], shift, scale).astype(BF16)
    for j in range(HG_IN // D_MODEL):
        sl = slice(j * D_MODEL, (j + 1) * D_MODEL)
        o_ref[:, sl] = _dot(a, w_ref[:, sl])


def _hgrn_in_call(rows, hs, mod_l, g, w_in):
    return pl.pallas_call(
        _hgrn_in_kernel,
        grid=(rows.all_tiles,),
        in_specs=[_tile_spec(D_MODEL), rows.mod_spec(), _resident((1, D_MODEL)),
                  _resident((D_MODEL, HG_IN))],
        out_specs=_tile_spec(HG_IN),
        out_shape=jax.ShapeDtypeStruct((rows.n_tok, HG_IN), F32),
        compiler_params=_cparams(("arbitrary",)),
        name="hgrn_in_proj",
    )(hs, mod_l, g, w_in)


def _hgrn_out_kernel(h_ref, mod_ref, of_ref, ob_ref, gate_ref, gn_ref, bd_ref, w_ref, o_ref):
    (res_gate,) = _mod_slices(mod_ref, 5, 1)
    o = of_ref[...] + ob_ref[...]
    sq = o * o
    sq_hi = sq.astype(BF16)
    sq_lo = (sq - sq_hi.astype(F32)).astype(BF16)
    ss = _dot(sq_hi, bd_ref[...]) + _dot(sq_lo, bd_ref[...])
    on = o * lax.rsqrt(ss * (1.0 / HG_DV) + RMS_EPS) * gn_ref[...]
    gt = gate_ref[...]
    y = _dot((on * (gt * jax.nn.sigmoid(gt))).astype(BF16), w_ref[...])
    o_ref[...] = h_ref[...] + res_gate * y


def _hgrn_out_call(rows, hs, mod_l, o_f, o_b, proj, gn, bd, w_out, n_tiles):
    return pl.pallas_call(
        _hgrn_out_kernel,
        grid=(n_tiles,),
        in_specs=[_tile_spec(D_MODEL), rows.mod_spec(), _tile_spec(HG_HV), _tile_spec(HG_HV),
                  _tile_spec(D_MODEL, col=(3 * HG_HK + HG_HV) // D_MODEL),
                  _resident((1, HG_HV)), _resident((HG_HV, HG_HV)), _resident((HG_HV, D_MODEL))],
        out_specs=_tile_spec(D_MODEL),
        out_shape=jax.ShapeDtypeStruct((n_tiles * TM, D_MODEL), F32),
        compiler_params=_cparams(("arbitrary",)),
        name="hgrn_readout",
    )(hs, mod_l, o_f, o_b, proj, gn, bd, w_out)


def _mla_proj_kernel(h_ref, mod_ref, g_ref, wd_ref, qn_ref, kvn_ref, wq_ref, wqs_ref, wk_ref, wvt_ref,
                     cos_ref, sin_ref, q_ref, k_ref, vt_ref):
    shift, scale = _mod_slices(mod_ref, 3, 2)
    a = _rms_modulate(h_ref[...], g_ref[...], shift, scale).astype(BF16)
    dp = _dot(a, wd_ref[...])
    cq = dp[:, :Q_LORA]
    ckv = dp[:, Q_LORA:Q_LORA + KV_LORA]
    kr_a = dp[:, Q_LORA + KV_LORA:Q_LORA + KV_LORA + LANES]
    kr_b = dp[:, Q_LORA + KV_LORA + LANES:]
    cqn = (cq * lax.rsqrt(jnp.mean(cq * cq, axis=-1, keepdims=True) + RMS_EPS) * qn_ref[...]).astype(BF16)
    ckvn = (ckv * lax.rsqrt(jnp.mean(ckv * ckv, axis=-1, keepdims=True) + RMS_EPS) * kvn_ref[...]).astype(BF16)
    cos = cos_ref[...]
    sin = sin_ref[...]
    cos_q = jnp.concatenate([cos * (MLA_SCALE * LOG2_E)] * MLA_HEADS, axis=1)
    sin_q = jnp.concatenate([sin * (MLA_SCALE * LOG2_E)] * MLA_HEADS, axis=1)
    q_ref[...] = (_dot(cqn, wq_ref[...]) * cos_q + _dot(cqn, wqs_ref[...]) * sin_q).astype(BF16)
    kr = (kr_a * cos[:, LANES:] + kr_b * sin[:, LANES:]).astype(BF16)
    kn = _dot(ckvn, wk_ref[...]).astype(BF16)
    pieces = []
    for hh in range(MLA_HEADS):
        pieces += [kn[:, hh * MLA_NOPE:(hh + 1) * MLA_NOPE], kr]
    k_ref[...] = jnp.concatenate(pieces, axis=1)
    vt_ref[...] = _dot_nt(wvt_ref[...], ckvn).astype(BF16)


def _mla_proj_call(rows, hs, mod_l, g, wd, qn, kvn, wq, wqs, wk, wvt, cos_t, sin_t):
    lat_tiles, tps = rows.lat_tiles, rows.tiles_per_seq
    rope_spec = pl.BlockSpec((TM, MLA_HW), lambda i: (jnp.where(i < lat_tiles, i % tps, tps), 0))
    qk_w = MLA_HEADS * MLA_HW
    v_w = MLA_HEADS * MLA_V
    return pl.pallas_call(
        _mla_proj_kernel,
        grid=(rows.all_tiles,),
        in_specs=[_tile_spec(D_MODEL), rows.mod_spec(), _resident((1, D_MODEL)),
                  _resident((D_MODEL, MLA_DOWN_P)), _resident((1, Q_LORA)), _resident((1, KV_LORA)),
                  _resident((Q_LORA, qk_w)), _resident((Q_LORA, qk_w)),
                  _resident((KV_LORA, MLA_HEADS * MLA_NOPE)), _resident((v_w, KV_LORA)),
                  rope_spec, rope_spec],
        out_specs=[_tile_spec(qk_w), _tile_spec(qk_w), pl.BlockSpec((v_w, TM), lambda i: (0, i))],
        out_shape=[jax.ShapeDtypeStruct((rows.n_tok, qk_w), BF16),
                   jax.ShapeDtypeStruct((rows.n_tok, qk_w), BF16),
                   jax.ShapeDtypeStruct((v_w, rows.n_tok), BF16)],
        compiler_params=_cparams(("arbitrary",)),
        name="mla_proj",
    )(hs, mod_l, g, wd, qn, kvn, wq, wqs, wk, wvt, cos_t, sin_t)


def _mla_out_kernel(h_ref, mod_ref, al_ref, *rest, lat_tiles):
    w_ref, o_ref = rest[-2:]
    (res_gate,) = _mod_slices(mod_ref, 5, 1)

    def project(a_ref):
        o_ref[...] = h_ref[...] + res_gate * _dot(a_ref[...], w_ref[...])

    if len(rest) == 2:
        project(al_ref)
    else:
        pl.when(pl.program_id(0) < lat_tiles)(lambda: project(al_ref))
        pl.when(pl.program_id(0) >= lat_tiles)(lambda: project(rest[0]))


def _mla_out_call(rows, hs, mod_l, attn_lat, attn_ctx, w_o, n_tiles):
    lat_tiles = rows.lat_tiles
    v_w = MLA_HEADS * MLA_V
    in_specs = [_tile_spec(D_MODEL), rows.mod_spec(),
                pl.BlockSpec((TM, v_w), lambda i: (jnp.minimum(i, lat_tiles - 1), 0))]
    args = [hs, mod_l, attn_lat]
    if attn_ctx is not None:
        in_specs.append(pl.BlockSpec((TM, v_w), lambda i: (jnp.maximum(i - lat_tiles, 0), 0)))
        args.append(attn_ctx)
    return pl.pallas_call(
        functools.partial(_mla_out_kernel, lat_tiles=lat_tiles),
        grid=(n_tiles,),
        in_specs=in_specs + [_resident((v_w, D_MODEL))],
        out_specs=_tile_spec(D_MODEL),
        out_shape=jax.ShapeDtypeStruct((n_tiles * TM, D_MODEL), F32),
        compiler_params=_cparams(("arbitrary",)),
        name="mla_out_proj",
    )(*args, w_o)


def _hgrn_dir(q, z, v, lb, tri, st_ref, out_ref, rev):
    c = HG_C
    width = q.shape[1]
    pair_w = 2 * LANES
    sg = jax.nn.sigmoid(z)
    f = lb + (1.0 - lb) * sg
    lg = jnp.log2(jnp.maximum(f, F_MIN))
    k = (1.0 - lb) * (1.0 - sg)
    c3 = _dot(tri, jnp.concatenate(_split3(lg), axis=1))
    cum = c3[:, :width] + c3[:, width:2 * width] + c3[:, 2 * width:]
    edge = cum[0:1, :] if rev else cum[c - 1:c, :]
    vb = v.astype(BF16)
    q_in = (q * jnp.exp2(cum)).astype(BF16)
    k_end = (k * jnp.exp2(edge - cum)).astype(BF16)
    dec = jnp.exp2(edge)

    mid_row = SUBLANES // 2 if rev else SUBLANES // 2 - 1
    ref8 = jnp.concatenate(
        [jnp.broadcast_to(cum[b * SUBLANES + mid_row:b * SUBLANES + mid_row + 1, :], (SUBLANES, width))
         for b in range(c // SUBLANES)], axis=0)
    d8 = cum - ref8
    q_lv = [(q * jnp.exp2(d8)).astype(BF16)]
    k_lv = [(k * jnp.exp2(-d8)).astype(BF16)]
    lvl = 3
    while (1 << lvl) < c:
        half = 1 << lvl
        parts = []
        for b in range(c // (2 * half)):
            lo = cum[2 * b * half:(2 * b + 1) * half, :]
            hi = cum[(2 * b + 1) * half:(2 * b + 2) * half, :]
            if rev:
                ref = hi[0:1, :]
                parts += [lo - ref, ref - hi]
            else:
                ref = lo[half - 1:half, :]
                parts += [ref - lo, hi - ref]
        w = jnp.exp2(jnp.concatenate(parts, axis=0))
        q_lv.append((q * w).astype(BF16))
        k_lv.append((k * w).astype(BF16))
        lvl += 1

    row = lax.broadcasted_iota(jnp.int32, (c, pair_w), 0)
    col = lax.broadcasted_iota(jnp.int32, (c, pair_w), 1) & (LANES - 1)
    x = row ^ col
    causal = (row <= col) if rev else (row >= col)
    same_head = ((lax.broadcasted_iota(jnp.int32, (pair_w, pair_w), 0) >= LANES)
                 == (lax.broadcasted_iota(jnp.int32, (pair_w, pair_w), 1) >= LANES))
    zero = jnp.zeros((c, LANES), BF16)

    def block_diag(m):
        return jnp.concatenate([jnp.concatenate([m[:, :LANES], zero], axis=1),
                                jnp.concatenate([zero, m[:, LANES:]], axis=1)], axis=0)

    for p in range(width // pair_w):
        cols = slice(p * pair_w, (p + 1) * pair_w)
        st = st_ref[p]
        o = _dot_nt(q_in[:, cols], st.astype(BF16))
        upd = _dot_tn(vb[:, cols], k_end[:, cols])
        st_ref[p] = jnp.where(same_head, st * dec[:, cols] + upd, 0.0)
        a = _dot_nt(q_lv[-1][:, cols], block_diag(k_lv[-1][:, cols]))
        for j in range(len(q_lv) - 2, -1, -1):
            s_j = _dot_nt(q_lv[j][:, cols], block_diag(k_lv[j][:, cols]))
            a = jnp.where(x < (SUBLANES << j), s_j, a)
        a = jnp.where(causal, a, 0.0)
        out_ref[:, cols] = o + _dot(a.astype(BF16), block_diag(vb[:, cols]))


def _hgrn_scan_kernel(qf_ref, zf_ref, vf_ref, qb_ref, zb_ref, vb_ref, lb_ref, tri_ref,
                      of_ref, ob_ref, st_ref):
    @pl.when(pl.program_id(1) == 0)
    def _():
        st_ref[...] = jnp.zeros_like(st_ref)

    _hgrn_dir(qf_ref[...], zf_ref[...], vf_ref[...], lb_ref[0:1, :], tri_ref[0], st_ref.at[0], of_ref, False)
    _hgrn_dir(qb_ref[...], zb_ref[...], vb_ref[...], lb_ref[1:2, :], tri_ref[1], st_ref.at[1], ob_ref, True)


def _hgrn_scan_call(rows, proj, lb, tri):
    c_chunks = rows.ctx_len // HG_C
    l_chunks = rows.seq // HG_C
    steps = c_chunks + l_chunks
    ctx0 = rows.n_lat // HG_C

    def fwd_blk(b, s):
        return jnp.where(s < c_chunks, ctx0 + b * c_chunks + s, b * l_chunks + s - c_chunks)

    def bwd_blk(b, s):
        return jnp.where(s < c_chunks, ctx0 + b * c_chunks + (c_chunks - 1 - s),
                         b * l_chunks + (l_chunks - 1 - (s - c_chunks)))

    def in_spec(blk, col):
        return pl.BlockSpec((HG_C, HG_HK), lambda b, s: (blk(b, s), col))

    out_sd = jax.ShapeDtypeStruct((rows.n_tok, HG_HV), F32)
    return pl.pallas_call(
        _hgrn_scan_kernel,
        grid=(rows.batch, steps),
        in_specs=[in_spec(fwd_blk, 0), in_spec(fwd_blk, 1), in_spec(fwd_blk, 3),
                  in_spec(bwd_blk, 0), in_spec(bwd_blk, 2), in_spec(bwd_blk, 3),
                  pl.BlockSpec((2, HG_HK), lambda b, s: (0, 0)),
                  pl.BlockSpec((2, HG_C, HG_C), lambda b, s: (0, 0, 0))],
        out_specs=[pl.BlockSpec((HG_C, HG_HV), lambda b, s: (fwd_blk(b, s), 0)),
                   pl.BlockSpec((HG_C, HG_HV), lambda b, s: (bwd_blk(b, s), 0))],
        out_shape=[out_sd, out_sd],
        scratch_shapes=[pltpu.VMEM((2, HG_HEADS // 2, 2 * HG_DV, 2 * HG_DK), F32)],
        compiler_params=_cparams(("arbitrary", "arbitrary")),
        name="hgrn_scan",
    )(proj, proj, proj, proj, proj, proj, lb, tri)


def _attn_tile(q, kc_ref, vtc_ref, kl_ref=None, vtl_ref=None):
    s_c = _dot_nt(kc_ref[...], q)
    m = jnp.max(s_c, axis=0, keepdims=True)
    s_l, chunks = [], []
    if kl_ref is not None:
        chunks = [pl.ds(j * KEY_CHUNK, KEY_CHUNK) for j in range(kl_ref.shape[0] // KEY_CHUNK)]
        s_l = [_dot_nt(kl_ref[ch, :], q) for ch in chunks]
        for s in s_l:
            m = jnp.maximum(m, jnp.max(s, axis=0, keepdims=True))
    p_c = jnp.exp2(s_c - m)
    den = jnp.sum(p_c, axis=0, keepdims=True)
    acc = _dot(vtc_ref[...], p_c.astype(BF16))
    for s, ch in zip(s_l, chunks):
        p = jnp.exp2(s - m)
        den = den + jnp.sum(p, axis=0, keepdims=True)
        acc = acc + _dot(vtl_ref[:, ch], p.astype(BF16))
    return (acc / den).T


def _attn_lat_kernel(q_ref, kl_ref, kc_ref, vtl_ref, vtc_ref, o_ref):
    for j in range(Q_SUB):
        rws = pl.ds(j * TQ, TQ)
        o_ref[rws, :] = _attn_tile(q_ref[rws, :], kc_ref, vtc_ref, kl_ref, vtl_ref).astype(o_ref.dtype)


def _attn_ctx_kernel(q_ref, kc_ref, vtc_ref, o_ref):
    o_ref[...] = _attn_tile(q_ref[...], kc_ref, vtc_ref).astype(o_ref.dtype)


def _attn_call(rows, q, k, vt, with_ctx):
    tq = Q_SUB * TQ
    assert rows.ctx_len == TQ and rows.seq % tq == 0
    n_q = rows.seq // tq
    ctx0 = rows.n_lat // TQ
    v_w = MLA_HEADS * MLA_V
    ctx_k = pl.BlockSpec((TQ, MLA_HW), lambda b, h, *_: (ctx0 + b, h))
    ctx_vt = pl.BlockSpec((MLA_V, TQ), lambda b, h, *_: (h, ctx0 + b))
    lat = pl.pallas_call(
        _attn_lat_kernel,
        grid=(rows.batch, MLA_HEADS, n_q),
        in_specs=[pl.BlockSpec((tq, MLA_HW), lambda b, h, qi: (b * n_q + qi, h)),
                  pl.BlockSpec((rows.seq, MLA_HW), lambda b, h, qi: (b, h)), ctx_k,
                  pl.BlockSpec((MLA_V, rows.seq), lambda b, h, qi: (h, b)), ctx_vt],
        out_specs=pl.BlockSpec((tq, MLA_V), lambda b, h, qi: (b * n_q + qi, h)),
        out_shape=jax.ShapeDtypeStruct((rows.n_lat, v_w), BF16),
        compiler_params=_cparams(("arbitrary", "arbitrary", "arbitrary")),
        name="mla_attention",
    )(q, k, k, vt, vt)
    if not with_ctx:
        return lat, None
    ctx = pl.pallas_call(
        _attn_ctx_kernel,
        grid=(rows.batch, MLA_HEADS),
        in_specs=[ctx_k, ctx_k, ctx_vt],
        out_specs=pl.BlockSpec((TQ, MLA_V), lambda b, h: (b, h)),
        out_shape=jax.ShapeDtypeStruct((rows.n_ctx, v_w), BF16),
        compiler_params=_cparams(("arbitrary", "arbitrary")),
        name="mla_attention_ctx",
    )(q, k, vt)
    return lat, ctx


def _rope_swap_cols():
    idx = np.arange(MLA_ROPE).reshape(2, 2, MLA_ROPE // 4)
    return idx[:, ::-1, :].reshape(-1)


def _mla_weights(w_down, w_uq, w_ukv):
    swap = _rope_swap_cols()
    kr = w_down[:, Q_LORA + KV_LORA:]
    zpad = jnp.zeros((D_MODEL, LANES - MLA_ROPE), w_down.dtype)
    wd = jnp.concatenate([w_down[:, :Q_LORA + KV_LORA], kr, zpad, kr[:, swap], zpad], axis=1)
    uq = w_uq.reshape(Q_LORA, MLA_HEADS, MLA_QK)
    nope, rope = uq[..., :MLA_NOPE], uq[..., MLA_NOPE:]
    z64 = jnp.zeros((Q_LORA, MLA_HEADS, MLA_HW - MLA_QK), w_uq.dtype)
    z128 = jnp.zeros((Q_LORA, MLA_HEADS, MLA_NOPE), w_uq.dtype)
    wq = jnp.concatenate([nope, rope, z64], axis=-1).reshape(Q_LORA, MLA_HEADS * MLA_HW)
    wqs = jnp.concatenate([z128, rope[..., swap], z64], axis=-1).reshape(Q_LORA, MLA_HEADS * MLA_HW)
    ukv = w_ukv.reshape(KV_LORA, MLA_HEADS, MLA_NOPE + MLA_V)
    wk = ukv[..., :MLA_NOPE].reshape(KV_LORA, MLA_HEADS * MLA_NOPE)
    wvt = ukv[..., MLA_NOPE:].reshape(KV_LORA, MLA_HEADS * MLA_V).T
    return wd.astype(BF16), wq.astype(BF16), wqs.astype(BF16), wk.astype(BF16), wvt.astype(BF16)


def _rope_tables(seq):
    n_rows = seq // GRID_W
    row = jnp.repeat(jnp.arange(n_rows), GRID_W).astype(F32)
    col = jnp.tile(jnp.arange(GRID_W), n_rows).astype(F32)
    axis_dims = MLA_ROPE // 2
    inv = 1.0 / (ROPE_BASE ** (jnp.arange(0, axis_dims, 2, dtype=F32) / axis_dims))
    ang_r, ang_c = row[:, None] * inv, col[:, None] * inv
    cos64 = jnp.concatenate([jnp.cos(ang_r)] * 2 + [jnp.cos(ang_c)] * 2, axis=1)
    sin64 = jnp.concatenate([-jnp.sin(ang_r), jnp.sin(ang_r), -jnp.sin(ang_c), jnp.sin(ang_c)], axis=1)
    cos64 = jnp.concatenate([cos64, jnp.ones((TM, MLA_ROPE), F32)], axis=0)
    sin64 = jnp.concatenate([sin64, jnp.zeros((TM, MLA_ROPE), F32)], axis=0)
    n = seq + TM
    cos_t = jnp.concatenate([jnp.ones((n, MLA_NOPE), F32), cos64, jnp.zeros((n, MLA_HW - MLA_QK), F32)], axis=1)
    sin_t = jnp.concatenate([jnp.zeros((n, MLA_NOPE), F32), sin64, jnp.zeros((n, MLA_HW - MLA_QK), F32)], axis=1)
    return cos_t, sin_t


def kernel(x, c, ctx, c_ctx, mod_w, mod_b, norm_g, ffn_w_gate, ffn_w_up, ffn_w_down, hg_w_in, hg_w_out,
           hg_gn, hg_lb_logits, mla_w_down, mla_q_norm, mla_w_uq, mla_kv_norm, mla_w_ukv, mla_w_o, final_g):
    batch, seq, _ = x.shape
    rows = _Rows(batch, seq, ctx.shape[1])
    hs = jnp.concatenate([x.reshape(rows.n_lat, D_MODEL), ctx.reshape(rows.n_ctx, D_MODEL)], axis=0)

    mod_rows = -(-(batch + 1) // SUBLANES) * SUBLANES
    cond = jnp.concatenate([c, c_ctx[None], jnp.zeros((mod_rows - batch - 1, D_MODEL), F32)], axis=0)
    mod = _mod_call(cond, mod_w, mod_b)

    p_lb = jax.nn.softmax(hg_lb_logits.astype(F32), axis=0)
    lbs = jnp.maximum(jnp.cumsum(p_lb, axis=0) - p_lb[0:1], 0.0)
    tri = jnp.stack([jnp.tril(jnp.ones((HG_C, HG_C), F32)), jnp.triu(jnp.ones((HG_C, HG_C), F32))]).astype(BF16)
    head_id = np.arange(HG_HV) // HG_DV
    bd = jnp.asarray(head_id[:, None] == head_id[None, :], BF16)
    cos_t, sin_t = _rope_tables(seq)

    wg, wu, wd = ffn_w_gate.astype(BF16), ffn_w_up.astype(BF16), ffn_w_down.astype(BF16)

    for i in range(DEPTH):
        last = i == DEPTH - 1
        j = i // 2
        mod_l = mod[i].reshape(mod_rows, 1, N_MOD * D_MODEL)
        g = norm_g[i].reshape(3, 1, D_MODEL)
        hs = _ffn_call(rows, hs, mod_l, g[0], wg[i, 0], wu[i, 0], wd[i, 0], 0, rows.all_tiles)
        n_out = rows.lat_tiles if last else rows.all_tiles
        if i % 2 == 0:
            proj = _hgrn_in_call(rows, hs, mod_l, g[1], hg_w_in[j].astype(BF16))
            o_f, o_b = _hgrn_scan_call(rows, proj, lbs[j], tri)
            gn = jnp.tile(hg_gn[j], HG_HEADS).reshape(1, HG_HV)
            hs = _hgrn_out_call(rows, hs, mod_l, o_f, o_b, proj, gn, bd, hg_w_out[j].astype(BF16), n_out)
        else:
            w_dn, w_q, w_qs, w_k, w_vt = _mla_weights(mla_w_down[j], mla_w_uq[j], mla_w_ukv[j])
            q, k, vt = _mla_proj_call(rows, hs, mod_l, g[1], w_dn, mla_q_norm[j].reshape(1, Q_LORA),
                                      mla_kv_norm[j].reshape(1, KV_LORA), w_q, w_qs, w_k, w_vt, cos_t, sin_t)
            attn_lat, attn_ctx = _attn_call(rows, q, k, vt, with_ctx=not last)
            hs = _mla_out_call(rows, hs, mod_l, attn_lat, attn_ctx, mla_w_o[j].astype(BF16), n_out)
        hs = _ffn_call(rows, hs, mod_l, g[2], wg[i, 1], wu[i, 1], wd[i, 1], 6, n_out,
                       final_g=final_g.reshape(1, D_MODEL) if last else None)
    return hs.reshape(batch, seq, D_MODEL)
```

```python
import functools

import jax
import jax.numpy as jnp
import numpy as np
from jax import lax
from jax.experimental import pallas as pl
from jax.experimental.pallas import tpu as pltpu

F32 = jnp.float32
BF16 = jnp.bfloat16

D_MODEL = 1024
DEPTH = 4
GRID_W = 64
N_MOD = 9
FFN_HIDDEN = 2816
RMS_EPS = 1e-6
HG_HEADS = 8
HG_DK = 128
HG_DV = 128
HG_HK = HG_HEADS * HG_DK
HG_HV = HG_HEADS * HG_DV
HG_IN = 3 * HG_HK + HG_HV + D_MODEL
F_MIN = 1e-6
MLA_HEADS = 8
MLA_NOPE = 128
MLA_ROPE = 64
MLA_V = 128
Q_LORA = 384
KV_LORA = 256
MLA_QK = MLA_NOPE + MLA_ROPE
MLA_SCALE = MLA_QK ** -0.5
ROPE_BASE = 10000.0
LOG2_E = 1.4426950408889634

LANES = 128
SUBLANES = 8
MXU_DIM = 256
VMEM_LIMIT = 56 * 1024 * 1024

TM = 512
FFN_CHUNK = MXU_DIM
HG_C = 128
TQ = 256
Q_SUB = 4
KEY_CHUNK = 512
MLA_HW = 2 * LANES
MLA_DOWN_P = Q_LORA + KV_LORA + 2 * LANES


def _cparams(sem):
    return pltpu.CompilerParams(dimension_semantics=sem, vmem_limit_bytes=VMEM_LIMIT)


def _resident(shape):
    nd = len(shape)
    return pl.BlockSpec(shape, lambda *_: (0,) * nd, pipeline_mode=pl.Buffered(1))


def _dot(a, b):
    return jnp.dot(a, b, preferred_element_type=F32)


def _dot_nt(a, b):
    return lax.dot_general(a, b, (((1,), (1,)), ((), ())), preferred_element_type=F32)


def _dot_tn(a, b):
    return lax.dot_general(a, b, (((0,), (0,)), ((), ())), preferred_element_type=F32)


def _split3(x):
    hi = x.astype(BF16)
    r1 = x - hi.astype(F32)
    mid = r1.astype(BF16)
    lo = (r1 - mid.astype(F32)).astype(BF16)
    return hi, mid, lo


def _rms_modulate(h, g, shift, scale):
    ms = jnp.mean(h * h, axis=-1, keepdims=True)
    return (h * lax.rsqrt(ms + RMS_EPS) * g) * (1.0 + scale) + shift


def _mod_slices(mod_ref, base, n):
    m = mod_ref[0]
    return [m[:, (base + j) * D_MODEL:(base + j + 1) * D_MODEL] for j in range(n)]


def _mod_kernel(c_ref, w_ref, b_ref, o_ref):
    c = c_ref[...]
    s = c * jax.nn.sigmoid(c)
    s_hi = s.astype(BF16)
    s_lo = (s - s_hi.astype(F32)).astype(BF16)
    w = w_ref[0]
    w_hi = w.astype(BF16)
    w_lo = (w - w_hi.astype(F32)).astype(BF16)
    acc = _dot(s_hi, w_hi) + (_dot(s_lo, w_hi) + _dot(s_hi, w_lo))
    o_ref[0] = acc + b_ref[0]


def _mod_call(cond, mod_w, mod_b):
    rows = cond.shape[0]
    tn = D_MODEL
    n_out = N_MOD * D_MODEL
    return pl.pallas_call(
        _mod_kernel,
        grid=(DEPTH, n_out // tn),
        in_specs=[
            pl.BlockSpec((rows, D_MODEL), lambda l, n: (0, 0)),
            pl.BlockSpec((1, D_MODEL, tn), lambda l, n: (l, 0, n)),
            pl.BlockSpec((1, 1, tn), lambda l, n: (l, 0, n)),
        ],
        out_specs=pl.BlockSpec((1, rows, tn), lambda l, n: (l, 0, n)),
        out_shape=jax.ShapeDtypeStruct((DEPTH, rows, n_out), F32),
        compiler_params=_cparams(("arbitrary", "arbitrary")),
        name="mod_vectors",
    )(cond, mod_w, mod_b.reshape(DEPTH, 1, n_out))


class _Rows:
    def __init__(self, batch, seq, ctx_len):
        self.batch, self.seq, self.ctx_len = batch, seq, ctx_len
        self.n_lat = batch * seq
        self.n_ctx = batch * ctx_len
        self.n_tok = self.n_lat + self.n_ctx
        assert seq % TM == 0 and self.n_ctx % TM == 0
        self.lat_tiles = self.n_lat // TM
        self.tiles_per_seq = seq // TM
        self.all_tiles = self.n_tok // TM

    def mod_spec(self):
        lat_tiles, tps, ctx_row = self.lat_tiles, self.tiles_per_seq, self.batch
        return pl.BlockSpec(
            (1, 1, N_MOD * D_MODEL),
            lambda i: (jnp.where(i < lat_tiles, i // tps, ctx_row), 0, 0))


def _tile_spec(width, col=0):
    return pl.BlockSpec((TM, width), lambda i: (i, col))


def _ffn_kernel(h_ref, mod_ref, g_ref, wg_ref, wu_ref, wd_ref, *rest, base, final):
    o_ref = rest[-1]
    h = h_ref[...]
    shift, scale, gate = _mod_slices(mod_ref, base, 3)
    a = _rms_modulate(h, g_ref[...], shift, scale).astype(BF16)
    acc = jnp.zeros((TM, D_MODEL), F32)
    for j in range(FFN_HIDDEN // FFN_CHUNK):
        sl = slice(j * FFN_CHUNK, (j + 1) * FFN_CHUNK)
        gg = _dot(a, wg_ref[:, sl])
        uu = _dot(a, wu_ref[:, sl])
        hm = (gg * jax.nn.sigmoid(gg) * uu).astype(BF16)
        acc = acc + _dot(hm, wd_ref[sl, :])
    out = h + (0.5 * gate) * acc
    if final:
        fg_ref = rest[0]
        ms = jnp.mean(out * out, axis=-1, keepdims=True)
        out = out * lax.rsqrt(ms + RMS_EPS) * fg_ref[...]
    o_ref[...] = out


def _ffn_call(rows, hs, mod_l, g, wg, wu, wd, base, n_tiles, final_g=None):
    final = final_g is not None
    in_specs = [
        _tile_spec(D_MODEL), rows.mod_spec(), _resident((1, D_MODEL)),
        _resident((D_MODEL, FFN_HIDDEN)), _resident((D_MODEL, FFN_HIDDEN)),
        _resident((FFN_HIDDEN, D_MODEL)),
    ]
    args = [hs, mod_l, g, wg, wu, wd]
    if final:
        in_specs.append(_resident((1, D_MODEL)))
        args.append(final_g)
    return pl.pallas_call(
        functools.partial(_ffn_kernel, base=base, final=final),
        grid=(n_tiles,),
        in_specs=in_specs,
        out_specs=_tile_spec(D_MODEL),
        out_shape=jax.ShapeDtypeStruct((n_tiles * TM, D_MODEL), F32),
        compiler_params=_cparams(("arbitrary",)),
        name="ffn_half",
    )(*args)


def _hgrn_in_kernel(h_ref, mod_ref, g_ref, w_ref, o_ref):
    shift, scale = _mod_slices(mod_ref, 3, 2)
    a = _rms_modulate(h_ref[...], g_ref[...], shift, scale).astype(BF16)
    for j in range(HG_IN // D_MODEL):
        sl = slice(j * D_MODEL, (j + 1) * D_MODEL)
        o_ref[:, sl] = _dot(a, w_ref[:, sl])


def _hgrn_in_call(rows, hs, mod_l, g, w_in):
    return pl.pallas_call(
        _hgrn_in_kernel,
        grid=(rows.all_tiles,),
        in_specs=[_tile_spec(D_MODEL), rows.mod_spec(), _resident((1, D_MODEL)),
                  _resident((D_MODEL, HG_IN))],
        out_specs=_tile_spec(HG_IN),
        out_shape=jax.ShapeDtypeStruct((rows.n_tok, HG_IN), F32),
        compiler_params=_cparams(("arbitrary",)),
        name="hgrn_in_proj",
    )(hs, mod_l, g, w_in)


def _hgrn_out_kernel(h_ref, mod_ref, of_ref, ob_ref, gate_ref, gn_ref, bd_ref, w_ref, o_ref):
    (res_gate,) = _mod_slices(mod_ref, 5, 1)
    o = of_ref[...] + ob_ref[...]
    sq = o * o
    sq_hi = sq.astype(BF16)
    sq_lo = (sq - sq_hi.astype(F32)).astype(BF16)
    ss = _dot(sq_hi, bd_ref[...]) + _dot(sq_lo, bd_ref[...])
    on = o * lax.rsqrt(ss * (1.0 / HG_DV) + RMS_EPS) * gn_ref[...]
    gt = gate_ref[...]
    y = _dot((on * (gt * jax.nn.sigmoid(gt))).astype(BF16), w_ref[...])
    o_ref[...] = h_ref[...] + res_gate * y


def _hgrn_out_call(rows, hs, mod_l, o_f, o_b, proj, gn, bd, w_out, n_tiles):
    return pl.pallas_call(
        _hgrn_out_kernel,
        grid=(n_tiles,),
        in_specs=[_tile_spec(D_MODEL), rows.mod_spec(), _tile_spec(HG_HV), _tile_spec(HG_HV),
                  _tile_spec(D_MODEL, col=(3 * HG_HK + HG_HV) // D_MODEL),
                  _resident((1, HG_HV)), _resident((HG_HV, HG_HV)), _resident((HG_HV, D_MODEL))],
        out_specs=_tile_spec(D_MODEL),
        out_shape=jax.ShapeDtypeStruct((n_tiles * TM, D_MODEL), F32),
        compiler_params=_cparams(("arbitrary",)),
        name="hgrn_readout",
    )(hs, mod_l, o_f, o_b, proj, gn, bd, w_out)


def _mla_proj_kernel(h_ref, mod_ref, g_ref, wd_ref, qn_ref, kvn_ref, wq_ref, wqs_ref, wk_ref, wvt_ref,
                     cos_ref, sin_ref, q_ref, k_ref, vt_ref):
    shift, scale = _mod_slices(mod_ref, 3, 2)
    a = _rms_modulate(h_ref[...], g_ref[...], shift, scale).astype(BF16)
    dp = _dot(a, wd_ref[...])
    cq = dp[:, :Q_LORA]
    ckv = dp[:, Q_LORA:Q_LORA + KV_LORA]
    kr_a = dp[:, Q_LORA + KV_LORA:Q_LORA + KV_LORA + LANES]
    kr_b = dp[:, Q_LORA + KV_LORA + LANES:]
    cqn = (cq * lax.rsqrt(jnp.mean(cq * cq, axis=-1, keepdims=True) + RMS_EPS) * qn_ref[...]).astype(BF16)
    ckvn = (ckv * lax.rsqrt(jnp.mean(ckv * ckv, axis=-1, keepdims=True) + RMS_EPS) * kvn_ref[...]).astype(BF16)
    cos = cos_ref[...]
    sin = sin_ref[...]
    cos_q = jnp.concatenate([cos * (MLA_SCALE * LOG2_E)] * MLA_HEADS, axis=1)
    sin_q = jnp.concatenate([sin * (MLA_SCALE * LOG2_E)] * MLA_HEADS, axis=1)
    q_ref[...] = (_dot(cqn, wq_ref[...]) * cos_q + _dot(cqn, wqs_ref[...]) * sin_q).astype(BF16)
    kr = (kr_a * cos[:, LANES:] + kr_b * sin[:, LANES:]).astype(BF16)
    kn = _dot(ckvn, wk_ref[...]).astype(BF16)
    pieces = []
    for hh in range(MLA_HEADS):
        pieces += [kn[:, hh * MLA_NOPE:(hh + 1) * MLA_NOPE], kr]
    k_ref[...] = jnp.concatenate(pieces, axis=1)
    vt_ref[...] = _dot_nt(wvt_ref[...], ckvn).astype(BF16)


def _mla_proj_call(rows, hs, mod_l, g, wd, qn, kvn, wq, wqs, wk, wvt, cos_t, sin_t):
    lat_tiles, tps = rows.lat_tiles, rows.tiles_per_seq
    rope_spec = pl.BlockSpec((TM, MLA_HW), lambda i: (jnp.where(i < lat_tiles, i % tps, tps), 0))
    qk_w = MLA_HEADS * MLA_HW
    v_w = MLA_HEADS * MLA_V
    return pl.pallas_call(
        _mla_proj_kernel,
        grid=(rows.all_tiles,),
        in_specs=[_tile_spec(D_MODEL), rows.mod_spec(), _resident((1, D_MODEL)),
                  _resident((D_MODEL, MLA_DOWN_P)), _resident((1, Q_LORA)), _resident((1, KV_LORA)),
                  _resident((Q_LORA, qk_w)), _resident((Q_LORA, qk_w)),
                  _resident((KV_LORA, MLA_HEADS * MLA_NOPE)), _resident((v_w, KV_LORA)),
                  rope_spec, rope_spec],
        out_specs=[_tile_spec(qk_w), _tile_spec(qk_w), pl.BlockSpec((v_w, TM), lambda i: (0, i))],
        out_shape=[jax.ShapeDtypeStruct((rows.n_tok, qk_w), BF16),
                   jax.ShapeDtypeStruct((rows.n_tok, qk_w), BF16),
                   jax.ShapeDtypeStruct((v_w, rows.n_tok), BF16)],
        compiler_params=_cparams(("arbitrary",)),
        name="mla_proj",
    )(hs, mod_l, g, wd, qn, kvn, wq, wqs, wk, wvt, cos_t, sin_t)


def _mla_out_kernel(h_ref, mod_ref, al_ref, *rest, lat_tiles):
    w_ref, o_ref = rest[-2:]
    (res_gate,) = _mod_slices(mod_ref, 5, 1)

    def project(a_ref):
        o_ref[...] = h_ref[...] + res_gate * _dot(a_ref[...], w_ref[...])

    if len(rest) == 2:
        project(al_ref)
    else:
        pl.when(pl.program_id(0) < lat_tiles)(lambda: project(al_ref))
        pl.when(pl.program_id(0) >= lat_tiles)(lambda: project(rest[0]))


def _mla_out_call(rows, hs, mod_l, attn_lat, attn_ctx, w_o, n_tiles):
    lat_tiles = rows.lat_tiles
    v_w = MLA_HEADS * MLA_V
    in_specs = [_tile_spec(D_MODEL), rows.mod_spec(),
                pl.BlockSpec((TM, v_w), lambda i: (jnp.minimum(i, lat_tiles - 1), 0))]
    args = [hs, mod_l, attn_lat]
    if attn_ctx is not None:
        in_specs.append(pl.BlockSpec((TM, v_w), lambda i: (jnp.maximum(i - lat_tiles, 0), 0)))
        args.append(attn_ctx)
    return pl.pallas_call(
        functools.partial(_mla_out_kernel, lat_tiles=lat_tiles),
        grid=(n_tiles,),
        in_specs=in_specs + [_resident((v_w, D_MODEL))],
        out_specs=_tile_spec(D_MODEL),
        out_shape=jax.ShapeDtypeStruct((n_tiles * TM, D_MODEL), F32),
        compiler_params=_cparams(("arbitrary",)),
        name="mla_out_proj",
    )(*args, w_o)


def _hgrn_gates(z, lb, tri):
    width = z.shape[1]
    f = lb + (1.0 - lb) * jax.nn.sigmoid(z)
    lg = jnp.log2(jnp.maximum(f, F_MIN))
    hi = lg.astype(BF16)
    lo = (lg - hi.astype(F32)).astype(BF16)
    c2 = _dot(tri, jnp.concatenate([hi, lo], axis=1))
    return 1.0 - f, c2[:, :width] + c2[:, width:]


def _block_diag(a, b):
    zero = jnp.zeros_like(a)
    return jnp.concatenate([jnp.concatenate([a, zero], axis=1), jnp.concatenate([zero, b], axis=1)], axis=0)


def _hgrn_pair(q, k, cum, v, st_a_ref, st_b_ref, rev):
    c = HG_C
    pair_w = 2 * LANES
    row = lax.broadcasted_iota(jnp.int32, (c, pair_w), 0)
    col = lax.broadcasted_iota(jnp.int32, (c, pair_w), 1) & (LANES - 1)
    x = row ^ col
    causal = (row <= col) if rev else (row >= col)
    qb, kb, vb = q.astype(BF16), k.astype(BF16), v.astype(BF16)

    def halves(m):
        return m[:, :LANES], m[:, LANES:]

    def pair_scores(w_q, w_k):
        return _dot_nt(qb * w_q.astype(BF16), _block_diag(*halves(kb * w_k.astype(BF16))))

    edge = cum[0:1, :] if rev else cum[c - 1:c, :]
    st_a, st_b = st_a_ref[...], st_b_ref[...]
    o = _dot_nt(qb * jnp.exp2(cum).astype(BF16), _block_diag(st_a.astype(BF16), st_b.astype(BF16)))
    upd = _dot_tn(vb, kb * jnp.exp2(edge - cum).astype(BF16))
    dec_a, dec_b = halves(jnp.exp2(edge))
    st_a_ref[...] = st_a * dec_a + upd[:LANES, :LANES]
    st_b_ref[...] = st_b * dec_b + upd[LANES:, LANES:]

    mid_row = SUBLANES // 2 if rev else SUBLANES // 2 - 1
    ref8 = jnp.concatenate(
        [jnp.broadcast_to(cum[b * SUBLANES + mid_row:b * SUBLANES + mid_row + 1, :], (SUBLANES, pair_w))
         for b in range(c // SUBLANES)], axis=0)
    d8 = cum - ref8
    a = pair_scores(jnp.exp2(d8), jnp.exp2(-d8))
    lvl = 3
    while (1 << lvl) < c:
        half = 1 << lvl
        parts = []
        for b in range(c // (2 * half)):
            lo = cum[2 * b * half:(2 * b + 1) * half, :]
            hi = cum[(2 * b + 1) * half:(2 * b + 2) * half, :]
            if rev:
                ref = hi[0:1, :]
                parts += [lo - ref, ref - hi]
            else:
                ref = lo[half - 1:half, :]
                parts += [ref - lo, hi - ref]
        w = jnp.exp2(jnp.concatenate(parts, axis=0))
        a = jnp.where(x < half, a, pair_scores(w, w))
        lvl += 1
    a = jnp.where(causal, a, 0.0)
    return o + _dot(a.astype(BF16), _block_diag(*halves(vb)))


def _hgrn_scan_kernel(qf_ref, zf_ref, vf_ref, qb_ref, zb_ref, vb_ref, lb_ref, tri_ref,
                      of_ref, ob_ref, st_ref):
    @pl.when(pl.program_id(1) == 0)
    def _():
        st_ref[...] = jnp.zeros_like(st_ref)

    k_f, cum_f = _hgrn_gates(zf_ref[...], lb_ref[0:1, :], tri_ref[0])
    k_b, cum_b = _hgrn_gates(zb_ref[...], lb_ref[1:2, :], tri_ref[1])
    for p in range(HG_HEADS // 2):
        cols = slice(p * 2 * LANES, (p + 1) * 2 * LANES)
        of_ref[:, cols] = _hgrn_pair(qf_ref[:, cols], k_f[:, cols], cum_f[:, cols], vf_ref[:, cols],
                                     st_ref.at[0, 2 * p], st_ref.at[0, 2 * p + 1], False)
        ob_ref[:, cols] = _hgrn_pair(qb_ref[:, cols], k_b[:, cols], cum_b[:, cols], vb_ref[:, cols],
                                     st_ref.at[1, 2 * p], st_ref.at[1, 2 * p + 1], True)


def _hgrn_scan_call(rows, proj, lb, tri):
    c_chunks = rows.ctx_len // HG_C
    l_chunks = rows.seq // HG_C
    steps = c_chunks + l_chunks
    ctx0 = rows.n_lat // HG_C

    def fwd_blk(b, s):
        return jnp.where(s < c_chunks, ctx0 + b * c_chunks + s, b * l_chunks + s - c_chunks)

    def bwd_blk(b, s):
        return jnp.where(s < c_chunks, ctx0 + b * c_chunks + (c_chunks - 1 - s),
                         b * l_chunks + (l_chunks - 1 - (s - c_chunks)))

    def in_spec(blk, col):
        return pl.BlockSpec((HG_C, HG_HK), lambda b, s: (blk(b, s), col))

    out_sd = jax.ShapeDtypeStruct((rows.n_tok, HG_HV), F32)
    return pl.pallas_call(
        _hgrn_scan_kernel,
        grid=(rows.batch, steps),
        in_specs=[in_spec(fwd_blk, 0), in_spec(fwd_blk, 1), in_spec(fwd_blk, 3),
                  in_spec(bwd_blk, 0), in_spec(bwd_blk, 2), in_spec(bwd_blk, 3),
                  pl.BlockSpec((2, HG_HK), lambda b, s: (0, 0)),
                  pl.BlockSpec((2, HG_C, HG_C), lambda b, s: (0, 0, 0))],
        out_specs=[pl.BlockSpec((HG_C, HG_HV), lambda b, s: (fwd_blk(b, s), 0)),
                   pl.BlockSpec((HG_C, HG_HV), lambda b, s: (bwd_blk(b, s), 0))],
        out_shape=[out_sd, out_sd],
        scratch_shapes=[pltpu.VMEM((2, HG_HEADS, HG_DV, HG_DK), F32)],
        compiler_params=_cparams(("arbitrary", "arbitrary")),
        name="hgrn_scan",
    )(proj, proj, proj, proj, proj, proj, lb, tri)


def _attn_scores(q, kc_ref, kl_ref=None):
    scores = [_dot_nt(kc_ref[...], q)]
    if kl_ref is not None:
        scores += [_dot_nt(kl_ref[pl.ds(j * KEY_CHUNK, KEY_CHUNK), :], q)
                   for j in range(kl_ref.shape[0] // KEY_CHUNK)]
    m = jnp.max(scores[0], axis=0, keepdims=True)
    for s in scores[1:]:
        m = jnp.maximum(m, jnp.max(s, axis=0, keepdims=True))
    return scores, m


def _attn_values(scores, m, vtc_ref, vtl_ref=None):
    den, acc = None, None
    for j, s in enumerate(scores):
        p = jnp.exp2(s - m)
        vt = vtc_ref[...] if j == 0 else vtl_ref[:, pl.ds((j - 1) * KEY_CHUNK, KEY_CHUNK)]
        d_j = jnp.sum(p, axis=0, keepdims=True)
        a_j = _dot(vt, p.astype(BF16))
        den, acc = (d_j, a_j) if den is None else (den + d_j, acc + a_j)
    return (acc / den).T


def _attn_lat_kernel(q_ref, kl_ref, kc_ref, vtl_ref, vtc_ref, o_ref):
    nxt = _attn_scores(q_ref[pl.ds(0, TQ), :], kc_ref, kl_ref)
    for j in range(Q_SUB):
        cur = nxt
        if j + 1 < Q_SUB:
            nxt = _attn_scores(q_ref[pl.ds((j + 1) * TQ, TQ), :], kc_ref, kl_ref)
        o_ref[pl.ds(j * TQ, TQ), :] = _attn_values(*cur, vtc_ref, vtl_ref).astype(o_ref.dtype)


def _attn_ctx_kernel(q_ref, kc_ref, vtc_ref, o_ref):
    scores, m = _attn_scores(q_ref[...], kc_ref)
    o_ref[...] = _attn_values(scores, m, vtc_ref).astype(o_ref.dtype)


def _attn_call(rows, q, k, vt, with_ctx):
    tq = Q_SUB * TQ
    assert rows.ctx_len == TQ and rows.seq % tq == 0
    n_q = rows.seq // tq
    ctx0 = rows.n_lat // TQ
    v_w = MLA_HEADS * MLA_V
    ctx_k = pl.BlockSpec((TQ, MLA_HW), lambda b, h, *_: (ctx0 + b, h))
    ctx_vt = pl.BlockSpec((MLA_V, TQ), lambda b, h, *_: (h, ctx0 + b))
    lat = pl.pallas_call(
        _attn_lat_kernel,
        grid=(rows.batch, MLA_HEADS, n_q),
        in_specs=[pl.BlockSpec((tq, MLA_HW), lambda b, h, qi: (b * n_q + qi, h)),
                  pl.BlockSpec((rows.seq, MLA_HW), lambda b, h, qi: (b, h)), ctx_k,
                  pl.BlockSpec((MLA_V, rows.seq), lambda b, h, qi: (h, b)), ctx_vt],
        out_specs=pl.BlockSpec((tq, MLA_V), lambda b, h, qi: (b * n_q + qi, h)),
        out_shape=jax.ShapeDtypeStruct((rows.n_lat, v_w), BF16),
        compiler_params=_cparams(("arbitrary", "arbitrary", "arbitrary")),
        name="mla_attention",
    )(q, k, k, vt, vt)
    if not with_ctx:
        return lat, None
    ctx = pl.pallas_call(
        _attn_ctx_kernel,
        grid=(rows.batch, MLA_HEADS),
        in_specs=[ctx_k, ctx_k, ctx_vt],
        out_specs=pl.BlockSpec((TQ, MLA_V), lambda b, h: (b, h)),
        out_shape=jax.ShapeDtypeStruct((rows.n_ctx, v_w), BF16),
        compiler_params=_cparams(("arbitrary", "arbitrary")),
        name="mla_attention_ctx",
    )(q, k, vt)
    return lat, ctx


def _rope_swap_cols():
    idx = np.arange(MLA_ROPE).reshape(2, 2, MLA_ROPE // 4)
    return idx[:, ::-1, :].reshape(-1)


def _mla_weights(w_down, w_uq, w_ukv):
    swap = _rope_swap_cols()
    kr = w_down[:, Q_LORA + KV_LORA:]
    zpad = jnp.zeros((D_MODEL, LANES - MLA_ROPE), w_down.dtype)
    wd = jnp.concatenate([w_down[:, :Q_LORA + KV_LORA], kr, zpad, kr[:, swap], zpad], axis=1)
    uq = w_uq.reshape(Q_LORA, MLA_HEADS, MLA_QK)
    nope, rope = uq[..., :MLA_NOPE], uq[..., MLA_NOPE:]
    z64 = jnp.zeros((Q_LORA, MLA_HEADS, MLA_HW - MLA_QK), w_uq.dtype)
    z128 = jnp.zeros((Q_LORA, MLA_HEADS, MLA_NOPE), w_uq.dtype)
    wq = jnp.concatenate([nope, rope, z64], axis=-1).reshape(Q_LORA, MLA_HEADS * MLA_HW)
    wqs = jnp.concatenate([z128, rope[..., swap], z64], axis=-1).reshape(Q_LORA, MLA_HEADS * MLA_HW)
    ukv = w_ukv.reshape(KV_LORA, MLA_HEADS, MLA_NOPE + MLA_V)
    wk = ukv[..., :MLA_NOPE].reshape(KV_LORA, MLA_HEADS * MLA_NOPE)
    wvt = ukv[..., MLA_NOPE:].reshape(KV_LORA, MLA_HEADS * MLA_V).T
    return wd.astype(BF16), wq.astype(BF16), wqs.astype(BF16), wk.astype(BF16), wvt.astype(BF16)


def _rope_tables(seq):
    n_rows = seq // GRID_W
    row = jnp.repeat(jnp.arange(n_rows), GRID_W).astype(F32)
    col = jnp.tile(jnp.arange(GRID_W), n_rows).astype(F32)
    axis_dims = MLA_ROPE // 2
    inv = 1.0 / (ROPE_BASE ** (jnp.arange(0, axis_dims, 2, dtype=F32) / axis_dims))
    ang_r, ang_c = row[:, None] * inv, col[:, None] * inv
    cos64 = jnp.concatenate([jnp.cos(ang_r)] * 2 + [jnp.cos(ang_c)] * 2, axis=1)
    sin64 = jnp.concatenate([-jnp.sin(ang_r), jnp.sin(ang_r), -jnp.sin(ang_c), jnp.sin(ang_c)], axis=1)
    cos64 = jnp.concatenate([cos64, jnp.ones((TM, MLA_ROPE), F32)], axis=0)
    sin64 = jnp.concatenate([sin64, jnp.zeros((TM, MLA_ROPE), F32)], axis=0)
    n = seq + TM
    cos_t = jnp.concatenate([jnp.ones((n, MLA_NOPE), F32), cos64, jnp.zeros((n, MLA_HW - MLA_QK), F32)], axis=1)
    sin_t = jnp.concatenate([jnp.zeros((n, MLA_NOPE), F32), sin64, jnp.zeros((n, MLA_HW - MLA_QK), F32)], axis=1)
    return cos_t, sin_t


def kernel(x, c, ctx, c_ctx, mod_w, mod_b, norm_g, ffn_w_gate, ffn_w_up, ffn_w_down, hg_w_in, hg_w_out,
           hg_gn, hg_lb_logits, mla_w_down, mla_q_norm, mla_w_uq, mla_kv_norm, mla_w_ukv, mla_w_o, final_g):
    batch, seq, _ = x.shape
    rows = _Rows(batch, seq, ctx.shape[1])
    hs = jnp.concatenate([x.reshape(rows.n_lat, D_MODEL), ctx.reshape(rows.n_ctx, D_MODEL)], axis=0)

    mod_rows = -(-(batch + 1) // SUBLANES) * SUBLANES
    cond = jnp.concatenate([c, c_ctx[None], jnp.zeros((mod_rows - batch - 1, D_MODEL), F32)], axis=0)
    mod = _mod_call(cond, mod_w, mod_b)

    p_lb = jax.nn.softmax(hg_lb_logits.astype(F32), axis=0)
    lbs = jnp.maximum(jnp.cumsum(p_lb, axis=0) - p_lb[0:1], 0.0)
    tri = jnp.stack([jnp.tril(jnp.ones((HG_C, HG_C), F32)), jnp.triu(jnp.ones((HG_C, HG_C), F32))]).astype(BF16)
    head_id = np.arange(HG_HV) // HG_DV
    bd = jnp.asarray(head_id[:, None] == head_id[None, :], BF16)
    cos_t, sin_t = _rope_tables(seq)

    wg, wu, wd = ffn_w_gate.astype(BF16), ffn_w_up.astype(BF16), ffn_w_down.astype(BF16)

    for i in range(DEPTH):
        last = i == DEPTH - 1
        j = i // 2
        mod_l = mod[i].reshape(mod_rows, 1, N_MOD * D_MODEL)
        g = norm_g[i].reshape(3, 1, D_MODEL)
        hs = _ffn_call(rows, hs, mod_l, g[0], wg[i, 0], wu[i, 0], wd[i, 0], 0, rows.all_tiles)
        n_out = rows.lat_tiles if last else rows.all_tiles
        if i % 2 == 0:
            proj = _hgrn_in_call(rows, hs, mod_l, g[1], hg_w_in[j].astype(BF16))
            o_f, o_b = _hgrn_scan_call(rows, proj, lbs[j], tri)
            gn = jnp.tile(hg_gn[j], HG_HEADS).reshape(1, HG_HV)
            hs = _hgrn_out_call(rows, hs, mod_l, o_f, o_b, proj, gn, bd, hg_w_out[j].astype(BF16), n_out)
        else:
            w_dn, w_q, w_qs, w_k, w_vt = _mla_weights(mla_w_down[j], mla_w_uq[j], mla_w_ukv[j])
            q, k, vt = _mla_proj_call(rows, hs, mod_l, g[1], w_dn, mla_q_norm[j].reshape(1, Q_LORA),
                                      mla_kv_norm[j].reshape(1, KV_LORA), w_q, w_qs, w_k, w_vt, cos_t, sin_t)
            attn_lat, attn_ctx = _attn_call(rows, q, k, vt, with_ctx=not last)
            hs = _mla_out_call(rows, hs, mod_l, attn_lat, attn_ctx, mla_w_o[j].astype(BF16), n_out)
        hs = _ffn_call(rows, hs, mod_l, g[2], wg[i, 1], wu[i, 1], wd[i, 1], 6, n_out,
                       final_g=final_g.reshape(1, D_MODEL) if last else None)
    return hs.reshape(batch, seq, D_MODEL)
```

```python
import functools

import jax
import jax.numpy as jnp
import numpy as np
from jax import lax
from jax.experimental import pallas as pl
from jax.experimental.pallas import tpu as pltpu

F32 = jnp.float32
BF16 = jnp.bfloat16

D_MODEL = 1024
DEPTH = 4
GRID_W = 64
N_MOD = 9
FFN_HIDDEN = 2816
RMS_EPS = 1e-6
HG_HEADS = 8
HG_DK = 128
HG_DV = 128
HG_HK = HG_HEADS * HG_DK
HG_HV = HG_HEADS * HG_DV
HG_IN = 3 * HG_HK + HG_HV + D_MODEL
F_MIN = 1e-6
MLA_HEADS = 8
MLA_NOPE = 128
MLA_ROPE = 64
MLA_V = 128
Q_LORA = 384
KV_LORA = 256
MLA_QK = MLA_NOPE + MLA_ROPE
MLA_SCALE = MLA_QK ** -0.5
ROPE_BASE = 10000.0
LOG2_E = 1.4426950408889634

LANES = 128
SUBLANES = 8
MXU_DIM = 256
VMEM_LIMIT = 56 * 1024 * 1024

TM = 512
FFN_CHUNK = MXU_DIM
HG_C = 128
TQ = 256
Q_SUB = 8
KEY_CHUNK = 1024
MLA_HW = 2 * LANES
MLA_DOWN_P = Q_LORA + KV_LORA + 2 * LANES


def _cparams(sem):
    return pltpu.CompilerParams(dimension_semantics=sem, vmem_limit_bytes=VMEM_LIMIT)


def _resident(shape):
    nd = len(shape)
    return pl.BlockSpec(shape, lambda *_: (0,) * nd, pipeline_mode=pl.Buffered(1))


def _dot(a, b):
    return jnp.dot(a, b, preferred_element_type=F32)


def _dot_nt(a, b):
    return lax.dot_general(a, b, (((1,), (1,)), ((), ())), preferred_element_type=F32)


def _dot_tn(a, b):
    return lax.dot_general(a, b, (((0,), (0,)), ((), ())), preferred_element_type=F32)


def _rms_modulate(h, g, shift, scale):
    ms = jnp.mean(h * h, axis=-1, keepdims=True)
    return (h * lax.rsqrt(ms + RMS_EPS) * g) * (1.0 + scale) + shift


def _mod_slices(mod_ref, base, n):
    m = mod_ref[0]
    return [m[:, (base + j) * D_MODEL:(base + j + 1) * D_MODEL] for j in range(n)]


def _mod_kernel(c_ref, w_ref, b_ref, o_ref):
    c = c_ref[...]
    s = c * jax.nn.sigmoid(c)
    s_hi = s.astype(BF16)
    s_lo = (s - s_hi.astype(F32)).astype(BF16)
    w = w_ref[0]
    w_hi = w.astype(BF16)
    w_lo = (w - w_hi.astype(F32)).astype(BF16)
    acc = _dot(s_hi, w_hi) + (_dot(s_lo, w_hi) + _dot(s_hi, w_lo))
    o_ref[0] = acc + b_ref[0]


def _mod_call(cond, mod_w, mod_b):
    rows = cond.shape[0]
    tn = D_MODEL
    n_out = N_MOD * D_MODEL
    return pl.pallas_call(
        _mod_kernel,
        grid=(DEPTH, n_out // tn),
        in_specs=[
            pl.BlockSpec((rows, D_MODEL), lambda l, n: (0, 0)),
            pl.BlockSpec((1, D_MODEL, tn), lambda l, n: (l, 0, n)),
            pl.BlockSpec((1, 1, tn), lambda l, n: (l, 0, n)),
        ],
        out_specs=pl.BlockSpec((1, rows, tn), lambda l, n: (l, 0, n)),
        out_shape=jax.ShapeDtypeStruct((DEPTH, rows, n_out), F32),
        compiler_params=_cparams(("arbitrary", "arbitrary")),
        name="mod_vectors",
    )(cond, mod_w, mod_b.reshape(DEPTH, 1, n_out))


class _Rows:
    def __init__(self, batch, seq, ctx_len):
        self.batch, self.seq, self.ctx_len = batch, seq, ctx_len
        self.n_lat = batch * seq
        self.n_ctx = batch * ctx_len
        self.n_tok = self.n_lat + self.n_ctx
        assert seq % TM == 0 and self.n_ctx % TM == 0
        self.lat_tiles = self.n_lat // TM
        self.tiles_per_seq = seq // TM
        self.all_tiles = self.n_tok // TM

    def mod_spec(self):
        lat_tiles, tps, ctx_row = self.lat_tiles, self.tiles_per_seq, self.batch
        return pl.BlockSpec(
            (1, 1, N_MOD * D_MODEL),
            lambda i: (jnp.where(i < lat_tiles, i // tps, ctx_row), 0, 0))


def _tile_spec(width, col=0):
    return pl.BlockSpec((TM, width), lambda i: (i, col))


def _hgrn_readout(of_ref, ob_ref, gate_ref, gn_ref, bd_ref, w_ref):
    o = of_ref[...] + ob_ref[...]
    sq = o * o
    sq_hi = sq.astype(BF16)
    sq_lo = (sq - sq_hi.astype(F32)).astype(BF16)
    ss = _dot(sq_hi, bd_ref[...]) + _dot(sq_lo, bd_ref[...])
    on = o * lax.rsqrt(ss * (1.0 / HG_DV) + RMS_EPS) * gn_ref[...]
    gt = gate_ref[...]
    return _dot((on * (gt * jax.nn.sigmoid(gt))).astype(BF16), w_ref[...])


def _ffn_kernel(h_ref, mod_ref, g_ref, *rest, base, final, mixer):
    rest = list(rest)
    o_ref = rest.pop()
    h = h_ref[...]
    if mixer is not None:
        (res_gate,) = _mod_slices(mod_ref, 5, 1)
        if mixer == "hgrn":
            y = _hgrn_readout(*rest[:6])
            rest = rest[6:]
        else:
            y = _dot(rest[0][...], rest[1][...])
            rest = rest[2:]
        h = h + res_gate * y
    wg_ref, wu_ref, wd_ref = rest[:3]
    shift, scale, gate = _mod_slices(mod_ref, base, 3)
    a = _rms_modulate(h, g_ref[...], shift, scale).astype(BF16)
    acc = jnp.zeros((TM, D_MODEL), F32)
    for j in range(FFN_HIDDEN // FFN_CHUNK):
        sl = slice(j * FFN_CHUNK, (j + 1) * FFN_CHUNK)
        gg = _dot(a, wg_ref[:, sl])
        uu = _dot(a, wu_ref[:, sl])
        hm = (gg * jax.nn.sigmoid(gg) * uu).astype(BF16)
        acc = acc + _dot(hm, wd_ref[sl, :])
    out = h + (0.5 * gate) * acc
    if final:
        fg_ref = rest[3]
        ms = jnp.mean(out * out, axis=-1, keepdims=True)
        out = out * lax.rsqrt(ms + RMS_EPS) * fg_ref[...]
    o_ref[...] = out


def _ffn_call(rows, hs, mod_l, g, ffn_w, layer, half, base, n_tiles, final_g=None, mixer=None, mixer_args=()):
    final = final_g is not None

    def stacked(shape):
        return pl.BlockSpec((None, None) + shape, lambda i: (layer, half, 0, 0), pipeline_mode=pl.Buffered(1))

    in_specs = [_tile_spec(D_MODEL), rows.mod_spec(), _resident((1, D_MODEL))]
    if mixer == "hgrn":
        in_specs += [_tile_spec(HG_HV), _tile_spec(HG_HV), _tile_spec(D_MODEL, col=(3 * HG_HK + HG_HV) // D_MODEL),
                     _resident((1, HG_HV)), _resident((HG_HV, HG_HV)), _resident((HG_HV, D_MODEL))]
    elif mixer == "mla":
        in_specs += [_tile_spec(MLA_HEADS * MLA_V), _resident((MLA_HEADS * MLA_V, D_MODEL))]
    in_specs += [stacked((D_MODEL, FFN_HIDDEN)), stacked((D_MODEL, FFN_HIDDEN)), stacked((FFN_HIDDEN, D_MODEL))]
    args = [hs, mod_l, g, *mixer_args, *ffn_w]
    if final:
        in_specs.append(_resident((1, D_MODEL)))
        args.append(final_g)
    return pl.pallas_call(
        functools.partial(_ffn_kernel, base=base, final=final, mixer=mixer),
        grid=(n_tiles,),
        in_specs=in_specs,
        out_specs=_tile_spec(D_MODEL),
        out_shape=jax.ShapeDtypeStruct((n_tiles * TM, D_MODEL), F32),
        compiler_params=_cparams(("arbitrary",)),
        name="ffn_half",
    )(*args)


def _hgrn_in_kernel(h_ref, mod_ref, g_ref, w_ref, o_ref):
    shift, scale = _mod_slices(mod_ref, 3, 2)
    a = _rms_modulate(h_ref[...], g_ref[...], shift, scale).astype(BF16)
    for j in range(HG_IN // D_MODEL):
        sl = slice(j * D_MODEL, (j + 1) * D_MODEL)
        o_ref[:, sl] = _dot(a, w_ref[:, sl])


def _hgrn_in_call(rows, hs, mod_l, g, w_in):
    return pl.pallas_call(
        _hgrn_in_kernel,
        grid=(rows.all_tiles,),
        in_specs=[_tile_spec(D_MODEL), rows.mod_spec(), _resident((1, D_MODEL)),
                  _resident((D_MODEL, HG_IN))],
        out_specs=_tile_spec(HG_IN),
        out_shape=jax.ShapeDtypeStruct((rows.n_tok, HG_IN), F32),
        compiler_params=_cparams(("arbitrary",)),
        name="hgrn_in_proj",
    )(hs, mod_l, g, w_in)


def _mla_proj_kernel(h_ref, mod_ref, g_ref, wd_ref, qn_ref, kvn_ref, wq_ref, wqs_ref, wk_ref, wvt_ref,
                     cos_ref, sin_ref, q_ref, k_ref, vt_ref):
    shift, scale = _mod_slices(mod_ref, 3, 2)
    a = _rms_modulate(h_ref[...], g_ref[...], shift, scale).astype(BF16)
    dp = _dot(a, wd_ref[...])
    cq = dp[:, :Q_LORA]
    ckv = dp[:, Q_LORA:Q_LORA + KV_LORA]
    kr_a = dp[:, Q_LORA + KV_LORA:Q_LORA + KV_LORA + LANES]
    kr_b = dp[:, Q_LORA + KV_LORA + LANES:]
    cqn = (cq * lax.rsqrt(jnp.mean(cq * cq, axis=-1, keepdims=True) + RMS_EPS) * qn_ref[...]).astype(BF16)
    ckvn = (ckv * lax.rsqrt(jnp.mean(ckv * ckv, axis=-1, keepdims=True) + RMS_EPS) * kvn_ref[...]).astype(BF16)
    cos = cos_ref[...]
    sin = sin_ref[...]
    cos_q = jnp.concatenate([cos * (MLA_SCALE * LOG2_E)] * MLA_HEADS, axis=1)
    sin_q = jnp.concatenate([sin * (MLA_SCALE * LOG2_E)] * MLA_HEADS, axis=1)
    q_ref[...] = (_dot(cqn, wq_ref[...]) * cos_q + _dot(cqn, wqs_ref[...]) * sin_q).astype(BF16)
    kr = (kr_a * cos[:, LANES:] + kr_b * sin[:, LANES:]).astype(BF16)
    kn = _dot(ckvn, wk_ref[...]).astype(BF16)
    pieces = []
    for hh in range(MLA_HEADS):
        pieces += [kn[:, hh * MLA_NOPE:(hh + 1) * MLA_NOPE], kr]
    k_ref[...] = jnp.concatenate(pieces, axis=1)
    vt_ref[...] = _dot_nt(wvt_ref[...], ckvn).astype(BF16)


def _mla_proj_call(rows, hs, mod_l, g, wd, qn, kvn, wq, wqs, wk, wvt, cos_t, sin_t):
    lat_tiles, tps = rows.lat_tiles, rows.tiles_per_seq
    rope_spec = pl.BlockSpec((TM, MLA_HW), lambda i: (jnp.where(i < lat_tiles, i % tps, tps), 0))
    qk_w = MLA_HEADS * MLA_HW
    v_w = MLA_HEADS * MLA_V
    return pl.pallas_call(
        _mla_proj_kernel,
        grid=(rows.all_tiles,),
        in_specs=[_tile_spec(D_MODEL), rows.mod_spec(), _resident((1, D_MODEL)),
                  _resident((D_MODEL, MLA_DOWN_P)), _resident((1, Q_LORA)), _resident((1, KV_LORA)),
                  _resident((Q_LORA, qk_w)), _resident((Q_LORA, qk_w)),
                  _resident((KV_LORA, MLA_HEADS * MLA_NOPE)), _resident((v_w, KV_LORA)),
                  rope_spec, rope_spec],
        out_specs=[_tile_spec(qk_w), _tile_spec(qk_w), pl.BlockSpec((v_w, TM), lambda i: (0, i))],
        out_shape=[jax.ShapeDtypeStruct((rows.n_tok, qk_w), BF16),
                   jax.ShapeDtypeStruct((rows.n_tok, qk_w), BF16),
                   jax.ShapeDtypeStruct((v_w, rows.n_tok), BF16)],
        compiler_params=_cparams(("arbitrary",)),
        name="mla_proj",
    )(hs, mod_l, g, wd, qn, kvn, wq, wqs, wk, wvt, cos_t, sin_t)


def _hgrn_gates(z, lb, tri):
    width = z.shape[1]
    f = lb + (1.0 - lb) * jax.nn.sigmoid(z)
    lg = jnp.log2(jnp.maximum(f, F_MIN))
    hi = lg.astype(BF16)
    lo = (lg - hi.astype(F32)).astype(BF16)
    c2 = _dot(tri, jnp.concatenate([hi, lo], axis=1))
    return 1.0 - f, c2[:, :width] + c2[:, width:]


def _block_diag(a, b):
    zero = jnp.zeros_like(a)
    return jnp.concatenate([jnp.concatenate([a, zero], axis=1), jnp.concatenate([zero, b], axis=1)], axis=0)


def _hgrn_pair(q, k, cum, v, st_a_ref, st_b_ref, rev):
    c = HG_C
    pair_w = 2 * LANES
    row = lax.broadcasted_iota(jnp.int32, (c, pair_w), 0)
    col = lax.broadcasted_iota(jnp.int32, (c, pair_w), 1) & (LANES - 1)
    x = row ^ col
    causal = (row <= col) if rev else (row >= col)
    qb, kb, vb = q.astype(BF16), k.astype(BF16), v.astype(BF16)

    def halves(m):
        return m[:, :LANES], m[:, LANES:]

    def pair_scores(w_q, w_k):
        return _dot_nt(qb * w_q.astype(BF16), _block_diag(*halves(kb * w_k.astype(BF16))))

    edge = cum[0:1, :] if rev else cum[c - 1:c, :]
    st_a, st_b = st_a_ref[...], st_b_ref[...]
    o = _dot_nt(qb * jnp.exp2(cum).astype(BF16), _block_diag(st_a.astype(BF16), st_b.astype(BF16)))
    upd = _dot_tn(vb, kb * jnp.exp2(edge - cum).astype(BF16))
    dec_a, dec_b = halves(jnp.exp2(edge))
    st_a_ref[...] = st_a * dec_a + upd[:LANES, :LANES]
    st_b_ref[...] = st_b * dec_b + upd[LANES:, LANES:]

    mid_row = SUBLANES // 2 if rev else SUBLANES // 2 - 1
    ref8 = jnp.concatenate(
        [jnp.broadcast_to(cum[b * SUBLANES + mid_row:b * SUBLANES + mid_row + 1, :], (SUBLANES, pair_w))
         for b in range(c // SUBLANES)], axis=0)
    d8 = cum - ref8
    a = pair_scores(jnp.exp2(d8), jnp.exp2(-d8))
    lvl = 3
    while (1 << lvl) < c:
        half = 1 << lvl
        parts = []
        for b in range(c // (2 * half)):
            lo = cum[2 * b * half:(2 * b + 1) * half, :]
            hi = cum[(2 * b + 1) * half:(2 * b + 2) * half, :]
            if rev:
                ref = hi[0:1, :]
                parts += [lo - ref, ref - hi]
            else:
                ref = lo[half - 1:half, :]
                parts += [ref - lo, hi - ref]
        w = jnp.exp2(jnp.concatenate(parts, axis=0))
        a = jnp.where(x < half, a, pair_scores(w, w))
        lvl += 1
    a = jnp.where(causal, a, 0.0)
    return o + _dot(a.astype(BF16), _block_diag(*halves(vb)))


def _hgrn_scan_kernel(qf_ref, zf_ref, vf_ref, qb_ref, zb_ref, vb_ref, lb_ref, tri_ref,
                      of_ref, ob_ref, st_ref):
    @pl.when(pl.program_id(1) == 0)
    def _():
        st_ref[...] = jnp.zeros_like(st_ref)

    k_f, cum_f = _hgrn_gates(zf_ref[...], lb_ref[0:1, :], tri_ref[0])
    k_b, cum_b = _hgrn_gates(zb_ref[...], lb_ref[1:2, :], tri_ref[1])
    for p in range(HG_HEADS // 2):
        cols = slice(p * 2 * LANES, (p + 1) * 2 * LANES)
        of_ref[:, cols] = _hgrn_pair(qf_ref[:, cols], k_f[:, cols], cum_f[:, cols], vf_ref[:, cols],
                                     st_ref.at[0, 2 * p], st_ref.at[0, 2 * p + 1], False)
        ob_ref[:, cols] = _hgrn_pair(qb_ref[:, cols], k_b[:, cols], cum_b[:, cols], vb_ref[:, cols],
                                     st_ref.at[1, 2 * p], st_ref.at[1, 2 * p + 1], True)


def _hgrn_scan_call(rows, proj, lb, tri):
    c_chunks = rows.ctx_len // HG_C
    l_chunks = rows.seq // HG_C
    steps = c_chunks + l_chunks
    ctx0 = rows.n_lat // HG_C

    def fwd_blk(b, s):
        return jnp.where(s < c_chunks, ctx0 + b * c_chunks + s, b * l_chunks + s - c_chunks)

    def bwd_blk(b, s):
        return jnp.where(s < c_chunks, ctx0 + b * c_chunks + (c_chunks - 1 - s),
                         b * l_chunks + (l_chunks - 1 - (s - c_chunks)))

    def in_spec(blk, col):
        return pl.BlockSpec((HG_C, HG_HK), lambda b, s: (blk(b, s), col))

    out_sd = jax.ShapeDtypeStruct((rows.n_tok, HG_HV), F32)
    return pl.pallas_call(
        _hgrn_scan_kernel,
        grid=(rows.batch, steps),
        in_specs=[in_spec(fwd_blk, 0), in_spec(fwd_blk, 1), in_spec(fwd_blk, 3),
                  in_spec(bwd_blk, 0), in_spec(bwd_blk, 2), in_spec(bwd_blk, 3),
                  pl.BlockSpec((2, HG_HK), lambda b, s: (0, 0)),
                  pl.BlockSpec((2, HG_C, HG_C), lambda b, s: (0, 0, 0))],
        out_specs=[pl.BlockSpec((HG_C, HG_HV), lambda b, s: (fwd_blk(b, s), 0)),
                   pl.BlockSpec((HG_C, HG_HV), lambda b, s: (bwd_blk(b, s), 0))],
        out_shape=[out_sd, out_sd],
        scratch_shapes=[pltpu.VMEM((2, HG_HEADS, HG_DV, HG_DK), F32)],
        compiler_params=_cparams(("arbitrary", "arbitrary")),
        name="hgrn_scan",
    )(proj, proj, proj, proj, proj, proj, lb, tri)


def _attn_scores(q, kc_ref, kl_ref=None):
    scores = [_dot_nt(kc_ref[...], q)]
    if kl_ref is not None:
        scores += [_dot_nt(kl_ref[pl.ds(j * KEY_CHUNK, KEY_CHUNK), :], q)
                   for j in range(kl_ref.shape[0] // KEY_CHUNK)]
    m = jnp.max(scores[0], axis=0, keepdims=True)
    for s in scores[1:]:
        m = jnp.maximum(m, jnp.max(s, axis=0, keepdims=True))
    return scores, m


def _attn_values(scores, m, vtc_ref, vtl_ref=None):
    den, acc = None, None
    for j, s in enumerate(scores):
        p = jnp.exp2(s - m)
        vt = vtc_ref[...] if j == 0 else vtl_ref[:, pl.ds((j - 1) * KEY_CHUNK, KEY_CHUNK)]
        d_j = jnp.sum(p, axis=0, keepdims=True)
        a_j = _dot(vt, p.astype(BF16))
        den, acc = (d_j, a_j) if den is None else (den + d_j, acc + a_j)
    return (acc / den).T


def _attn_lat_kernel(q_ref, kl_ref, kc_ref, vtl_ref, vtc_ref, o_ref):
    nxt = _attn_scores(q_ref[pl.ds(0, TQ), :], kc_ref, kl_ref)
    for j in range(Q_SUB):
        cur = nxt
        if j + 1 < Q_SUB:
            nxt = _attn_scores(q_ref[pl.ds((j + 1) * TQ, TQ), :], kc_ref, kl_ref)
        o_ref[pl.ds(j * TQ, TQ), :] = _attn_values(*cur, vtc_ref, vtl_ref).astype(o_ref.dtype)


def _attn_ctx_kernel(q_ref, kc_ref, vtc_ref, lat_ref, o_ref):
    del lat_ref
    scores, m = _attn_scores(q_ref[...], kc_ref)
    o_ref[...] = _attn_values(scores, m, vtc_ref).astype(o_ref.dtype)


def _attn_call(rows, q, k, vt, with_ctx):
    tq = Q_SUB * TQ
    assert rows.ctx_len == TQ and rows.seq % tq == 0
    n_q = rows.seq // tq
    ctx0 = rows.n_lat // TQ
    out_sd = jax.ShapeDtypeStruct((rows.n_tok, MLA_HEADS * MLA_V), BF16)
    ctx_k = pl.BlockSpec((TQ, MLA_HW), lambda b, h, *_: (ctx0 + b, h))
    ctx_vt = pl.BlockSpec((MLA_V, TQ), lambda b, h, *_: (h, ctx0 + b))
    out = pl.pallas_call(
        _attn_lat_kernel,
        grid=(rows.batch, MLA_HEADS, n_q),
        in_specs=[pl.BlockSpec((tq, MLA_HW), lambda b, h, qi: (b * n_q + qi, h)),
                  pl.BlockSpec((rows.seq, MLA_HW), lambda b, h, qi: (b, h)), ctx_k,
                  pl.BlockSpec((MLA_V, rows.seq), lambda b, h, qi: (h, b)), ctx_vt],
        out_specs=pl.BlockSpec((tq, MLA_V), lambda b, h, qi: (b * n_q + qi, h)),
        out_shape=out_sd,
        compiler_params=_cparams(("arbitrary", "arbitrary", "arbitrary")),
        name="mla_attention",
    )(q, k, k, vt, vt)
    if not with_ctx:
        return out
    return pl.pallas_call(
        _attn_ctx_kernel,
        grid=(rows.batch, MLA_HEADS),
        in_specs=[ctx_k, ctx_k, ctx_vt, pl.BlockSpec(memory_space=pl.ANY)],
        out_specs=pl.BlockSpec((TQ, MLA_V), lambda b, h: (ctx0 + b, h)),
        out_shape=out_sd,
        input_output_aliases={3: 0},
        compiler_params=_cparams(("arbitrary", "arbitrary")),
        name="mla_attention_ctx",
    )(q, k, vt, out)


def _rope_swap_cols():
    idx = np.arange(MLA_ROPE).reshape(2, 2, MLA_ROPE // 4)
    return idx[:, ::-1, :].reshape(-1)


def _mla_weights(w_down, w_uq, w_ukv):
    swap = _rope_swap_cols()
    kr = w_down[:, Q_LORA + KV_LORA:]
    zpad = jnp.zeros((D_MODEL, LANES - MLA_ROPE), w_down.dtype)
    wd = jnp.concatenate([w_down[:, :Q_LORA + KV_LORA], kr, zpad, kr[:, swap], zpad], axis=1)
    uq = w_uq.reshape(Q_LORA, MLA_HEADS, MLA_QK)
    nope, rope = uq[..., :MLA_NOPE], uq[..., MLA_NOPE:]
    z64 = jnp.zeros((Q_LORA, MLA_HEADS, MLA_HW - MLA_QK), w_uq.dtype)
    z128 = jnp.zeros((Q_LORA, MLA_HEADS, MLA_NOPE), w_uq.dtype)
    wq = jnp.concatenate([nope, rope, z64], axis=-1).reshape(Q_LORA, MLA_HEADS * MLA_HW)
    wqs = jnp.concatenate([z128, rope[..., swap], z64], axis=-1).reshape(Q_LORA, MLA_HEADS * MLA_HW)
    ukv = w_ukv.reshape(KV_LORA, MLA_HEADS, MLA_NOPE + MLA_V)
    wk = ukv[..., :MLA_NOPE].reshape(KV_LORA, MLA_HEADS * MLA_NOPE)
    wvt = ukv[..., MLA_NOPE:].reshape(KV_LORA, MLA_HEADS * MLA_V).T
    return wd.astype(BF16), wq.astype(BF16), wqs.astype(BF16), wk.astype(BF16), wvt.astype(BF16)


def _rope_tables(seq):
    n_rows = seq // GRID_W
    row = jnp.repeat(jnp.arange(n_rows), GRID_W).astype(F32)
    col = jnp.tile(jnp.arange(GRID_W), n_rows).astype(F32)
    axis_dims = MLA_ROPE // 2
    inv = 1.0 / (ROPE_BASE ** (jnp.arange(0, axis_dims, 2, dtype=F32) / axis_dims))
    ang_r, ang_c = row[:, None] * inv, col[:, None] * inv
    cos64 = jnp.concatenate([jnp.cos(ang_r)] * 2 + [jnp.cos(ang_c)] * 2, axis=1)
    sin64 = jnp.concatenate([-jnp.sin(ang_r), jnp.sin(ang_r), -jnp.sin(ang_c), jnp.sin(ang_c)], axis=1)
    cos64 = jnp.concatenate([cos64, jnp.ones((TM, MLA_ROPE), F32)], axis=0)
    sin64 = jnp.concatenate([sin64, jnp.zeros((TM, MLA_ROPE), F32)], axis=0)
    n = seq + TM
    cos_t = jnp.concatenate([jnp.ones((n, MLA_NOPE), F32), cos64, jnp.zeros((n, MLA_HW - MLA_QK), F32)], axis=1)
    sin_t = jnp.concatenate([jnp.zeros((n, MLA_NOPE), F32), sin64, jnp.zeros((n, MLA_HW - MLA_QK), F32)], axis=1)
    return cos_t, sin_t


def kernel(x, c, ctx, c_ctx, mod_w, mod_b, norm_g, ffn_w_gate, ffn_w_up, ffn_w_down, hg_w_in, hg_w_out,
           hg_gn, hg_lb_logits, mla_w_down, mla_q_norm, mla_w_uq, mla_kv_norm, mla_w_ukv, mla_w_o, final_g):
    batch, seq, _ = x.shape
    rows = _Rows(batch, seq, ctx.shape[1])
    hs = jnp.concatenate([x.reshape(rows.n_lat, D_MODEL), ctx.reshape(rows.n_ctx, D_MODEL)], axis=0)

    mod_rows = -(-(batch + 1) // SUBLANES) * SUBLANES
    cond = jnp.concatenate([c, c_ctx[None], jnp.zeros((mod_rows - batch - 1, D_MODEL), F32)], axis=0)
    mod = _mod_call(cond, mod_w, mod_b)

    p_lb = jax.nn.softmax(hg_lb_logits.astype(F32), axis=0)
    lbs = jnp.maximum(jnp.cumsum(p_lb, axis=0) - p_lb[0:1], 0.0)
    tri = jnp.stack([jnp.tril(jnp.ones((HG_C, HG_C), F32)), jnp.triu(jnp.ones((HG_C, HG_C), F32))]).astype(BF16)
    head_id = np.arange(HG_HV) // HG_DV
    bd = jnp.asarray(head_id[:, None] == head_id[None, :], BF16)
    cos_t, sin_t = _rope_tables(seq)

    ffn_w = (ffn_w_gate.astype(BF16), ffn_w_up.astype(BF16), ffn_w_down.astype(BF16))

    for i in range(DEPTH):
        last = i == DEPTH - 1
        j = i // 2
        mod_l = mod[i].reshape(mod_rows, 1, N_MOD * D_MODEL)
        g = norm_g[i].reshape(3, 1, D_MODEL)
        hs = _ffn_call(rows, hs, mod_l, g[0], ffn_w, i, 0, 0, rows.all_tiles)
        n_out = rows.lat_tiles if last else rows.all_tiles
        if i % 2 == 0:
            proj = _hgrn_in_call(rows, hs, mod_l, g[1], hg_w_in[j].astype(BF16))
            o_f, o_b = _hgrn_scan_call(rows, proj, lbs[j], tri)
            gn = jnp.tile(hg_gn[j], HG_HEADS).reshape(1, HG_HV)
            mixer, mixer_args = "hgrn", (o_f, o_b, proj, gn, bd, hg_w_out[j].astype(BF16))
        else:
            w_dn, w_q, w_qs, w_k, w_vt = _mla_weights(mla_w_down[j], mla_w_uq[j], mla_w_ukv[j])
            q, k, vt = _mla_proj_call(rows, hs, mod_l, g[1], w_dn, mla_q_norm[j].reshape(1, Q_LORA),
                                      mla_kv_norm[j].reshape(1, KV_LORA), w_q, w_qs, w_k, w_vt, cos_t, sin_t)
            mixer, mixer_args = "mla", (_attn_call(rows, q, k, vt, with_ctx=not last), mla_w_o[j].astype(BF16))
        hs = _ffn_call(rows, hs, mod_l, g[2], ffn_w, i, 1, 6, n_out,
                       final_g=final_g.reshape(1, D_MODEL) if last else None, mixer=mixer, mixer_args=mixer_args)
    return hs.reshape(batch, seq, D_MODEL)
```

```python
import functools

import jax
import jax.numpy as jnp
import numpy as np
from jax import lax
from jax.experimental import pallas as pl
from jax.experimental.pallas import tpu as pltpu

F32 = jnp.float32
BF16 = jnp.bfloat16

D_MODEL = 1024
DEPTH = 4
GRID_W = 64
N_MOD = 9
FFN_HIDDEN = 2816
RMS_EPS = 1e-6
HG_HEADS = 8
HG_DK = 128
HG_DV = 128
HG_HK = HG_HEADS * HG_DK
HG_HV = HG_HEADS * HG_DV
HG_IN = 3 * HG_HK + HG_HV + D_MODEL
F_MIN = 1e-6
MLA_HEADS = 8
MLA_NOPE = 128
MLA_ROPE = 64
MLA_V = 128
Q_LORA = 384
KV_LORA = 256
MLA_QK = MLA_NOPE + MLA_ROPE
MLA_SCALE = MLA_QK ** -0.5
ROPE_BASE = 10000.0
LOG2_E = 1.4426950408889634

LANES = 128
SUBLANES = 8
MXU_DIM = 256
VMEM_LIMIT = 56 * 1024 * 1024

TM = 512
FFN_CHUNK = MXU_DIM
HG_C = 128
TQ = 256
Q_SUB = 8
KEY_CHUNK = 1024
MLA_HW = 2 * LANES
MLA_DOWN_P = Q_LORA + KV_LORA + 2 * LANES


def _cparams(sem):
    return pltpu.CompilerParams(dimension_semantics=sem, vmem_limit_bytes=VMEM_LIMIT)


def _resident(shape):
    nd = len(shape)
    return pl.BlockSpec(shape, lambda *_: (0,) * nd, pipeline_mode=pl.Buffered(1))


def _dot(a, b):
    return jnp.dot(a, b, preferred_element_type=F32)


def _dot_nt(a, b):
    return lax.dot_general(a, b, (((1,), (1,)), ((), ())), preferred_element_type=F32)


def _dot_tn(a, b):
    return lax.dot_general(a, b, (((0,), (0,)), ((), ())), preferred_element_type=F32)


def _rms_modulate(h, g, shift, scale):
    ms = jnp.mean(h * h, axis=-1, keepdims=True)
    return (h * lax.rsqrt(ms + RMS_EPS) * g) * (1.0 + scale) + shift


def _mod_slices(mod_ref, base, n):
    m = mod_ref[0]
    return [m[:, (base + j) * D_MODEL:(base + j + 1) * D_MODEL] for j in range(n)]


def _mod_kernel(c_ref, w_ref, b_ref, o_ref):
    c = c_ref[...]
    s = c * jax.nn.sigmoid(c)
    s_hi = s.astype(BF16)
    s_lo = (s - s_hi.astype(F32)).astype(BF16)
    w = w_ref[0]
    w_hi = w.astype(BF16)
    w_lo = (w - w_hi.astype(F32)).astype(BF16)
    acc = _dot(s_hi, w_hi) + (_dot(s_lo, w_hi) + _dot(s_hi, w_lo))
    o_ref[0] = acc + b_ref[0]


def _mod_call(cond, mod_w, mod_b):
    rows = cond.shape[0]
    tn = D_MODEL
    n_out = N_MOD * D_MODEL
    return pl.pallas_call(
        _mod_kernel,
        grid=(DEPTH, n_out // tn),
        in_specs=[
            pl.BlockSpec((rows, D_MODEL), lambda l, n: (0, 0)),
            pl.BlockSpec((1, D_MODEL, tn), lambda l, n: (l, 0, n)),
            pl.BlockSpec((1, 1, tn), lambda l, n: (l, 0, n)),
        ],
        out_specs=pl.BlockSpec((1, rows, tn), lambda l, n: (l, 0, n)),
        out_shape=jax.ShapeDtypeStruct((DEPTH, rows, n_out), F32),
        compiler_params=_cparams(("arbitrary", "arbitrary")),
        name="mod_vectors",
    )(cond, mod_w, mod_b.reshape(DEPTH, 1, n_out))


class _Rows:
    def __init__(self, batch, seq, ctx_len):
        self.batch, self.seq, self.ctx_len = batch, seq, ctx_len
        self.n_lat = batch * seq
        self.n_ctx = batch * ctx_len
        self.n_tok = self.n_lat + self.n_ctx
        assert seq % TM == 0 and self.n_ctx % TM == 0
        self.lat_tiles = self.n_lat // TM
        self.tiles_per_seq = seq // TM
        self.all_tiles = self.n_tok // TM

    def mod_spec(self):
        lat_tiles, tps, ctx_row = self.lat_tiles, self.tiles_per_seq, self.batch
        return pl.BlockSpec(
            (1, 1, N_MOD * D_MODEL),
            lambda i: (jnp.where(i < lat_tiles, i // tps, ctx_row), 0, 0))


def _tile_spec(width, col=0):
    return pl.BlockSpec((TM, width), lambda i: (i, col))


def _hgrn_readout(of_ref, ob_ref, gate_ref, gn_ref, bd_ref, w_ref):
    o = of_ref[...] + ob_ref[...]
    ss = _dot((o * o).astype(BF16), bd_ref[...])
    on = o * lax.rsqrt(ss * (1.0 / HG_DV) + RMS_EPS) * gn_ref[...]
    gt = gate_ref[...]
    return _dot((on * (gt * jax.nn.sigmoid(gt))).astype(BF16), w_ref[...])


def _ffn_kernel(*refs, base, final, mixer, lat_tiles):
    rest = list(refs)
    if lat_tiles is None:
        o_ref = rest.pop()
        h = rest.pop(0)[...]
    else:
        h_scr = rest.pop()
        o_ref = rest.pop()
        x_ref, c_ref = rest.pop(0), rest.pop(0)

        @pl.when(pl.program_id(0) < lat_tiles)
        def _():
            h_scr[...] = x_ref[...]

        @pl.when(pl.program_id(0) >= lat_tiles)
        def _():
            h_scr[...] = c_ref[...]

        h = h_scr[...]
    mod_ref, g_ref = rest.pop(0), rest.pop(0)
    if mixer is not None:
        (res_gate,) = _mod_slices(mod_ref, 5, 1)
        if mixer == "hgrn":
            y = _hgrn_readout(*rest[:6])
            rest = rest[6:]
        else:
            y = _dot(rest[0][...], rest[1][...])
            rest = rest[2:]
        h = h + res_gate * y
    wg_ref, wu_ref, wd_ref = rest[:3]
    shift, scale, gate = _mod_slices(mod_ref, base, 3)
    a = _rms_modulate(h, g_ref[...], shift, scale).astype(BF16)
    acc = jnp.zeros((TM, D_MODEL), F32)
    for j in range(FFN_HIDDEN // FFN_CHUNK):
        sl = slice(j * FFN_CHUNK, (j + 1) * FFN_CHUNK)
        gg = _dot(a, wg_ref[:, sl])
        uu = _dot(a, wu_ref[:, sl])
        hm = (gg * jax.nn.sigmoid(gg) * uu).astype(BF16)
        acc = acc + _dot(hm, wd_ref[sl, :])
    out = h + (0.5 * gate) * acc
    if final:
        fg_ref = rest[3]
        ms = jnp.mean(out * out, axis=-1, keepdims=True)
        out = out * lax.rsqrt(ms + RMS_EPS) * fg_ref[...]
    o_ref[...] = out


def _ffn_call(rows, hs, mod_l, g, ffn_w, layer, half, base, n_tiles, final_g=None, mixer=None, mixer_args=()):
    final = final_g is not None
    split_in = isinstance(hs, tuple)
    lat_tiles = rows.lat_tiles

    def stacked(shape):
        return pl.BlockSpec((None, None) + shape, lambda i: (layer, half, 0, 0), pipeline_mode=pl.Buffered(1))

    if split_in:
        in_specs = [pl.BlockSpec((TM, D_MODEL), lambda i: (jnp.minimum(i, lat_tiles - 1), 0)),
                    pl.BlockSpec((TM, D_MODEL), lambda i: (jnp.maximum(i - lat_tiles, 0), 0))]
        args = list(hs)
    else:
        in_specs, args = [_tile_spec(D_MODEL)], [hs]
    in_specs += [rows.mod_spec(), _resident((1, D_MODEL))]
    if mixer == "hgrn":
        in_specs += [_tile_spec(HG_HV), _tile_spec(HG_HV), _tile_spec(D_MODEL, col=HG_GATE),
                     _resident((1, HG_HV)), _resident((HG_HV, HG_HV)), _resident((HG_HV, D_MODEL))]
    elif mixer == "mla":
        in_specs += [_tile_spec(MLA_HEADS * MLA_V), _resident((MLA_HEADS * MLA_V, D_MODEL))]
    in_specs += [stacked((D_MODEL, FFN_HIDDEN)), stacked((D_MODEL, FFN_HIDDEN)), stacked((FFN_HIDDEN, D_MODEL))]
    args += [mod_l, g, *mixer_args, *ffn_w]
    if final:
        in_specs.append(_resident((1, D_MODEL)))
        args.append(final_g)
    return pl.pallas_call(
        functools.partial(_ffn_kernel, base=base, final=final, mixer=mixer,
                          lat_tiles=lat_tiles if split_in else None),
        grid=(n_tiles,),
        in_specs=in_specs,
        out_specs=_tile_spec(D_MODEL),
        out_shape=jax.ShapeDtypeStruct((n_tiles * TM, D_MODEL), F32),
        scratch_shapes=[pltpu.VMEM((TM, D_MODEL), F32)] if split_in else [],
        compiler_params=_cparams(("arbitrary",)),
        name="ffn_half",
    )(*args)


HG_Q, HG_V, HG_KF, HG_KB = range(4)
HG_LGF, HG_LGB, HG_GATE = range(3)


def _hgrn_in_kernel(h_ref, mod_ref, g_ref, w_ref, lb_ref, pb_ref, pf_ref):
    shift, scale = _mod_slices(mod_ref, 3, 2)
    a = _rms_modulate(h_ref[...], g_ref[...], shift, scale).astype(BF16)

    for piece in range(D_MODEL // MXU_DIM):
        def proj(j):
            return _dot(a, w_ref[:, pl.ds(j * D_MODEL + piece * MXU_DIM, MXU_DIM)])

        def put(ref, blk, val):
            ref[:, pl.ds(blk * D_MODEL + piece * MXU_DIM, MXU_DIM)] = val.astype(ref.dtype)

        put(pb_ref, HG_Q, proj(0))
        put(pb_ref, HG_V, proj(3))
        for d, (k_blk, lg_blk) in enumerate(((HG_KF, HG_LGF), (HG_KB, HG_LGB))):
            lb = lb_ref[d:d + 1, pl.ds(piece * MXU_DIM, MXU_DIM)]
            f = lb + (1.0 - lb) * jax.nn.sigmoid(proj(1 + d))
            put(pb_ref, k_blk, 1.0 - f)
            put(pf_ref, lg_blk, jnp.log2(jnp.maximum(f, F_MIN)))
        put(pf_ref, HG_GATE, proj(4))


def _hgrn_in_call(rows, hs, mod_l, g, w_in, lb):
    return pl.pallas_call(
        _hgrn_in_kernel,
        grid=(rows.all_tiles,),
        in_specs=[_tile_spec(D_MODEL), rows.mod_spec(), _resident((1, D_MODEL)),
                  _resident((D_MODEL, HG_IN)), _resident((2, HG_HK))],
        out_specs=[_tile_spec(4 * D_MODEL), _tile_spec(3 * D_MODEL)],
        out_shape=[jax.ShapeDtypeStruct((rows.n_tok, 4 * D_MODEL), BF16),
                   jax.ShapeDtypeStruct((rows.n_tok, 3 * D_MODEL), F32)],
        compiler_params=_cparams(("arbitrary",)),
        name="hgrn_in_proj",
    )(hs, mod_l, g, w_in, lb)


def _mla_proj_kernel(h_ref, mod_ref, g_ref, wd_ref, qn_ref, kvn_ref, wq_ref, wqs_ref, wk_ref, wvt_ref,
                     cos_ref, sin_ref, q_ref, k_ref, vt_ref):
    shift, scale = _mod_slices(mod_ref, 3, 2)
    a = _rms_modulate(h_ref[...], g_ref[...], shift, scale).astype(BF16)
    dp = _dot(a, wd_ref[...])
    cq = dp[:, :Q_LORA]
    ckv = dp[:, Q_LORA:Q_LORA + KV_LORA]
    kr_a = dp[:, Q_LORA + KV_LORA:Q_LORA + KV_LORA + LANES]
    kr_b = dp[:, Q_LORA + KV_LORA + LANES:]
    cqn = (cq * lax.rsqrt(jnp.mean(cq * cq, axis=-1, keepdims=True) + RMS_EPS) * qn_ref[...]).astype(BF16)
    ckvn = (ckv * lax.rsqrt(jnp.mean(ckv * ckv, axis=-1, keepdims=True) + RMS_EPS) * kvn_ref[...]).astype(BF16)
    cos = cos_ref[...]
    sin = sin_ref[...]
    cos_q = jnp.concatenate([cos * (MLA_SCALE * LOG2_E)] * MLA_HEADS, axis=1)
    sin_q = jnp.concatenate([sin * (MLA_SCALE * LOG2_E)] * MLA_HEADS, axis=1)
    q_ref[...] = (_dot(cqn, wq_ref[...]) * cos_q + _dot(cqn, wqs_ref[...]) * sin_q).astype(BF16)
    kr = (kr_a * cos[:, LANES:] + kr_b * sin[:, LANES:]).astype(BF16)
    kn = _dot(ckvn, wk_ref[...]).astype(BF16)
    pieces = []
    for hh in range(MLA_HEADS):
        pieces += [kn[:, hh * MLA_NOPE:(hh + 1) * MLA_NOPE], kr]
    k_ref[...] = jnp.concatenate(pieces, axis=1)
    vt_ref[...] = _dot_nt(wvt_ref[...], ckvn).astype(BF16)


def _mla_proj_call(rows, hs, mod_l, g, wd, qn, kvn, wq, wqs, wk, wvt, cos_t, sin_t):
    lat_tiles, tps = rows.lat_tiles, rows.tiles_per_seq
    rope_spec = pl.BlockSpec((TM, MLA_HW), lambda i: (jnp.where(i < lat_tiles, i % tps, tps), 0))
    qk_w = MLA_HEADS * MLA_HW
    v_w = MLA_HEADS * MLA_V
    return pl.pallas_call(
        _mla_proj_kernel,
        grid=(rows.all_tiles,),
        in_specs=[_tile_spec(D_MODEL), rows.mod_spec(), _resident((1, D_MODEL)),
                  _resident((D_MODEL, MLA_DOWN_P)), _resident((1, Q_LORA)), _resident((1, KV_LORA)),
                  _resident((Q_LORA, qk_w)), _resident((Q_LORA, qk_w)),
                  _resident((KV_LORA, MLA_HEADS * MLA_NOPE)), _resident((v_w, KV_LORA)),
                  rope_spec, rope_spec],
        out_specs=[_tile_spec(qk_w), _tile_spec(qk_w), pl.BlockSpec((v_w, TM), lambda i: (0, i))],
        out_shape=[jax.ShapeDtypeStruct((rows.n_tok, qk_w), BF16),
                   jax.ShapeDtypeStruct((rows.n_tok, qk_w), BF16),
                   jax.ShapeDtypeStruct((v_w, rows.n_tok), BF16)],
        compiler_params=_cparams(("arbitrary",)),
        name="mla_proj",
    )(hs, mod_l, g, wd, qn, kvn, wq, wqs, wk, wvt, cos_t, sin_t)


def _hgrn_cumsum(lg, tri):
    width = lg.shape[1]
    hi = lg.astype(BF16)
    lo = (lg - hi.astype(F32)).astype(BF16)
    c2 = _dot(tri, jnp.concatenate([hi, lo], axis=1))
    return c2[:, :width] + c2[:, width:]


def _block_diag(a, b):
    zero = jnp.zeros_like(a)
    return jnp.concatenate([jnp.concatenate([a, zero], axis=1), jnp.concatenate([zero, b], axis=1)], axis=0)


def _hgrn_pair(qb, kb, cum, vb, st_a_ref, st_b_ref, rev):
    c = HG_C
    pair_w = 2 * LANES
    row = lax.broadcasted_iota(jnp.int32, (c, pair_w), 0)
    col = lax.broadcasted_iota(jnp.int32, (c, pair_w), 1) & (LANES - 1)
    x = row ^ col
    causal = (row <= col) if rev else (row >= col)

    def halves(m):
        return m[:, :LANES], m[:, LANES:]

    def pair_scores(w_q, w_k):
        return _dot_nt(qb * w_q.astype(BF16), _block_diag(*halves(kb * w_k.astype(BF16))))

    edge = cum[0:1, :] if rev else cum[c - 1:c, :]
    st_a, st_b = st_a_ref[...], st_b_ref[...]
    o = _dot_nt(qb * jnp.exp2(cum).astype(BF16), _block_diag(st_a.astype(BF16), st_b.astype(BF16)))
    upd = _dot_tn(vb, kb * jnp.exp2(edge - cum).astype(BF16))
    dec_a, dec_b = halves(jnp.exp2(edge))
    st_a_ref[...] = st_a * dec_a + upd[:LANES, :LANES]
    st_b_ref[...] = st_b * dec_b + upd[LANES:, LANES:]

    mid_row = SUBLANES // 2 if rev else SUBLANES // 2 - 1
    ref8 = jnp.concatenate(
        [jnp.broadcast_to(cum[b * SUBLANES + mid_row:b * SUBLANES + mid_row + 1, :], (SUBLANES, pair_w))
         for b in range(c // SUBLANES)], axis=0)
    d8 = cum - ref8
    a = pair_scores(jnp.exp2(d8), jnp.exp2(-d8))
    lvl = 3
    while (1 << lvl) < c:
        half = 1 << lvl
        parts = []
        for b in range(c // (2 * half)):
            lo = cum[2 * b * half:(2 * b + 1) * half, :]
            hi = cum[(2 * b + 1) * half:(2 * b + 2) * half, :]
            if rev:
                ref = hi[0:1, :]
                parts += [lo - ref, ref - hi]
            else:
                ref = lo[half - 1:half, :]
                parts += [ref - lo, hi - ref]
        w = jnp.exp2(jnp.concatenate(parts, axis=0))
        a = jnp.where(x < half, a, pair_scores(w, w))
        lvl += 1
    a = jnp.where(causal, a, 0.0)
    return o + _dot(a.astype(BF16), _block_diag(*halves(vb)))


def _hgrn_scan_kernel(qf_ref, kf_ref, vf_ref, lgf_ref, qb_ref, kb_ref, vb_ref, lgb_ref, tri_ref,
                      of_ref, ob_ref, st_ref):
    @pl.when(pl.program_id(1) == 0)
    def _():
        st_ref[...] = jnp.zeros_like(st_ref)

    cum_f = _hgrn_cumsum(lgf_ref[...], tri_ref[0])
    cum_b = _hgrn_cumsum(lgb_ref[...], tri_ref[1])
    for p in range(HG_HEADS // 2):
        cols = slice(p * 2 * LANES, (p + 1) * 2 * LANES)
        of_ref[:, cols] = _hgrn_pair(qf_ref[:, cols], kf_ref[:, cols], cum_f[:, cols], vf_ref[:, cols],
                                     st_ref.at[0, 2 * p], st_ref.at[0, 2 * p + 1], False)
        ob_ref[:, cols] = _hgrn_pair(qb_ref[:, cols], kb_ref[:, cols], cum_b[:, cols], vb_ref[:, cols],
                                     st_ref.at[1, 2 * p], st_ref.at[1, 2 * p + 1], True)


def _hgrn_scan_call(rows, pb, pf, tri):
    c_chunks = rows.ctx_len // HG_C
    l_chunks = rows.seq // HG_C
    steps = c_chunks + l_chunks
    ctx0 = rows.n_lat // HG_C

    def fwd_blk(b, s):
        return jnp.where(s < c_chunks, ctx0 + b * c_chunks + s, b * l_chunks + s - c_chunks)

    def bwd_blk(b, s):
        return jnp.where(s < c_chunks, ctx0 + b * c_chunks + (c_chunks - 1 - s),
                         b * l_chunks + (l_chunks - 1 - (s - c_chunks)))

    def in_spec(blk, col):
        return pl.BlockSpec((HG_C, HG_HK), lambda b, s: (blk(b, s), col))

    out_sd = jax.ShapeDtypeStruct((rows.n_tok, HG_HV), F32)
    return pl.pallas_call(
        _hgrn_scan_kernel,
        grid=(rows.batch, steps),
        in_specs=[in_spec(fwd_blk, HG_Q), in_spec(fwd_blk, HG_KF), in_spec(fwd_blk, HG_V), in_spec(fwd_blk, HG_LGF),
                  in_spec(bwd_blk, HG_Q), in_spec(bwd_blk, HG_KB), in_spec(bwd_blk, HG_V), in_spec(bwd_blk, HG_LGB),
                  pl.BlockSpec((2, HG_C, HG_C), lambda b, s: (0, 0, 0))],
        out_specs=[pl.BlockSpec((HG_C, HG_HV), lambda b, s: (fwd_blk(b, s), 0)),
                   pl.BlockSpec((HG_C, HG_HV), lambda b, s: (bwd_blk(b, s), 0))],
        out_shape=[out_sd, out_sd],
        scratch_shapes=[pltpu.VMEM((2, HG_HEADS, HG_DV, HG_DK), F32)],
        compiler_params=_cparams(("arbitrary", "arbitrary")),
        name="hgrn_scan",
    )(pb, pb, pb, pf, pb, pb, pb, pf, tri)


def _attn_scores(q, kc_ref, kl_ref=None):
    scores = [_dot_nt(kc_ref[...], q)]
    if kl_ref is not None:
        scores += [_dot_nt(kl_ref[pl.ds(j * KEY_CHUNK, KEY_CHUNK), :], q)
                   for j in range(kl_ref.shape[0] // KEY_CHUNK)]
    m = jnp.max(scores[0], axis=0, keepdims=True)
    for s in scores[1:]:
        m = jnp.maximum(m, jnp.max(s, axis=0, keepdims=True))
    return scores, m


def _attn_values(scores, m, vtc_ref, vtl_ref=None):
    den, acc = None, None
    for j, s in enumerate(scores):
        p = jnp.exp2(s - m)
        vt = vtc_ref[...] if j == 0 else vtl_ref[:, pl.ds((j - 1) * KEY_CHUNK, KEY_CHUNK)]
        d_j = jnp.sum(p, axis=0, keepdims=True)
        a_j = _dot(vt, p.astype(BF16))
        den, acc = (d_j, a_j) if den is None else (den + d_j, acc + a_j)
    return (acc / den).T


def _attn_lat_kernel(q_ref, kl_ref, kc_ref, vtl_ref, vtc_ref, o_ref):
    nxt = _attn_scores(q_ref[pl.ds(0, TQ), :], kc_ref, kl_ref)
    for j in range(Q_SUB):
        cur = nxt
        if j + 1 < Q_SUB:
            nxt = _attn_scores(q_ref[pl.ds((j + 1) * TQ, TQ), :], kc_ref, kl_ref)
        o_ref[pl.ds(j * TQ, TQ), :] = _attn_values(*cur, vtc_ref, vtl_ref).astype(o_ref.dtype)


def _attn_ctx_kernel(q_ref, kc_ref, vtc_ref, lat_ref, o_ref):
    del lat_ref
    scores, m = _attn_scores(q_ref[...], kc_ref)
    o_ref[...] = _attn_values(scores, m, vtc_ref).astype(o_ref.dtype)


def _attn_call(rows, q, k, vt, with_ctx):
    tq = Q_SUB * TQ
    assert rows.ctx_len == TQ and rows.seq % tq == 0
    n_q = rows.seq // tq
    ctx0 = rows.n_lat // TQ
    out_sd = jax.ShapeDtypeStruct((rows.n_tok, MLA_HEADS * MLA_V), BF16)
    ctx_k = pl.BlockSpec((TQ, MLA_HW), lambda b, h, *_: (ctx0 + b, h))
    ctx_vt = pl.BlockSpec((MLA_V, TQ), lambda b, h, *_: (h, ctx0 + b))
    out = pl.pallas_call(
        _attn_lat_kernel,
        grid=(rows.batch, MLA_HEADS, n_q),
        in_specs=[pl.BlockSpec((tq, MLA_HW), lambda b, h, qi: (b * n_q + qi, h)),
                  pl.BlockSpec((rows.seq, MLA_HW), lambda b, h, qi: (b, h)), ctx_k,
                  pl.BlockSpec((MLA_V, rows.seq), lambda b, h, qi: (h, b)), ctx_vt],
        out_specs=pl.BlockSpec((tq, MLA_V), lambda b, h, qi: (b * n_q + qi, h)),
        out_shape=out_sd,
        compiler_params=_cparams(("arbitrary", "arbitrary", "arbitrary")),
        name="mla_attention",
    )(q, k, k, vt, vt)
    if not with_ctx:
        return out
    return pl.pallas_call(
        _attn_ctx_kernel,
        grid=(rows.batch, MLA_HEADS),
        in_specs=[ctx_k, ctx_k, ctx_vt, pl.BlockSpec(memory_space=pl.ANY)],
        out_specs=pl.BlockSpec((TQ, MLA_V), lambda b, h: (ctx0 + b, h)),
        out_shape=out_sd,
        input_output_aliases={3: 0},
        compiler_params=_cparams(("arbitrary", "arbitrary")),
        name="mla_attention_ctx",
    )(q, k, vt, out)


def _rope_swap_cols():
    idx = np.arange(MLA_ROPE).reshape(2, 2, MLA_ROPE // 4)
    return idx[:, ::-1, :].reshape(-1)


def _mla_weights(w_down, w_uq, w_ukv):
    swap = _rope_swap_cols()
    kr = w_down[:, Q_LORA + KV_LORA:]
    zpad = jnp.zeros((D_MODEL, LANES - MLA_ROPE), w_down.dtype)
    wd = jnp.concatenate([w_down[:, :Q_LORA + KV_LORA], kr, zpad, kr[:, swap], zpad], axis=1)
    uq = w_uq.reshape(Q_LORA, MLA_HEADS, MLA_QK)
    nope, rope = uq[..., :MLA_NOPE], uq[..., MLA_NOPE:]
    z64 = jnp.zeros((Q_LORA, MLA_HEADS, MLA_HW - MLA_QK), w_uq.dtype)
    z128 = jnp.zeros((Q_LORA, MLA_HEADS, MLA_NOPE), w_uq.dtype)
    wq = jnp.concatenate([nope, rope, z64], axis=-1).reshape(Q_LORA, MLA_HEADS * MLA_HW)
    wqs = jnp.concatenate([z128, rope[..., swap], z64], axis=-1).reshape(Q_LORA, MLA_HEADS * MLA_HW)
    ukv = w_ukv.reshape(KV_LORA, MLA_HEADS, MLA_NOPE + MLA_V)
    wk = ukv[..., :MLA_NOPE].reshape(KV_LORA, MLA_HEADS * MLA_NOPE)
    wvt = ukv[..., MLA_NOPE:].reshape(KV_LORA, MLA_HEADS * MLA_V).T
    return wd.astype(BF16), wq.astype(BF16), wqs.astype(BF16), wk.astype(BF16), wvt.astype(BF16)


def _rope_tables(seq):
    n_rows = seq // GRID_W
    row = jnp.repeat(jnp.arange(n_rows), GRID_W).astype(F32)
    col = jnp.tile(jnp.arange(GRID_W), n_rows).astype(F32)
    axis_dims = MLA_ROPE // 2
    inv = 1.0 / (ROPE_BASE ** (jnp.arange(0, axis_dims, 2, dtype=F32) / axis_dims))
    ang_r, ang_c = row[:, None] * inv, col[:, None] * inv
    cos64 = jnp.concatenate([jnp.cos(ang_r)] * 2 + [jnp.cos(ang_c)] * 2, axis=1)
    sin64 = jnp.concatenate([-jnp.sin(ang_r), jnp.sin(ang_r), -jnp.sin(ang_c), jnp.sin(ang_c)], axis=1)
    cos64 = jnp.concatenate([cos64, jnp.ones((TM, MLA_ROPE), F32)], axis=0)
    sin64 = jnp.concatenate([sin64, jnp.zeros((TM, MLA_ROPE), F32)], axis=0)
    n = seq + TM
    cos_t = jnp.concatenate([jnp.ones((n, MLA_NOPE), F32), cos64, jnp.zeros((n, MLA_HW - MLA_QK), F32)], axis=1)
    sin_t = jnp.concatenate([jnp.zeros((n, MLA_NOPE), F32), sin64, jnp.zeros((n, MLA_HW - MLA_QK), F32)], axis=1)
    return cos_t, sin_t


def kernel(x, c, ctx, c_ctx, mod_w, mod_b, norm_g, ffn_w_gate, ffn_w_up, ffn_w_down, hg_w_in, hg_w_out,
           hg_gn, hg_lb_logits, mla_w_down, mla_q_norm, mla_w_uq, mla_kv_norm, mla_w_ukv, mla_w_o, final_g):
    batch, seq, _ = x.shape
    rows = _Rows(batch, seq, ctx.shape[1])
    hs = (x.reshape(rows.n_lat, D_MODEL), ctx.reshape(rows.n_ctx, D_MODEL))

    mod_rows = -(-(batch + 1) // SUBLANES) * SUBLANES
    cond = jnp.concatenate([c, c_ctx[None], jnp.zeros((mod_rows - batch - 1, D_MODEL), F32)], axis=0)
    mod = _mod_call(cond, mod_w, mod_b)

    p_lb = jax.nn.softmax(hg_lb_logits.astype(F32), axis=0)
    lbs = jnp.maximum(jnp.cumsum(p_lb, axis=0) - p_lb[0:1], 0.0)
    tri = jnp.stack([jnp.tril(jnp.ones((HG_C, HG_C), F32)), jnp.triu(jnp.ones((HG_C, HG_C), F32))]).astype(BF16)
    head_id = np.arange(HG_HV) // HG_DV
    bd = jnp.asarray(head_id[:, None] == head_id[None, :], BF16)
    cos_t, sin_t = _rope_tables(seq)

    ffn_w = (ffn_w_gate.astype(BF16), ffn_w_up.astype(BF16), ffn_w_down.astype(BF16))

    for i in range(DEPTH):
        last = i == DEPTH - 1
        j = i // 2
        mod_l = mod[i].reshape(mod_rows, 1, N_MOD * D_MODEL)
        g = norm_g[i].reshape(3, 1, D_MODEL)
        hs = _ffn_call(rows, hs, mod_l, g[0], ffn_w, i, 0, 0, rows.all_tiles)
        n_out = rows.lat_tiles if last else rows.all_tiles
        if i % 2 == 0:
            pb, pf = _hgrn_in_call(rows, hs, mod_l, g[1], hg_w_in[j].astype(BF16), lbs[j])
            o_f, o_b = _hgrn_scan_call(rows, pb, pf, tri)
            gn = jnp.tile(hg_gn[j], HG_HEADS).reshape(1, HG_HV)
            mixer, mixer_args = "hgrn", (o_f, o_b, pf, gn, bd, hg_w_out[j].astype(BF16))
        else:
            w_dn, w_q, w_qs, w_k, w_vt = _mla_weights(mla_w_down[j], mla_w_uq[j], mla_w_ukv[j])
            q, k, vt = _mla_proj_call(rows, hs, mod_l, g[1], w_dn, mla_q_norm[j].reshape(1, Q_LORA),
                                      mla_kv_norm[j].reshape(1, KV_LORA), w_q, w_qs, w_k, w_vt, cos_t, sin_t)
            mixer, mixer_args = "mla", (_attn_call(rows, q, k, vt, with_ctx=not last), mla_w_o[j].astype(BF16))
        hs = _ffn_call(rows, hs, mod_l, g[2], ffn_w, i, 1, 6, n_out,
                       final_g=final_g.reshape(1, D_MODEL) if last else None, mixer=mixer, mixer_args=mixer_args)
    return hs.reshape(batch, seq, D_MODEL)
```

```python
import functools

import jax
import jax.numpy as jnp
import numpy as np
from jax import lax
from jax.experimental import pallas as pl
from jax.experimental.pallas import tpu as pltpu

F32 = jnp.float32
BF16 = jnp.bfloat16

D_MODEL = 1024
DEPTH = 4
GRID_W = 64
N_MOD = 9
FFN_HIDDEN = 2816
RMS_EPS = 1e-6
HG_HEADS = 8
HG_DK = 128
HG_DV = 128
HG_HK = HG_HEADS * HG_DK
HG_HV = HG_HEADS * HG_DV
HG_IN = 3 * HG_HK + HG_HV + D_MODEL
F_MIN = 1e-6
MLA_HEADS = 8
MLA_NOPE = 128
MLA_ROPE = 64
MLA_V = 128
Q_LORA = 384
KV_LORA = 256
MLA_QK = MLA_NOPE + MLA_ROPE
MLA_SCALE = MLA_QK ** -0.5
ROPE_BASE = 10000.0
LOG2_E = 1.4426950408889634

LANES = 128
SUBLANES = 8
MXU_DIM = 256
VMEM_LIMIT = 56 * 1024 * 1024

TM = 512
FFN_CHUNK = MXU_DIM
HG_C = 128
HG_SUB = 2
TQ = 256
Q_SUB = 8
KEY_CHUNK = 1024
MLA_HW = 2 * LANES
MLA_DOWN_P = Q_LORA + KV_LORA + 2 * LANES


def _cparams(sem):
    return pltpu.CompilerParams(dimension_semantics=sem, vmem_limit_bytes=VMEM_LIMIT)


def _resident(shape):
    nd = len(shape)
    return pl.BlockSpec(shape, lambda *_: (0,) * nd, pipeline_mode=pl.Buffered(1))


def _dot(a, b):
    return jnp.dot(a, b, preferred_element_type=F32)


def _dot_nt(a, b):
    return lax.dot_general(a, b, (((1,), (1,)), ((), ())), preferred_element_type=F32)


def _dot_tn(a, b):
    return lax.dot_general(a, b, (((0,), (0,)), ((), ())), preferred_element_type=F32)


def _rms_modulate(h, g, shift, scale):
    ms = jnp.mean(h * h, axis=-1, keepdims=True)
    return (h * lax.rsqrt(ms + RMS_EPS) * g) * (1.0 + scale) + shift


def _mod_slices(mod_ref, base, n):
    m = mod_ref[0]
    return [m[:, (base + j) * D_MODEL:(base + j + 1) * D_MODEL] for j in range(n)]


def _mod_kernel(c_ref, w_ref, b_ref, o_ref):
    c = c_ref[...]
    s = c * jax.nn.sigmoid(c)
    s_hi = s.astype(BF16)
    s_lo = (s - s_hi.astype(F32)).astype(BF16)
    w = w_ref[0]
    w_hi = w.astype(BF16)
    w_lo = (w - w_hi.astype(F32)).astype(BF16)
    acc = _dot(s_hi, w_hi) + (_dot(s_lo, w_hi) + _dot(s_hi, w_lo))
    o_ref[0] = acc + b_ref[0]


def _mod_call(cond, mod_w, mod_b):
    rows = cond.shape[0]
    tn = D_MODEL
    n_out = N_MOD * D_MODEL
    return pl.pallas_call(
        _mod_kernel,
        grid=(DEPTH, n_out // tn),
        in_specs=[
            pl.BlockSpec((rows, D_MODEL), lambda l, n: (0, 0)),
            pl.BlockSpec((1, D_MODEL, tn), lambda l, n: (l, 0, n)),
            pl.BlockSpec((1, 1, tn), lambda l, n: (l, 0, n)),
        ],
        out_specs=pl.BlockSpec((1, rows, tn), lambda l, n: (l, 0, n)),
        out_shape=jax.ShapeDtypeStruct((DEPTH, rows, n_out), F32),
        compiler_params=_cparams(("arbitrary", "arbitrary")),
        name="mod_vectors",
    )(cond, mod_w, mod_b.reshape(DEPTH, 1, n_out))


class _Rows:
    def __init__(self, batch, seq, ctx_len):
        self.batch, self.seq, self.ctx_len = batch, seq, ctx_len
        self.n_lat = batch * seq
        self.n_ctx = batch * ctx_len
        self.n_tok = self.n_lat + self.n_ctx
        assert seq % TM == 0 and self.n_ctx % TM == 0
        self.lat_tiles = self.n_lat // TM
        self.tiles_per_seq = seq // TM
        self.all_tiles = self.n_tok // TM

    def mod_spec(self):
        lat_tiles, tps, ctx_row = self.lat_tiles, self.tiles_per_seq, self.batch
        return pl.BlockSpec(
            (1, 1, N_MOD * D_MODEL),
            lambda i: (jnp.where(i < lat_tiles, i // tps, ctx_row), 0, 0))


def _tile_spec(width, col=0):
    return pl.BlockSpec((TM, width), lambda i: (i, col))


def _hgrn_readout(of_ref, ob_ref, gate_ref, gn_ref, bd_ref, w_ref):
    o = of_ref[...] + ob_ref[...]
    ss = _dot((o * o).astype(BF16), bd_ref[...])
    on = o * lax.rsqrt(ss * (1.0 / HG_DV) + RMS_EPS) * gn_ref[...]
    gt = gate_ref[...]
    return _dot((on * (gt * jax.nn.sigmoid(gt))).astype(BF16), w_ref[...])


def _ffn_kernel(*refs, base, final, mixer, lat_tiles):
    rest = list(refs)
    if lat_tiles is None:
        o_ref = rest.pop()
        h = rest.pop(0)[...]
    else:
        h_scr = rest.pop()
        o_ref = rest.pop()
        x_ref, c_ref = rest.pop(0), rest.pop(0)

        @pl.when(pl.program_id(0) < lat_tiles)
        def _():
            h_scr[...] = x_ref[...]

        @pl.when(pl.program_id(0) >= lat_tiles)
        def _():
            h_scr[...] = c_ref[...]

        h = h_scr[...]
    mod_ref, g_ref = rest.pop(0), rest.pop(0)
    if mixer is not None:
        (res_gate,) = _mod_slices(mod_ref, 5, 1)
        if mixer == "hgrn":
            y = _hgrn_readout(*rest[:6])
            rest = rest[6:]
        else:
            y = _dot(rest[0][...], rest[1][...])
            rest = rest[2:]
        h = h + res_gate * y
    wg_ref, wu_ref, wd_ref = rest[:3]
    shift, scale, gate = _mod_slices(mod_ref, base, 3)
    a = _rms_modulate(h, g_ref[...], shift, scale).astype(BF16)
    acc = jnp.zeros((TM, D_MODEL), F32)
    for j in range(FFN_HIDDEN // FFN_CHUNK):
        sl = slice(j * FFN_CHUNK, (j + 1) * FFN_CHUNK)
        gg = _dot(a, wg_ref[:, sl])
        uu = _dot(a, wu_ref[:, sl])
        hm = (gg * jax.nn.sigmoid(gg) * uu).astype(BF16)
        acc = acc + _dot(hm, wd_ref[sl, :])
    out = h + (0.5 * gate) * acc
    if final:
        fg_ref = rest[3]
        ms = jnp.mean(out * out, axis=-1, keepdims=True)
        out = out * lax.rsqrt(ms + RMS_EPS) * fg_ref[...]
    o_ref[...] = out


def _ffn_call(rows, hs, mod_l, g, ffn_w, layer, half, base, n_tiles, final_g=None, mixer=None, mixer_args=()):
    final = final_g is not None
    split_in = isinstance(hs, tuple)
    lat_tiles = rows.lat_tiles

    def stacked(shape):
        return pl.BlockSpec((None, None) + shape, lambda i: (layer, half, 0, 0), pipeline_mode=pl.Buffered(1))

    if split_in:
        in_specs = [pl.BlockSpec((TM, D_MODEL), lambda i: (jnp.minimum(i, lat_tiles - 1), 0)),
                    pl.BlockSpec((TM, D_MODEL), lambda i: (jnp.maximum(i - lat_tiles, 0), 0))]
        args = list(hs)
    else:
        in_specs, args = [_tile_spec(D_MODEL)], [hs]
    in_specs += [rows.mod_spec(), _resident((1, D_MODEL))]
    if mixer == "hgrn":
        in_specs += [_tile_spec(HG_HV), _tile_spec(HG_HV), _tile_spec(D_MODEL, col=HG_GATE),
                     _resident((1, HG_HV)), _resident((HG_HV, HG_HV)), _resident((HG_HV, D_MODEL))]
    elif mixer == "mla":
        in_specs += [_tile_spec(MLA_HEADS * MLA_V), _resident((MLA_HEADS * MLA_V, D_MODEL))]
    in_specs += [stacked((D_MODEL, FFN_HIDDEN)), stacked((D_MODEL, FFN_HIDDEN)), stacked((FFN_HIDDEN, D_MODEL))]
    args += [mod_l, g, *mixer_args, *ffn_w]
    if final:
        in_specs.append(_resident((1, D_MODEL)))
        args.append(final_g)
    return pl.pallas_call(
        functools.partial(_ffn_kernel, base=base, final=final, mixer=mixer,
                          lat_tiles=lat_tiles if split_in else None),
        grid=(n_tiles,),
        in_specs=in_specs,
        out_specs=_tile_spec(D_MODEL),
        out_shape=jax.ShapeDtypeStruct((n_tiles * TM, D_MODEL), F32),
        scratch_shapes=[pltpu.VMEM((TM, D_MODEL), F32)] if split_in else [],
        compiler_params=_cparams(("arbitrary",)),
        name="ffn_half",
    )(*args)


HG_Q, HG_V, HG_KF, HG_KB = range(4)
HG_LGF, HG_LGB, HG_GATE = range(3)


def _hgrn_in_kernel(h_ref, mod_ref, g_ref, w_ref, lb_ref, pb_ref, pf_ref):
    shift, scale = _mod_slices(mod_ref, 3, 2)
    a = _rms_modulate(h_ref[...], g_ref[...], shift, scale).astype(BF16)

    for piece in range(D_MODEL // MXU_DIM):
        def proj(j):
            return _dot(a, w_ref[:, pl.ds(j * D_MODEL + piece * MXU_DIM, MXU_DIM)])

        def put(ref, blk, val):
            ref[:, pl.ds(blk * D_MODEL + piece * MXU_DIM, MXU_DIM)] = val.astype(ref.dtype)

        put(pb_ref, HG_Q, proj(0))
        put(pb_ref, HG_V, proj(3))
        for d, (k_blk, lg_blk) in enumerate(((HG_KF, HG_LGF), (HG_KB, HG_LGB))):
            lb = lb_ref[d:d + 1, pl.ds(piece * MXU_DIM, MXU_DIM)]
            f = lb + (1.0 - lb) * jax.nn.sigmoid(proj(1 + d))
            put(pb_ref, k_blk, 1.0 - f)
            put(pf_ref, lg_blk, jnp.log2(jnp.maximum(f, F_MIN)))
        put(pf_ref, HG_GATE, proj(4))


def _hgrn_in_call(rows, hs, mod_l, g, w_in, lb):
    return pl.pallas_call(
        _hgrn_in_kernel,
        grid=(rows.all_tiles,),
        in_specs=[_tile_spec(D_MODEL), rows.mod_spec(), _resident((1, D_MODEL)),
                  _resident((D_MODEL, HG_IN)), _resident((2, HG_HK))],
        out_specs=[_tile_spec(4 * D_MODEL), _tile_spec(3 * D_MODEL)],
        out_shape=[jax.ShapeDtypeStruct((rows.n_tok, 4 * D_MODEL), BF16),
                   jax.ShapeDtypeStruct((rows.n_tok, 3 * D_MODEL), F32)],
        compiler_params=_cparams(("arbitrary",)),
        name="hgrn_in_proj",
    )(hs, mod_l, g, w_in, lb)


def _mla_proj_kernel(h_ref, mod_ref, g_ref, wd_ref, qn_ref, kvn_ref, wq_ref, wqs_ref, wk_ref, wvt_ref,
                     cos_ref, sin_ref, q_ref, k_ref, vt_ref):
    shift, scale = _mod_slices(mod_ref, 3, 2)
    a = _rms_modulate(h_ref[...], g_ref[...], shift, scale).astype(BF16)
    dp = _dot(a, wd_ref[...])
    cq = dp[:, :Q_LORA]
    ckv = dp[:, Q_LORA:Q_LORA + KV_LORA]
    kr_a = dp[:, Q_LORA + KV_LORA:Q_LORA + KV_LORA + LANES]
    kr_b = dp[:, Q_LORA + KV_LORA + LANES:]
    cqn = (cq * lax.rsqrt(jnp.mean(cq * cq, axis=-1, keepdims=True) + RMS_EPS) * qn_ref[...]).astype(BF16)
    ckvn = (ckv * lax.rsqrt(jnp.mean(ckv * ckv, axis=-1, keepdims=True) + RMS_EPS) * kvn_ref[...]).astype(BF16)
    cos = cos_ref[...]
    sin = sin_ref[...]
    cos_q = jnp.concatenate([cos * (MLA_SCALE * LOG2_E)] * MLA_HEADS, axis=1)
    sin_q = jnp.concatenate([sin * (MLA_SCALE * LOG2_E)] * MLA_HEADS, axis=1)
    q_ref[...] = (_dot(cqn, wq_ref[...]) * cos_q + _dot(cqn, wqs_ref[...]) * sin_q).astype(BF16)
    kr = (kr_a * cos[:, LANES:] + kr_b * sin[:, LANES:]).astype(BF16)
    kn = _dot(ckvn, wk_ref[...]).astype(BF16)
    pieces = []
    for hh in range(MLA_HEADS):
        pieces += [kn[:, hh * MLA_NOPE:(hh + 1) * MLA_NOPE], kr]
    k_ref[...] = jnp.concatenate(pieces, axis=1)
    vt_ref[...] = _dot_nt(wvt_ref[...], ckvn).astype(BF16)


def _mla_proj_call(rows, hs, mod_l, g, wd, qn, kvn, wq, wqs, wk, wvt, cos_t, sin_t):
    lat_tiles, tps = rows.lat_tiles, rows.tiles_per_seq
    rope_spec = pl.BlockSpec((TM, MLA_HW), lambda i: (jnp.where(i < lat_tiles, i % tps, tps), 0))
    qk_w = MLA_HEADS * MLA_HW
    v_w = MLA_HEADS * MLA_V
    return pl.pallas_call(
        _mla_proj_kernel,
        grid=(rows.all_tiles,),
        in_specs=[_tile_spec(D_MODEL), rows.mod_spec(), _resident((1, D_MODEL)),
                  _resident((D_MODEL, MLA_DOWN_P)), _resident((1, Q_LORA)), _resident((1, KV_LORA)),
                  _resident((Q_LORA, qk_w)), _resident((Q_LORA, qk_w)),
                  _resident((KV_LORA, MLA_HEADS * MLA_NOPE)), _resident((v_w, KV_LORA)),
                  rope_spec, rope_spec],
        out_specs=[_tile_spec(qk_w), _tile_spec(qk_w), pl.BlockSpec((v_w, TM), lambda i: (0, i))],
        out_shape=[jax.ShapeDtypeStruct((rows.n_tok, qk_w), BF16),
                   jax.ShapeDtypeStruct((rows.n_tok, qk_w), BF16),
                   jax.ShapeDtypeStruct((v_w, rows.n_tok), BF16)],
        compiler_params=_cparams(("arbitrary",)),
        name="mla_proj",
    )(hs, mod_l, g, wd, qn, kvn, wq, wqs, wk, wvt, cos_t, sin_t)


def _hgrn_cumsum(lg, tri):
    width = lg.shape[1]
    hi = lg.astype(BF16)
    lo = (lg - hi.astype(F32)).astype(BF16)
    c2 = _dot(tri, jnp.concatenate([hi, lo], axis=1))
    return c2[:, :width] + c2[:, width:]


def _block_diag(a, b):
    zero = jnp.zeros_like(a)
    return jnp.concatenate([jnp.concatenate([a, zero], axis=1), jnp.concatenate([zero, b], axis=1)], axis=0)


def _hgrn_pair(qb, kb, cum, vb, st_a_ref, st_b_ref, rev):
    c = HG_C
    pair_w = 2 * LANES
    row = lax.broadcasted_iota(jnp.int32, (c, pair_w), 0)
    col = lax.broadcasted_iota(jnp.int32, (c, pair_w), 1) & (LANES - 1)
    x = row ^ col
    causal = (row <= col) if rev else (row >= col)

    def halves(m):
        return m[:, :LANES], m[:, LANES:]

    def pair_scores(w_q, w_k):
        return _dot_nt(qb * w_q.astype(BF16), _block_diag(*halves(kb * w_k.astype(BF16))))

    edge = cum[0:1, :] if rev else cum[c - 1:c, :]
    st_a, st_b = st_a_ref[...], st_b_ref[...]
    o = _dot_nt(qb * jnp.exp2(cum).astype(BF16), _block_diag(st_a.astype(BF16), st_b.astype(BF16)))
    upd = _dot_tn(vb, kb * jnp.exp2(edge - cum).astype(BF16))
    dec_a, dec_b = halves(jnp.exp2(edge))
    st_a_ref[...] = st_a * dec_a + upd[:LANES, :LANES]
    st_b_ref[...] = st_b * dec_b + upd[LANES:, LANES:]

    mid_row = SUBLANES // 2 if rev else SUBLANES // 2 - 1
    ref8 = jnp.concatenate(
        [jnp.broadcast_to(cum[b * SUBLANES + mid_row:b * SUBLANES + mid_row + 1, :], (SUBLANES, pair_w))
         for b in range(c // SUBLANES)], axis=0)
    d8 = cum - ref8
    a = pair_scores(jnp.exp2(d8), jnp.exp2(-d8))
    lvl = 3
    while (1 << lvl) < c:
        half = 1 << lvl
        parts = []
        for b in range(c // (2 * half)):
            lo = cum[2 * b * half:(2 * b + 1) * half, :]
            hi = cum[(2 * b + 1) * half:(2 * b + 2) * half, :]
            if rev:
                ref = hi[0:1, :]
                parts += [lo - ref, ref - hi]
            else:
                ref = lo[half - 1:half, :]
                parts += [ref - lo, hi - ref]
        w = jnp.exp2(jnp.concatenate(parts, axis=0))
        a = jnp.where(x < half, a, pair_scores(w, w))
        lvl += 1
    a = jnp.where(causal, a, 0.0)
    return o + _dot(a.astype(BF16), _block_diag(*halves(vb)))


def _hgrn_scan_kernel(qf_ref, kf_ref, vf_ref, lgf_ref, qb_ref, kb_ref, vb_ref, lgb_ref, tri_ref,
                      of_ref, ob_ref, st_ref):
    @pl.when(pl.program_id(1) == 0)
    def _():
        st_ref[...] = jnp.zeros_like(st_ref)

    for sub in range(HG_SUB):
        rf = slice(sub * HG_C, (sub + 1) * HG_C)
        rb = slice((HG_SUB - 1 - sub) * HG_C, (HG_SUB - sub) * HG_C)
        cum_f = _hgrn_cumsum(lgf_ref[rf, :], tri_ref[0])
        cum_b = _hgrn_cumsum(lgb_ref[rb, :], tri_ref[1])
        for p in range(HG_HEADS // 2):
            cols = slice(p * 2 * LANES, (p + 1) * 2 * LANES)
            of_ref[rf, cols] = _hgrn_pair(qf_ref[rf, cols], kf_ref[rf, cols], cum_f[:, cols], vf_ref[rf, cols],
                                          st_ref.at[0, 2 * p], st_ref.at[0, 2 * p + 1], False)
            ob_ref[rb, cols] = _hgrn_pair(qb_ref[rb, cols], kb_ref[rb, cols], cum_b[:, cols], vb_ref[rb, cols],
                                          st_ref.at[1, 2 * p], st_ref.at[1, 2 * p + 1], True)


def _hgrn_scan_call(rows, pb, pf, tri):
    step_rows = HG_SUB * HG_C
    assert rows.ctx_len % step_rows == 0 and rows.seq % step_rows == 0
    c_chunks = rows.ctx_len // step_rows
    l_chunks = rows.seq // step_rows
    steps = c_chunks + l_chunks
    ctx0 = rows.n_lat // step_rows

    def fwd_blk(b, s):
        return jnp.where(s < c_chunks, ctx0 + b * c_chunks + s, b * l_chunks + s - c_chunks)

    def bwd_blk(b, s):
        return jnp.where(s < c_chunks, ctx0 + b * c_chunks + (c_chunks - 1 - s),
                         b * l_chunks + (l_chunks - 1 - (s - c_chunks)))

    def in_spec(blk, col):
        return pl.BlockSpec((step_rows, HG_HK), lambda b, s: (blk(b, s), col))

    out_sd = jax.ShapeDtypeStruct((rows.n_tok, HG_HV), F32)
    return pl.pallas_call(
        _hgrn_scan_kernel,
        grid=(rows.batch, steps),
        in_specs=[in_spec(fwd_blk, HG_Q), in_spec(fwd_blk, HG_KF), in_spec(fwd_blk, HG_V), in_spec(fwd_blk, HG_LGF),
                  in_spec(bwd_blk, HG_Q), in_spec(bwd_blk, HG_KB), in_spec(bwd_blk, HG_V), in_spec(bwd_blk, HG_LGB),
                  pl.BlockSpec((2, HG_C, HG_C), lambda b, s: (0, 0, 0))],
        out_specs=[pl.BlockSpec((step_rows, HG_HV), lambda b, s: (fwd_blk(b, s), 0)),
                   pl.BlockSpec((step_rows, HG_HV), lambda b, s: (bwd_blk(b, s), 0))],
        out_shape=[out_sd, out_sd],
        scratch_shapes=[pltpu.VMEM((2, HG_HEADS, HG_DV, HG_DK), F32)],
        compiler_params=_cparams(("arbitrary", "arbitrary")),
        name="hgrn_scan",
    )(pb, pb, pb, pf, pb, pb, pb, pf, tri)


def _attn_scores(q, kc_ref, kl_ref=None):
    scores = [_dot_nt(kc_ref[...], q)]
    if kl_ref is not None:
        scores += [_dot_nt(kl_ref[pl.ds(j * KEY_CHUNK, KEY_CHUNK), :], q)
                   for j in range(kl_ref.shape[0] // KEY_CHUNK)]
    m = jnp.max(scores[0], axis=0, keepdims=True)
    for s in scores[1:]:
        m = jnp.maximum(m, jnp.max(s, axis=0, keepdims=True))
    return scores, m


def _attn_values(scores, m, vtc_ref, vtl_ref=None):
    den, acc = None, None
    for j, s in enumerate(scores):
        p = jnp.exp2(s - m)
        vt = vtc_ref[...] if j == 0 else vtl_ref[:, pl.ds((j - 1) * KEY_CHUNK, KEY_CHUNK)]
        d_j = jnp.sum(p, axis=0, keepdims=True)
        a_j = _dot(vt, p.astype(BF16))
        den, acc = (d_j, a_j) if den is None else (den + d_j, acc + a_j)
    return (acc / den).T


def _attn_lat_kernel(q_ref, kl_ref, kc_ref, vtl_ref, vtc_ref, o_ref):
    nxt = _attn_scores(q_ref[pl.ds(0, TQ), :], kc_ref, kl_ref)
    for j in range(Q_SUB):
        cur = nxt
        if j + 1 < Q_SUB:
            nxt = _attn_scores(q_ref[pl.ds((j + 1) * TQ, TQ), :], kc_ref, kl_ref)
        o_ref[pl.ds(j * TQ, TQ), :] = _attn_values(*cur, vtc_ref, vtl_ref).astype(o_ref.dtype)


def _attn_ctx_kernel(q_ref, kc_ref, vtc_ref, lat_ref, o_ref):
    del lat_ref
    scores, m = _attn_scores(q_ref[...], kc_ref)
    o_ref[...] = _attn_values(scores, m, vtc_ref).astype(o_ref.dtype)


def _attn_call(rows, q, k, vt, with_ctx):
    tq = Q_SUB * TQ
    assert rows.ctx_len == TQ and rows.seq % tq == 0
    n_q = rows.seq // tq
    ctx0 = rows.n_lat // TQ
    out_sd = jax.ShapeDtypeStruct((rows.n_tok, MLA_HEADS * MLA_V), BF16)
    ctx_k = pl.BlockSpec((TQ, MLA_HW), lambda b, h, *_: (ctx0 + b, h))
    ctx_vt = pl.BlockSpec((MLA_V, TQ), lambda b, h, *_: (h, ctx0 + b))
    out = pl.pallas_call(
        _attn_lat_kernel,
        grid=(rows.batch, MLA_HEADS, n_q),
        in_specs=[pl.BlockSpec((tq, MLA_HW), lambda b, h, qi: (b * n_q + qi, h)),
                  pl.BlockSpec((rows.seq, MLA_HW), lambda b, h, qi: (b, h)), ctx_k,
                  pl.BlockSpec((MLA_V, rows.seq), lambda b, h, qi: (h, b)), ctx_vt],
        out_specs=pl.BlockSpec((tq, MLA_V), lambda b, h, qi: (b * n_q + qi, h)),
        out_shape=out_sd,
        compiler_params=_cparams(("arbitrary", "arbitrary", "arbitrary")),
        name="mla_attention",
    )(q, k, k, vt, vt)
    if not with_ctx:
        return out
    return pl.pallas_call(
        _attn_ctx_kernel,
        grid=(rows.batch, MLA_HEADS),
        in_specs=[ctx_k, ctx_k, ctx_vt, pl.BlockSpec(memory_space=pl.ANY)],
        out_specs=pl.BlockSpec((TQ, MLA_V), lambda b, h: (ctx0 + b, h)),
        out_shape=out_sd,
        input_output_aliases={3: 0},
        compiler_params=_cparams(("arbitrary", "arbitrary")),
        name="mla_attention_ctx",
    )(q, k, vt, out)


def _rope_swap_cols():
    idx = np.arange(MLA_ROPE).reshape(2, 2, MLA_ROPE // 4)
    return idx[:, ::-1, :].reshape(-1)


def _mla_weights(w_down, w_uq, w_ukv):
    swap = _rope_swap_cols()
    kr = w_down[:, Q_LORA + KV_LORA:]
    zpad = jnp.zeros((D_MODEL, LANES - MLA_ROPE), w_down.dtype)
    wd = jnp.concatenate([w_down[:, :Q_LORA + KV_LORA], kr, zpad, kr[:, swap], zpad], axis=1)
    uq = w_uq.reshape(Q_LORA, MLA_HEADS, MLA_QK)
    nope, rope = uq[..., :MLA_NOPE], uq[..., MLA_NOPE:]
    z64 = jnp.zeros((Q_LORA, MLA_HEADS, MLA_HW - MLA_QK), w_uq.dtype)
    z128 = jnp.zeros((Q_LORA, MLA_HEADS, MLA_NOPE), w_uq.dtype)
    wq = jnp.concatenate([nope, rope, z64], axis=-1).reshape(Q_LORA, MLA_HEADS * MLA_HW)
    wqs = jnp.concatenate([z128, rope[..., swap], z64], axis=-1).reshape(Q_LORA, MLA_HEADS * MLA_HW)
    ukv = w_ukv.reshape(KV_LORA, MLA_HEADS, MLA_NOPE + MLA_V)
    wk = ukv[..., :MLA_NOPE].reshape(KV_LORA, MLA_HEADS * MLA_NOPE)
    wvt = ukv[..., MLA_NOPE:].reshape(KV_LORA, MLA_HEADS * MLA_V).T
    return wd.astype(BF16), wq.astype(BF16), wqs.astype(BF16), wk.astype(BF16), wvt.astype(BF16)


def _rope_tables(seq):
    n_rows = seq // GRID_W
    row = jnp.repeat(jnp.arange(n_rows), GRID_W).astype(F32)
    col = jnp.tile(jnp.arange(GRID_W), n_rows).astype(F32)
    axis_dims = MLA_ROPE // 2
    inv = 1.0 / (ROPE_BASE ** (jnp.arange(0, axis_dims, 2, dtype=F32) / axis_dims))
    ang_r, ang_c = row[:, None] * inv, col[:, None] * inv
    cos64 = jnp.concatenate([jnp.cos(ang_r)] * 2 + [jnp.cos(ang_c)] * 2, axis=1)
    sin64 = jnp.concatenate([-jnp.sin(ang_r), jnp.sin(ang_r), -jnp.sin(ang_c), jnp.sin(ang_c)], axis=1)
    cos64 = jnp.concatenate([cos64, jnp.ones((TM, MLA_ROPE), F32)], axis=0)
    sin64 = jnp.concatenate([sin64, jnp.zeros((TM, MLA_ROPE), F32)], axis=0)
    n = seq + TM
    cos_t = jnp.concatenate([jnp.ones((n, MLA_NOPE), F32), cos64, jnp.zeros((n, MLA_HW - MLA_QK), F32)], axis=1)
    sin_t = jnp.concatenate([jnp.zeros((n, MLA_NOPE), F32), sin64, jnp.zeros((n, MLA_HW - MLA_QK), F32)], axis=1)
    return cos_t, sin_t


def kernel(x, c, ctx, c_ctx, mod_w, mod_b, norm_g, ffn_w_gate, ffn_w_up, ffn_w_down, hg_w_in, hg_w_out,
           hg_gn, hg_lb_logits, mla_w_down, mla_q_norm, mla_w_uq, mla_kv_norm, mla_w_ukv, mla_w_o, final_g):
    batch, seq, _ = x.shape
    rows = _Rows(batch, seq, ctx.shape[1])
    hs = (x.reshape(rows.n_lat, D_MODEL), ctx.reshape(rows.n_ctx, D_MODEL))

    mod_rows = -(-(batch + 1) // SUBLANES) * SUBLANES
    cond = jnp.concatenate([c, c_ctx[None], jnp.zeros((mod_rows - batch - 1, D_MODEL), F32)], axis=0)
    mod = _mod_call(cond, mod_w, mod_b)

    p_lb = jax.nn.softmax(hg_lb_logits.astype(F32), axis=0)
    lbs = jnp.maximum(jnp.cumsum(p_lb, axis=0) - p_lb[0:1], 0.0)
    tri = jnp.stack([jnp.tril(jnp.ones((HG_C, HG_C), F32)), jnp.triu(jnp.ones((HG_C, HG_C), F32))]).astype(BF16)
    head_id = np.arange(HG_HV) // HG_DV
    bd = jnp.asarray(head_id[:, None] == head_id[None, :], BF16)
    cos_t, sin_t = _rope_tables(seq)

    ffn_w = (ffn_w_gate.astype(BF16), ffn_w_up.astype(BF16), ffn_w_down.astype(BF16))

    for i in range(DEPTH):
        last = i == DEPTH - 1
        j = i // 2
        mod_l = mod[i].reshape(mod_rows, 1, N_MOD * D_MODEL)
        g = norm_g[i].reshape(3, 1, D_MODEL)
        hs = _ffn_call(rows, hs, mod_l, g[0], ffn_w, i, 0, 0, rows.all_tiles)
        n_out = rows.lat_tiles if last else rows.all_tiles
        if i % 2 == 0:
            pb, pf = _hgrn_in_call(rows, hs, mod_l, g[1], hg_w_in[j].astype(BF16), lbs[j])
            o_f, o_b = _hgrn_scan_call(rows, pb, pf, tri)
            gn = jnp.tile(hg_gn[j], HG_HEADS).reshape(1, HG_HV)
            mixer, mixer_args = "hgrn", (o_f, o_b, pf, gn, bd, hg_w_out[j].astype(BF16))
        else:
            w_dn, w_q, w_qs, w_k, w_vt = _mla_weights(mla_w_down[j], mla_w_uq[j], mla_w_ukv[j])
            q, k, vt = _mla_proj_call(rows, hs, mod_l, g[1], w_dn, mla_q_norm[j].reshape(1, Q_LORA),
                                      mla_kv_norm[j].reshape(1, KV_LORA), w_q, w_qs, w_k, w_vt, cos_t, sin_t)
            mixer, mixer_args = "mla", (_attn_call(rows, q, k, vt, with_ctx=not last), mla_w_o[j].astype(BF16))
        hs = _ffn_call(rows, hs, mod_l, g[2], ffn_w, i, 1, 6, n_out,
                       final_g=final_g.reshape(1, D_MODEL) if last else None, mixer=mixer, mixer_args=mixer_args)
    return hs.reshape(batch, seq, D_MODEL)
```

```python
import functools

import jax
import jax.numpy as jnp
import numpy as np
from jax import lax
from jax.experimental import pallas as pl
from jax.experimental.pallas import tpu as pltpu

F32 = jnp.float32
BF16 = jnp.bfloat16

D_MODEL = 1024
DEPTH = 4
GRID_W = 64
N_MOD = 9
FFN_HIDDEN = 2816
RMS_EPS = 1e-6
HG_HEADS = 8
HG_DK = 128
HG_DV = 128
HG_HK = HG_HEADS * HG_DK
HG_HV = HG_HEADS * HG_DV
HG_IN = 3 * HG_HK + HG_HV + D_MODEL
F_MIN = 1e-6
MLA_HEADS = 8
MLA_NOPE = 128
MLA_ROPE = 64
MLA_V = 128
Q_LORA = 384
KV_LORA = 256
MLA_QK = MLA_NOPE + MLA_ROPE
MLA_SCALE = MLA_QK ** -0.5
ROPE_BASE = 10000.0
LOG2_E = 1.4426950408889634

LANES = 128
SUBLANES = 8
MXU_DIM = 256
VMEM_LIMIT = 56 * 1024 * 1024

TM = 512
FFN_CHUNK = MXU_DIM
HG_C = 128
HG_SUB = 2
TQ = 256
Q_SUB = 8
KEY_CHUNK = 1024
MLA_HW = 2 * LANES
MLA_DOWN_P = Q_LORA + KV_LORA + 2 * LANES


def _cparams(sem):
    return pltpu.CompilerParams(dimension_semantics=sem, vmem_limit_bytes=VMEM_LIMIT)


def _resident(shape):
    nd = len(shape)
    return pl.BlockSpec(shape, lambda *_: (0,) * nd, pipeline_mode=pl.Buffered(1))


def _dot(a, b):
    return jnp.dot(a, b, preferred_element_type=F32)


def _dot_nt(a, b):
    return lax.dot_general(a, b, (((1,), (1,)), ((), ())), preferred_element_type=F32)


def _rms_modulate(h, g, shift, scale):
    ms = jnp.mean(h * h, axis=-1, keepdims=True)
    return (h * lax.rsqrt(ms + RMS_EPS) * g) * (1.0 + scale) + shift


def _mod_slices(mod_ref, base, n):
    m = mod_ref[0]
    return [m[:, (base + j) * D_MODEL:(base + j + 1) * D_MODEL] for j in range(n)]


def _mod_kernel(c_ref, w_ref, b_ref, o_ref):
    c = c_ref[...]
    s = c * jax.nn.sigmoid(c)
    s_hi = s.astype(BF16)
    s_lo = (s - s_hi.astype(F32)).astype(BF16)
    w = w_ref[0]
    w_hi = w.astype(BF16)
    w_lo = (w - w_hi.astype(F32)).astype(BF16)
    acc = _dot(s_hi, w_hi) + (_dot(s_lo, w_hi) + _dot(s_hi, w_lo))
    o_ref[0] = acc + b_ref[0]


def _mod_call(cond, mod_w, mod_b):
    rows = cond.shape[0]
    tn = D_MODEL
    n_out = N_MOD * D_MODEL
    return pl.pallas_call(
        _mod_kernel,
        grid=(DEPTH, n_out // tn),
        in_specs=[
            pl.BlockSpec((rows, D_MODEL), lambda l, n: (0, 0)),
            pl.BlockSpec((1, D_MODEL, tn), lambda l, n: (l, 0, n)),
            pl.BlockSpec((1, 1, tn), lambda l, n: (l, 0, n)),
        ],
        out_specs=pl.BlockSpec((1, rows, tn), lambda l, n: (l, 0, n)),
        out_shape=jax.ShapeDtypeStruct((DEPTH, rows, n_out), F32),
        compiler_params=_cparams(("arbitrary", "arbitrary")),
        name="mod_vectors",
    )(cond, mod_w, mod_b.reshape(DEPTH, 1, n_out))


class _Rows:
    def __init__(self, batch, seq, ctx_len):
        self.batch, self.seq, self.ctx_len = batch, seq, ctx_len
        self.n_lat = batch * seq
        self.n_ctx = batch * ctx_len
        self.n_tok = self.n_lat + self.n_ctx
        assert seq % TM == 0 and self.n_ctx % TM == 0
        self.lat_tiles = self.n_lat // TM
        self.tiles_per_seq = seq // TM
        self.all_tiles = self.n_tok // TM

    def mod_spec(self):
        lat_tiles, tps, ctx_row = self.lat_tiles, self.tiles_per_seq, self.batch
        return pl.BlockSpec(
            (1, 1, N_MOD * D_MODEL),
            lambda i: (jnp.where(i < lat_tiles, i // tps, ctx_row), 0, 0))


def _tile_spec(width, col=0):
    return pl.BlockSpec((TM, width), lambda i: (i, col))


def _hgrn_readout(of_ref, ob_ref, gate_ref, gn_ref, bd_ref, w_ref):
    o = of_ref[...] + ob_ref[...]
    ss = _dot((o * o).astype(BF16), bd_ref[...])
    on = o * lax.rsqrt(ss * (1.0 / HG_DV) + RMS_EPS) * gn_ref[...]
    gt = gate_ref[...]
    return _dot((on * (gt * jax.nn.sigmoid(gt))).astype(BF16), w_ref[...])


def _ffn_kernel(*refs, base, final, mixer, lat_tiles):
    rest = list(refs)
    if lat_tiles is None:
        o_ref = rest.pop()
        h = rest.pop(0)[...]
    else:
        h_scr = rest.pop()
        o_ref = rest.pop()
        x_ref, c_ref = rest.pop(0), rest.pop(0)

        @pl.when(pl.program_id(0) < lat_tiles)
        def _():
            h_scr[...] = x_ref[...]

        @pl.when(pl.program_id(0) >= lat_tiles)
        def _():
            h_scr[...] = c_ref[...]

        h = h_scr[...]
    mod_ref, g_ref = rest.pop(0), rest.pop(0)
    if mixer is not None:
        (res_gate,) = _mod_slices(mod_ref, 5, 1)
        if mixer == "hgrn":
            y = _hgrn_readout(*rest[:6])
            rest = rest[6:]
        else:
            y = _dot(rest[0][...], rest[1][...])
            rest = rest[2:]
        h = h + res_gate * y
    wg_ref, wu_ref, wd_ref = rest[:3]
    shift, scale, gate = _mod_slices(mod_ref, base, 3)
    a = _rms_modulate(h, g_ref[...], shift, scale).astype(BF16)
    acc = jnp.zeros((TM, D_MODEL), F32)
    for j in range(FFN_HIDDEN // FFN_CHUNK):
        sl = slice(j * FFN_CHUNK, (j + 1) * FFN_CHUNK)
        gg = _dot(a, wg_ref[:, sl])
        uu = _dot(a, wu_ref[:, sl])
        hm = (gg * jax.nn.sigmoid(gg) * uu).astype(BF16)
        acc = acc + _dot(hm, wd_ref[sl, :])
    out = h + (0.5 * gate) * acc
    if final:
        fg_ref = rest[3]
        ms = jnp.mean(out * out, axis=-1, keepdims=True)
        out = out * lax.rsqrt(ms + RMS_EPS) * fg_ref[...]
    o_ref[...] = out


def _ffn_call(rows, hs, mod_l, g, ffn_w, layer, half, base, n_tiles, final_g=None, mixer=None, mixer_args=()):
    final = final_g is not None
    split_in = isinstance(hs, tuple)
    lat_tiles = rows.lat_tiles

    def stacked(shape):
        return pl.BlockSpec((None, None) + shape, lambda i: (layer, half, 0, 0), pipeline_mode=pl.Buffered(1))

    if split_in:
        in_specs = [pl.BlockSpec((TM, D_MODEL), lambda i: (jnp.minimum(i, lat_tiles - 1), 0)),
                    pl.BlockSpec((TM, D_MODEL), lambda i: (jnp.maximum(i - lat_tiles, 0), 0))]
        args = list(hs)
    else:
        in_specs, args = [_tile_spec(D_MODEL)], [hs]
    in_specs += [rows.mod_spec(), _resident((1, D_MODEL))]
    if mixer == "hgrn":
        in_specs += [_tile_spec(HG_HV), _tile_spec(HG_HV), _tile_spec(D_MODEL, col=HG_GATE),
                     _resident((1, HG_HV)), _resident((HG_HV, HG_HV)), _resident((HG_HV, D_MODEL))]
    elif mixer == "mla":
        in_specs += [_tile_spec(MLA_HEADS * MLA_V), _resident((MLA_HEADS * MLA_V, D_MODEL))]
    in_specs += [stacked((D_MODEL, FFN_HIDDEN)), stacked((D_MODEL, FFN_HIDDEN)), stacked((FFN_HIDDEN, D_MODEL))]
    args += [mod_l, g, *mixer_args, *ffn_w]
    if final:
        in_specs.append(_resident((1, D_MODEL)))
        args.append(final_g)
    return pl.pallas_call(
        functools.partial(_ffn_kernel, base=base, final=final, mixer=mixer,
                          lat_tiles=lat_tiles if split_in else None),
        grid=(n_tiles,),
        in_specs=in_specs,
        out_specs=_tile_spec(D_MODEL),
        out_shape=jax.ShapeDtypeStruct((n_tiles * TM, D_MODEL), F32),
        scratch_shapes=[pltpu.VMEM((TM, D_MODEL), F32)] if split_in else [],
        compiler_params=_cparams(("arbitrary",)),
        name="ffn_half",
    )(*args)


HG_Q, HG_KF, HG_KB = range(3)
HG_LGF, HG_LGB, HG_GATE = range(3)
HG_ZF, HG_ZB, HG_I, HG_G = 1, 2, 3, 4


def _hgrn_in_kernel(h_ref, mod_ref, g_ref, w_ref, wvt_ref, lb_ref, pb_ref, pf_ref, vt_ref):
    shift, scale = _mod_slices(mod_ref, 3, 2)
    a = _rms_modulate(h_ref[...], g_ref[...], shift, scale).astype(BF16)

    for piece in range(D_MODEL // MXU_DIM):
        def proj(j):
            return _dot(a, w_ref[:, pl.ds(j * D_MODEL + piece * MXU_DIM, MXU_DIM)])

        def put(ref, blk, val):
            ref[:, pl.ds(blk * D_MODEL + piece * MXU_DIM, MXU_DIM)] = val.astype(ref.dtype)

        put(pb_ref, HG_Q, proj(0))
        vt_rows = pl.ds(piece * MXU_DIM, MXU_DIM)
        vt_ref[vt_rows, :] = _dot_nt(wvt_ref[vt_rows, :], a).astype(BF16)
        for d, (k_blk, lg_blk) in enumerate(((HG_KF, HG_LGF), (HG_KB, HG_LGB))):
            lb = lb_ref[d:d + 1, pl.ds(piece * MXU_DIM, MXU_DIM)]
            f = lb + (1.0 - lb) * jax.nn.sigmoid(proj(HG_ZF + d))
            put(pb_ref, k_blk, 1.0 - f)
            put(pf_ref, lg_blk, jnp.log2(jnp.maximum(f, F_MIN)))
        put(pf_ref, HG_GATE, proj(HG_G))


def _hgrn_in_call(rows, hs, mod_l, g, w_in, w_vt, lb):
    return pl.pallas_call(
        _hgrn_in_kernel,
        grid=(rows.all_tiles,),
        in_specs=[_tile_spec(D_MODEL), rows.mod_spec(), _resident((1, D_MODEL)),
                  _resident((D_MODEL, HG_IN)), _resident((HG_HV, D_MODEL)), _resident((2, HG_HK))],
        out_specs=[_tile_spec(3 * D_MODEL), _tile_spec(3 * D_MODEL), pl.BlockSpec((HG_HV, TM), lambda i: (0, i))],
        out_shape=[jax.ShapeDtypeStruct((rows.n_tok, 3 * D_MODEL), BF16),
                   jax.ShapeDtypeStruct((rows.n_tok, 3 * D_MODEL), F32),
                   jax.ShapeDtypeStruct((HG_HV, rows.n_tok), BF16)],
        compiler_params=_cparams(("arbitrary",)),
        name="hgrn_in_proj",
    )(hs, mod_l, g, w_in, w_vt, lb)


def _mla_proj_kernel(h_ref, mod_ref, g_ref, wd_ref, qn_ref, kvn_ref, wq_ref, wqs_ref, wk_ref, wvt_ref,
                     cos_ref, sin_ref, q_ref, k_ref, vt_ref):
    shift, scale = _mod_slices(mod_ref, 3, 2)
    a = _rms_modulate(h_ref[...], g_ref[...], shift, scale).astype(BF16)
    dp = _dot(a, wd_ref[...])
    cq = dp[:, :Q_LORA]
    ckv = dp[:, Q_LORA:Q_LORA + KV_LORA]
    kr_a = dp[:, Q_LORA + KV_LORA:Q_LORA + KV_LORA + LANES]
    kr_b = dp[:, Q_LORA + KV_LORA + LANES:]
    cqn = (cq * lax.rsqrt(jnp.mean(cq * cq, axis=-1, keepdims=True) + RMS_EPS) * qn_ref[...]).astype(BF16)
    ckvn = (ckv * lax.rsqrt(jnp.mean(ckv * ckv, axis=-1, keepdims=True) + RMS_EPS) * kvn_ref[...]).astype(BF16)
    cos = cos_ref[...]
    sin = sin_ref[...]
    cos_r, sin_r = cos[:, LANES:], sin[:, LANES:]
    q_scale = MLA_SCALE * LOG2_E
    qm = _dot(cqn, wq_ref[...])
    qs = _dot(cqn, wqs_ref[...])
    pieces = []
    for hh in range(MLA_HEADS):
        nope = qm[:, hh * MLA_HW:hh * MLA_HW + LANES] * q_scale
        rope = (qm[:, hh * MLA_HW + LANES:(hh + 1) * MLA_HW] * (cos_r * q_scale)
                + qs[:, hh * LANES:(hh + 1) * LANES] * (sin_r * q_scale))
        pieces += [nope.astype(BF16), rope.astype(BF16)]
    q_ref[...] = jnp.concatenate(pieces, axis=1)
    kr = (kr_a * cos_r + kr_b * sin_r).astype(BF16)
    kn = _dot(ckvn, wk_ref[...]).astype(BF16)
    pieces = []
    for hh in range(MLA_HEADS):
        pieces += [kn[:, hh * MLA_NOPE:(hh + 1) * MLA_NOPE], kr]
    k_ref[...] = jnp.concatenate(pieces, axis=1)
    vt_ref[...] = _dot_nt(wvt_ref[...], ckvn).astype(BF16)


def _mla_proj_call(rows, hs, mod_l, g, wd, qn, kvn, wq, wqs, wk, wvt, cos_t, sin_t):
    lat_tiles, tps = rows.lat_tiles, rows.tiles_per_seq
    rope_spec = pl.BlockSpec((TM, MLA_HW), lambda i: (jnp.where(i < lat_tiles, i % tps, tps), 0))
    qk_w = MLA_HEADS * MLA_HW
    v_w = MLA_HEADS * MLA_V
    return pl.pallas_call(
        _mla_proj_kernel,
        grid=(rows.all_tiles,),
        in_specs=[_tile_spec(D_MODEL), rows.mod_spec(), _resident((1, D_MODEL)),
                  _resident((D_MODEL, MLA_DOWN_P)), _resident((1, Q_LORA)), _resident((1, KV_LORA)),
                  _resident((Q_LORA, qk_w)), _resident((Q_LORA, MLA_HEADS * LANES)),
                  _resident((KV_LORA, MLA_HEADS * MLA_NOPE)), _resident((v_w, KV_LORA)),
                  rope_spec, rope_spec],
        out_specs=[_tile_spec(qk_w), _tile_spec(qk_w), pl.BlockSpec((v_w, TM), lambda i: (0, i))],
        out_shape=[jax.ShapeDtypeStruct((rows.n_tok, qk_w), BF16),
                   jax.ShapeDtypeStruct((rows.n_tok, qk_w), BF16),
                   jax.ShapeDtypeStruct((v_w, rows.n_tok), BF16)],
        compiler_params=_cparams(("arbitrary",)),
        name="mla_proj",
    )(hs, mod_l, g, wd, qn, kvn, wq, wqs, wk, wvt, cos_t, sin_t)


def _hgrn_cumsum(lg, tri):
    width = lg.shape[1]
    hi = lg.astype(BF16)
    lo = (lg - hi.astype(F32)).astype(BF16)
    c2 = _dot(tri, jnp.concatenate([hi, lo], axis=1))
    return c2[:, :width] + c2[:, width:]


def _block_diag(a, b):
    zero = jnp.zeros_like(a)
    return jnp.concatenate([jnp.concatenate([a, zero], axis=1), jnp.concatenate([zero, b], axis=1)], axis=0)


def _hgrn_pair(qb, kb, cum, vt, st_a_ref, st_b_ref, rev):
    c = HG_C
    pair_w = 2 * LANES
    row = lax.broadcasted_iota(jnp.int32, (c, pair_w), 0)
    col = lax.broadcasted_iota(jnp.int32, (c, pair_w), 1) & (LANES - 1)
    x = row ^ col
    causal = (row <= col) if rev else (row >= col)

    def halves(m):
        return m[:, :LANES], m[:, LANES:]

    def pair_scores(w_q, w_k):
        return _dot_nt(qb * w_q.astype(BF16), _block_diag(*halves(kb * w_k.astype(BF16))))

    edge = cum[0:1, :] if rev else cum[c - 1:c, :]
    st_a, st_b = st_a_ref[...], st_b_ref[...]
    o = _dot_nt(qb * jnp.exp2(cum).astype(BF16), _block_diag(st_a.astype(BF16), st_b.astype(BF16)))
    upd = _dot(vt, kb * jnp.exp2(edge - cum).astype(BF16))
    dec_a, dec_b = halves(jnp.exp2(edge))
    st_a_ref[...] = st_a * dec_a + upd[:LANES, :LANES]
    st_b_ref[...] = st_b * dec_b + upd[LANES:, LANES:]

    mid_row = SUBLANES // 2 if rev else SUBLANES // 2 - 1
    ref8 = jnp.concatenate(
        [jnp.broadcast_to(cum[b * SUBLANES + mid_row:b * SUBLANES + mid_row + 1, :], (SUBLANES, pair_w))
         for b in range(c // SUBLANES)], axis=0)
    d8 = cum - ref8
    a = pair_scores(jnp.exp2(d8), jnp.exp2(-d8))
    lvl = 3
    while (1 << lvl) < c:
        half = 1 << lvl
        parts = []
        for b in range(c // (2 * half)):
            lo = cum[2 * b * half:(2 * b + 1) * half, :]
            hi = cum[(2 * b + 1) * half:(2 * b + 2) * half, :]
            if rev:
                ref = hi[0:1, :]
                parts += [lo - ref, ref - hi]
            else:
                ref = lo[half - 1:half, :]
                parts += [ref - lo, hi - ref]
        w = jnp.exp2(jnp.concatenate(parts, axis=0))
        a = jnp.where(x < half, a, pair_scores(w, w))
        lvl += 1
    a = jnp.where(causal, a, 0.0)
    return o + _dot_nt(a.astype(BF16), _block_diag(vt[:LANES, :], vt[LANES:, :]))


def _hgrn_scan_kernel(qf_ref, kf_ref, vtf_ref, lgf_ref, qb_ref, kb_ref, vtb_ref, lgb_ref, tri_ref,
                      of_ref, ob_ref, st_ref):
    @pl.when(pl.program_id(1) == 0)
    def _():
        st_ref[...] = jnp.zeros_like(st_ref)

    for sub in range(HG_SUB):
        rf = slice(sub * HG_C, (sub + 1) * HG_C)
        rb = slice((HG_SUB - 1 - sub) * HG_C, (HG_SUB - sub) * HG_C)
        cum_f = _hgrn_cumsum(lgf_ref[rf, :], tri_ref[0])
        cum_b = _hgrn_cumsum(lgb_ref[rb, :], tri_ref[1])
        for p in range(HG_HEADS // 2):
            cols = slice(p * 2 * LANES, (p + 1) * 2 * LANES)
            of_ref[rf, cols] = _hgrn_pair(qf_ref[rf, cols], kf_ref[rf, cols], cum_f[:, cols], vtf_ref[cols, rf],
                                          st_ref.at[0, 2 * p], st_ref.at[0, 2 * p + 1], False)
            ob_ref[rb, cols] = _hgrn_pair(qb_ref[rb, cols], kb_ref[rb, cols], cum_b[:, cols], vtb_ref[cols, rb],
                                          st_ref.at[1, 2 * p], st_ref.at[1, 2 * p + 1], True)


def _hgrn_scan_call(rows, pb, pf, vt, tri):
    step_rows = HG_SUB * HG_C
    assert rows.ctx_len % step_rows == 0 and rows.seq % step_rows == 0
    c_chunks = rows.ctx_len // step_rows
    l_chunks = rows.seq // step_rows
    steps = c_chunks + l_chunks
    ctx0 = rows.n_lat // step_rows

    def fwd_blk(b, s):
        return jnp.where(s < c_chunks, ctx0 + b * c_chunks + s, b * l_chunks + s - c_chunks)

    def bwd_blk(b, s):
        return jnp.where(s < c_chunks, ctx0 + b * c_chunks + (c_chunks - 1 - s),
                         b * l_chunks + (l_chunks - 1 - (s - c_chunks)))

    def in_spec(blk, col):
        return pl.BlockSpec((step_rows, HG_HK), lambda b, s: (blk(b, s), col))

    def vt_spec(blk):
        return pl.BlockSpec((HG_HV, step_rows), lambda b, s: (0, blk(b, s)))

    out_sd = jax.ShapeDtypeStruct((rows.n_tok, HG_HV), F32)
    return pl.pallas_call(
        _hgrn_scan_kernel,
        grid=(rows.batch, steps),
        in_specs=[in_spec(fwd_blk, HG_Q), in_spec(fwd_blk, HG_KF), vt_spec(fwd_blk), in_spec(fwd_blk, HG_LGF),
                  in_spec(bwd_blk, HG_Q), in_spec(bwd_blk, HG_KB), vt_spec(bwd_blk), in_spec(bwd_blk, HG_LGB),
                  pl.BlockSpec((2, HG_C, HG_C), lambda b, s: (0, 0, 0))],
        out_specs=[pl.BlockSpec((step_rows, HG_HV), lambda b, s: (fwd_blk(b, s), 0)),
                   pl.BlockSpec((step_rows, HG_HV), lambda b, s: (bwd_blk(b, s), 0))],
        out_shape=[out_sd, out_sd],
        scratch_shapes=[pltpu.VMEM((2, HG_HEADS, HG_DV, HG_DK), F32)],
        compiler_params=_cparams(("arbitrary", "arbitrary")),
        name="hgrn_scan",
    )(pb, pb, vt, pf, pb, pb, vt, pf, tri)


def _attn_scores(q, kc_ref, kl_ref=None):
    scores = [_dot_nt(kc_ref[...], q)]
    if kl_ref is not None:
        scores += [_dot_nt(kl_ref[pl.ds(j * KEY_CHUNK, KEY_CHUNK), :], q)
                   for j in range(kl_ref.shape[0] // KEY_CHUNK)]
    m = jnp.max(scores[0], axis=0, keepdims=True)
    for s in scores[1:]:
        m = jnp.maximum(m, jnp.max(s, axis=0, keepdims=True))
    return scores, m


def _attn_values(scores, m, vtc_ref, vtl_ref=None):
    den, acc = None, None
    for j, s in enumerate(scores):
        p = jnp.exp2(s - m)
        vt = vtc_ref[...] if j == 0 else vtl_ref[:, pl.ds((j - 1) * KEY_CHUNK, KEY_CHUNK)]
        d_j = jnp.sum(p, axis=0, keepdims=True)
        a_j = _dot(vt, p.astype(BF16))
        den, acc = (d_j, a_j) if den is None else (den + d_j, acc + a_j)
    return (acc / den).T


def _attn_lat_kernel(q_ref, kl_ref, kc_ref, vtl_ref, vtc_ref, o_ref):
    nxt = _attn_scores(q_ref[pl.ds(0, TQ), :], kc_ref, kl_ref)
    for j in range(Q_SUB):
        cur = nxt
        if j + 1 < Q_SUB:
            nxt = _attn_scores(q_ref[pl.ds((j + 1) * TQ, TQ), :], kc_ref, kl_ref)
        o_ref[pl.ds(j * TQ, TQ), :] = _attn_values(*cur, vtc_ref, vtl_ref).astype(o_ref.dtype)


def _attn_ctx_kernel(q_ref, kc_ref, vtc_ref, lat_ref, o_ref):
    del lat_ref
    scores, m = _attn_scores(q_ref[...], kc_ref)
    o_ref[...] = _attn_values(scores, m, vtc_ref).astype(o_ref.dtype)


def _attn_call(rows, q, k, vt, with_ctx):
    tq = Q_SUB * TQ
    assert rows.ctx_len == TQ and rows.seq % tq == 0
    n_q = rows.seq // tq
    ctx0 = rows.n_lat // TQ
    out_sd = jax.ShapeDtypeStruct((rows.n_tok, MLA_HEADS * MLA_V), BF16)
    ctx_k = pl.BlockSpec((TQ, MLA_HW), lambda b, h, *_: (ctx0 + b, h))
    ctx_vt = pl.BlockSpec((MLA_V, TQ), lambda b, h, *_: (h, ctx0 + b))
    out = pl.pallas_call(
        _attn_lat_kernel,
        grid=(rows.batch, MLA_HEADS, n_q),
        in_specs=[pl.BlockSpec((tq, MLA_HW), lambda b, h, qi: (b * n_q + qi, h)),
                  pl.BlockSpec((rows.seq, MLA_HW), lambda b, h, qi: (b, h)), ctx_k,
                  pl.BlockSpec((MLA_V, rows.seq), lambda b, h, qi: (h, b)), ctx_vt],
        out_specs=pl.BlockSpec((tq, MLA_V), lambda b, h, qi: (b * n_q + qi, h)),
        out_shape=out_sd,
        compiler_params=_cparams(("arbitrary", "arbitrary", "arbitrary")),
        name="mla_attention",
    )(q, k, k, vt, vt)
    if not with_ctx:
        return out
    return pl.pallas_call(
        _attn_ctx_kernel,
        grid=(rows.batch, MLA_HEADS),
        in_specs=[ctx_k, ctx_k, ctx_vt, pl.BlockSpec(memory_space=pl.ANY)],
        out_specs=pl.BlockSpec((TQ, MLA_V), lambda b, h: (ctx0 + b, h)),
        out_shape=out_sd,
        input_output_aliases={3: 0},
        compiler_params=_cparams(("arbitrary", "arbitrary")),
        name="mla_attention_ctx",
    )(q, k, vt, out)


def _rope_swap_cols():
    idx = np.arange(MLA_ROPE).reshape(2, 2, MLA_ROPE // 4)
    return idx[:, ::-1, :].reshape(-1)


def _mla_weights(w_down, w_uq, w_ukv):
    swap = _rope_swap_cols()
    kr = w_down[:, Q_LORA + KV_LORA:]
    zpad = jnp.zeros((D_MODEL, LANES - MLA_ROPE), w_down.dtype)
    wd = jnp.concatenate([w_down[:, :Q_LORA + KV_LORA], kr, zpad, kr[:, swap], zpad], axis=1)
    uq = w_uq.reshape(Q_LORA, MLA_HEADS, MLA_QK)
    nope, rope = uq[..., :MLA_NOPE], uq[..., MLA_NOPE:]
    z64 = jnp.zeros((Q_LORA, MLA_HEADS, MLA_HW - MLA_QK), w_uq.dtype)
    wq = jnp.concatenate([nope, rope, z64], axis=-1).reshape(Q_LORA, MLA_HEADS * MLA_HW)
    wqs = jnp.concatenate([rope[..., swap], z64], axis=-1).reshape(Q_LORA, MLA_HEADS * LANES)
    ukv = w_ukv.reshape(KV_LORA, MLA_HEADS, MLA_NOPE + MLA_V)
    wk = ukv[..., :MLA_NOPE].reshape(KV_LORA, MLA_HEADS * MLA_NOPE)
    wvt = ukv[..., MLA_NOPE:].reshape(KV_LORA, MLA_HEADS * MLA_V).T
    return wd.astype(BF16), wq.astype(BF16), wqs.astype(BF16), wk.astype(BF16), wvt.astype(BF16)


def _rope_tables(seq):
    n_rows = seq // GRID_W
    row = jnp.repeat(jnp.arange(n_rows), GRID_W).astype(F32)
    col = jnp.tile(jnp.arange(GRID_W), n_rows).astype(F32)
    axis_dims = MLA_ROPE // 2
    inv = 1.0 / (ROPE_BASE ** (jnp.arange(0, axis_dims, 2, dtype=F32) / axis_dims))
    ang_r, ang_c = row[:, None] * inv, col[:, None] * inv
    cos64 = jnp.concatenate([jnp.cos(ang_r)] * 2 + [jnp.cos(ang_c)] * 2, axis=1)
    sin64 = jnp.concatenate([-jnp.sin(ang_r), jnp.sin(ang_r), -jnp.sin(ang_c), jnp.sin(ang_c)], axis=1)
    cos64 = jnp.concatenate([cos64, jnp.ones((TM, MLA_ROPE), F32)], axis=0)
    sin64 = jnp.concatenate([sin64, jnp.zeros((TM, MLA_ROPE), F32)], axis=0)
    n = seq + TM
    cos_t = jnp.concatenate([jnp.ones((n, MLA_NOPE), F32), cos64, jnp.zeros((n, MLA_HW - MLA_QK), F32)], axis=1)
    sin_t = jnp.concatenate([jnp.zeros((n, MLA_NOPE), F32), sin64, jnp.zeros((n, MLA_HW - MLA_QK), F32)], axis=1)
    return cos_t, sin_t


def kernel(x, c, ctx, c_ctx, mod_w, mod_b, norm_g, ffn_w_gate, ffn_w_up, ffn_w_down, hg_w_in, hg_w_out,
           hg_gn, hg_lb_logits, mla_w_down, mla_q_norm, mla_w_uq, mla_kv_norm, mla_w_ukv, mla_w_o, final_g):
    batch, seq, _ = x.shape
    rows = _Rows(batch, seq, ctx.shape[1])
    hs = (x.reshape(rows.n_lat, D_MODEL), ctx.reshape(rows.n_ctx, D_MODEL))

    mod_rows = -(-(batch + 1) // SUBLANES) * SUBLANES
    cond = jnp.concatenate([c, c_ctx[None], jnp.zeros((mod_rows - batch - 1, D_MODEL), F32)], axis=0)
    mod = _mod_call(cond, mod_w, mod_b)

    p_lb = jax.nn.softmax(hg_lb_logits.astype(F32), axis=0)
    lbs = jnp.maximum(jnp.cumsum(p_lb, axis=0) - p_lb[0:1], 0.0)
    tri = jnp.stack([jnp.tril(jnp.ones((HG_C, HG_C), F32)), jnp.triu(jnp.ones((HG_C, HG_C), F32))]).astype(BF16)
    head_id = np.arange(HG_HV) // HG_DV
    bd = jnp.asarray(head_id[:, None] == head_id[None, :], BF16)
    cos_t, sin_t = _rope_tables(seq)

    ffn_w = (ffn_w_gate.astype(BF16), ffn_w_up.astype(BF16), ffn_w_down.astype(BF16))

    for i in range(DEPTH):
        last = i == DEPTH - 1
        j = i // 2
        mod_l = mod[i].reshape(mod_rows, 1, N_MOD * D_MODEL)
        g = norm_g[i].reshape(3, 1, D_MODEL)
        hs = _ffn_call(rows, hs, mod_l, g[0], ffn_w, i, 0, 0, rows.all_tiles)
        n_out = rows.lat_tiles if last else rows.all_tiles
        if i % 2 == 0:
            w_in = hg_w_in[j].astype(BF16)
            w_vt = w_in[:, HG_I * D_MODEL:(HG_I + 1) * D_MODEL].T
            pb, pf, vt = _hgrn_in_call(rows, hs, mod_l, g[1], w_in, w_vt, lbs[j])
            o_f, o_b = _hgrn_scan_call(rows, pb, pf, vt, tri)
            gn = jnp.tile(hg_gn[j], HG_HEADS).reshape(1, HG_HV)
            mixer, mixer_args = "hgrn", (o_f, o_b, pf, gn, bd, hg_w_out[j].astype(BF16))
        else:
            w_dn, w_q, w_qs, w_k, w_vt = _mla_weights(mla_w_down[j], mla_w_uq[j], mla_w_ukv[j])
            q, k, vt = _mla_proj_call(rows, hs, mod_l, g[1], w_dn, mla_q_norm[j].reshape(1, Q_LORA),
                                      mla_kv_norm[j].reshape(1, KV_LORA), w_q, w_qs, w_k, w_vt, cos_t, sin_t)
            mixer, mixer_args = "mla", (_attn_call(rows, q, k, vt, with_ctx=not last), mla_w_o[j].astype(BF16))
        hs = _ffn_call(rows, hs, mod_l, g[2], ffn_w, i, 1, 6, n_out,
                       final_g=final_g.reshape(1, D_MODEL) if last else None, mixer=mixer, mixer_args=mixer_args)
    return hs.reshape(batch, seq, D_MODEL)
```

```python
import functools

import jax
import jax.numpy as jnp
import numpy as np
from jax import lax
from jax.experimental import pallas as pl
from jax.experimental.pallas import tpu as pltpu

F32 = jnp.float32
BF16 = jnp.bfloat16

D_MODEL = 1024
DEPTH = 4
GRID_W = 64
N_MOD = 9
FFN_HIDDEN = 2816
RMS_EPS = 1e-6
HG_HEADS = 8
HG_DK = 128
HG_DV = 128
HG_HK = HG_HEADS * HG_DK
HG_HV = HG_HEADS * HG_DV
HG_IN = 3 * HG_HK + HG_HV + D_MODEL
F_MIN = 1e-6
MLA_HEADS = 8
MLA_NOPE = 128
MLA_ROPE = 64
MLA_V = 128
Q_LORA = 384
KV_LORA = 256
MLA_QK = MLA_NOPE + MLA_ROPE
MLA_SCALE = MLA_QK ** -0.5
ROPE_BASE = 10000.0
LOG2_E = 1.4426950408889634

LANES = 128
SUBLANES = 8
MXU_DIM = 256
VMEM_LIMIT = 56 * 1024 * 1024

TM = 512
FFN_CHUNK = MXU_DIM
HG_C = 128
HG_SUB = 2
TQ = 256
Q_SUB = 16
KEY_CHUNK = 1024
MLA_HW = 2 * LANES
MLA_DOWN_P = Q_LORA + KV_LORA + 2 * LANES


def _cparams(sem):
    return pltpu.CompilerParams(dimension_semantics=sem, vmem_limit_bytes=VMEM_LIMIT)


def _resident(shape):
    nd = len(shape)
    return pl.BlockSpec(shape, lambda *_: (0,) * nd, pipeline_mode=pl.Buffered(1))


def _dot(a, b):
    return jnp.dot(a, b, preferred_element_type=F32)


def _dot_nt(a, b):
    return lax.dot_general(a, b, (((1,), (1,)), ((), ())), preferred_element_type=F32)


def _rms_modulate(h, g, shift, scale):
    ms = jnp.mean(h * h, axis=-1, keepdims=True)
    return (h * lax.rsqrt(ms + RMS_EPS) * g) * (1.0 + scale) + shift


def _mod_slices(mod_ref, base, n):
    m = mod_ref[0]
    return [m[:, (base + j) * D_MODEL:(base + j + 1) * D_MODEL] for j in range(n)]


def _mod_kernel(c_ref, w_ref, b_ref, o_ref):
    c = c_ref[...]
    s = c * jax.nn.sigmoid(c)
    s_hi = s.astype(BF16)
    s_lo = (s - s_hi.astype(F32)).astype(BF16)
    w = w_ref[0]
    w_hi = w.astype(BF16)
    w_lo = (w - w_hi.astype(F32)).astype(BF16)
    acc = _dot(s_hi, w_hi) + (_dot(s_lo, w_hi) + _dot(s_hi, w_lo))
    o_ref[0] = acc + b_ref[0]


def _mod_call(cond, mod_w, mod_b):
    rows = cond.shape[0]
    tn = D_MODEL
    n_out = N_MOD * D_MODEL
    return pl.pallas_call(
        _mod_kernel,
        grid=(DEPTH, n_out // tn),
        in_specs=[
            pl.BlockSpec((rows, D_MODEL), lambda l, n: (0, 0)),
            pl.BlockSpec((1, D_MODEL, tn), lambda l, n: (l, 0, n)),
            pl.BlockSpec((1, 1, tn), lambda l, n: (l, 0, n)),
        ],
        out_specs=pl.BlockSpec((1, rows, tn), lambda l, n: (l, 0, n)),
        out_shape=jax.ShapeDtypeStruct((DEPTH, rows, n_out), F32),
        compiler_params=_cparams(("arbitrary", "arbitrary")),
        name="mod_vectors",
    )(cond, mod_w, mod_b.reshape(DEPTH, 1, n_out))


class _Rows:
    def __init__(self, batch, seq, ctx_len):
        self.batch, self.seq, self.ctx_len = batch, seq, ctx_len
        self.n_lat = batch * seq
        self.n_ctx = batch * ctx_len
        self.n_tok = self.n_lat + self.n_ctx
        assert seq % TM == 0 and self.n_ctx % TM == 0
        self.lat_tiles = self.n_lat // TM
        self.tiles_per_seq = seq // TM
        self.all_tiles = self.n_tok // TM

    def mod_spec(self):
        lat_tiles, tps, ctx_row = self.lat_tiles, self.tiles_per_seq, self.batch
        return pl.BlockSpec(
            (1, 1, N_MOD * D_MODEL),
            lambda i: (jnp.where(i < lat_tiles, i // tps, ctx_row), 0, 0))


def _tile_spec(width, col=0):
    return pl.BlockSpec((TM, width), lambda i: (i, col))


def _hgrn_readout(of_ref, ob_ref, gate_ref, gn_ref, bd_ref, w_ref):
    o = of_ref[...] + ob_ref[...]
    ss = _dot((o * o).astype(BF16), bd_ref[...])
    on = o * lax.rsqrt(ss * (1.0 / HG_DV) + RMS_EPS) * gn_ref[...]
    gt = gate_ref[...]
    return _dot((on * (gt * jax.nn.sigmoid(gt))).astype(BF16), w_ref[...])


def _ffn_kernel(*refs, base, final, mixer, lat_tiles):
    rest = list(refs)
    if lat_tiles is None:
        o_ref = rest.pop()
        h = rest.pop(0)[...]
    else:
        h_scr = rest.pop()
        o_ref = rest.pop()
        x_ref, c_ref = rest.pop(0), rest.pop(0)

        @pl.when(pl.program_id(0) < lat_tiles)
        def _():
            h_scr[...] = x_ref[...]

        @pl.when(pl.program_id(0) >= lat_tiles)
        def _():
            h_scr[...] = c_ref[...]

        h = h_scr[...]
    mod_ref, g_ref = rest.pop(0), rest.pop(0)
    if mixer is not None:
        (res_gate,) = _mod_slices(mod_ref, 5, 1)
        if mixer == "hgrn":
            y = _hgrn_readout(*rest[:6])
            rest = rest[6:]
        else:
            y = _dot(rest[0][...], rest[1][...])
            rest = rest[2:]
        h = h + res_gate * y
    wg_ref, wu_ref, wd_ref = rest[:3]
    shift, scale, gate = _mod_slices(mod_ref, base, 3)
    a = _rms_modulate(h, g_ref[...], shift, scale).astype(BF16)
    acc = jnp.zeros((TM, D_MODEL), F32)
    for j in range(FFN_HIDDEN // FFN_CHUNK):
        sl = slice(j * FFN_CHUNK, (j + 1) * FFN_CHUNK)
        gg = _dot(a, wg_ref[:, sl])
        uu = _dot(a, wu_ref[:, sl])
        hm = (gg * jax.nn.sigmoid(gg) * uu).astype(BF16)
        acc = acc + _dot(hm, wd_ref[sl, :])
    out = h + (0.5 * gate) * acc
    if final:
        fg_ref = rest[3]
        ms = jnp.mean(out * out, axis=-1, keepdims=True)
        out = out * lax.rsqrt(ms + RMS_EPS) * fg_ref[...]
    o_ref[...] = out


def _ffn_call(rows, hs, mod_l, g, ffn_w, layer, half, base, n_tiles, final_g=None, mixer=None, mixer_args=()):
    final = final_g is not None
    split_in = isinstance(hs, tuple)
    lat_tiles = rows.lat_tiles

    def stacked(shape):
        return pl.BlockSpec((None, None) + shape, lambda i: (layer, half, 0, 0), pipeline_mode=pl.Buffered(1))

    if split_in:
        in_specs = [pl.BlockSpec((TM, D_MODEL), lambda i: (jnp.minimum(i, lat_tiles - 1), 0)),
                    pl.BlockSpec((TM, D_MODEL), lambda i: (jnp.maximum(i - lat_tiles, 0), 0))]
        args = list(hs)
    else:
        in_specs, args = [_tile_spec(D_MODEL)], [hs]
    in_specs += [rows.mod_spec(), _resident((1, D_MODEL))]
    if mixer == "hgrn":
        in_specs += [_tile_spec(HG_HV), _tile_spec(HG_HV), _tile_spec(D_MODEL, col=HG_GATE),
                     _resident((1, HG_HV)), _resident((HG_HV, HG_HV)), _resident((HG_HV, D_MODEL))]
    elif mixer == "mla":
        in_specs += [_tile_spec(MLA_HEADS * MLA_V), _resident((MLA_HEADS * MLA_V, D_MODEL))]
    in_specs += [stacked((D_MODEL, FFN_HIDDEN)), stacked((D_MODEL, FFN_HIDDEN)), stacked((FFN_HIDDEN, D_MODEL))]
    args += [mod_l, g, *mixer_args, *ffn_w]
    if final:
        in_specs.append(_resident((1, D_MODEL)))
        args.append(final_g)
    return pl.pallas_call(
        functools.partial(_ffn_kernel, base=base, final=final, mixer=mixer,
                          lat_tiles=lat_tiles if split_in else None),
        grid=(n_tiles,),
        in_specs=in_specs,
        out_specs=_tile_spec(D_MODEL),
        out_shape=jax.ShapeDtypeStruct((n_tiles * TM, D_MODEL), F32),
        scratch_shapes=[pltpu.VMEM((TM, D_MODEL), F32)] if split_in else [],
        compiler_params=_cparams(("arbitrary",)),
        name="ffn_half",
    )(*args)


HG_Q, HG_KF, HG_KB = range(3)
HG_LGF, HG_LGB, HG_GATE = range(3)
HG_ZF, HG_ZB, HG_I, HG_G = 1, 2, 3, 4


def _hgrn_in_kernel(h_ref, mod_ref, g_ref, w_ref, wvt_ref, lb_ref, pb_ref, pf_ref, vt_ref):
    shift, scale = _mod_slices(mod_ref, 3, 2)
    a = _rms_modulate(h_ref[...], g_ref[...], shift, scale).astype(BF16)

    for piece in range(D_MODEL // MXU_DIM):
        def proj(j):
            return _dot(a, w_ref[:, pl.ds(j * D_MODEL + piece * MXU_DIM, MXU_DIM)])

        def put(ref, blk, val):
            ref[:, pl.ds(blk * D_MODEL + piece * MXU_DIM, MXU_DIM)] = val.astype(ref.dtype)

        put(pb_ref, HG_Q, proj(0))
        vt_rows = pl.ds(piece * MXU_DIM, MXU_DIM)
        vt_ref[vt_rows, :] = _dot_nt(wvt_ref[vt_rows, :], a).astype(BF16)
        for d, (k_blk, lg_blk) in enumerate(((HG_KF, HG_LGF), (HG_KB, HG_LGB))):
            lb = lb_ref[d:d + 1, pl.ds(piece * MXU_DIM, MXU_DIM)]
            f = lb + (1.0 - lb) * jax.nn.sigmoid(proj(HG_ZF + d))
            put(pb_ref, k_blk, 1.0 - f)
            put(pf_ref, lg_blk, jnp.log2(jnp.maximum(f, F_MIN)))
        put(pf_ref, HG_GATE, proj(HG_G))


def _hgrn_in_call(rows, hs, mod_l, g, w_in, w_vt, lb):
    return pl.pallas_call(
        _hgrn_in_kernel,
        grid=(rows.all_tiles,),
        in_specs=[_tile_spec(D_MODEL), rows.mod_spec(), _resident((1, D_MODEL)),
                  _resident((D_MODEL, HG_IN)), _resident((HG_HV, D_MODEL)), _resident((2, HG_HK))],
        out_specs=[_tile_spec(3 * D_MODEL), _tile_spec(3 * D_MODEL), pl.BlockSpec((HG_HV, TM), lambda i: (0, i))],
        out_shape=[jax.ShapeDtypeStruct((rows.n_tok, 3 * D_MODEL), BF16),
                   jax.ShapeDtypeStruct((rows.n_tok, 3 * D_MODEL), F32),
                   jax.ShapeDtypeStruct((HG_HV, rows.n_tok), BF16)],
        compiler_params=_cparams(("arbitrary",)),
        name="hgrn_in_proj",
    )(hs, mod_l, g, w_in, w_vt, lb)


def _mla_proj_kernel(h_ref, mod_ref, g_ref, wd_ref, qn_ref, kvn_ref, wq_ref, wqs_ref, wk_ref, wvt_ref,
                     cos_ref, sin_ref, q_ref, k_ref, vt_ref):
    shift, scale = _mod_slices(mod_ref, 3, 2)
    a = _rms_modulate(h_ref[...], g_ref[...], shift, scale).astype(BF16)
    dp = _dot(a, wd_ref[...])
    cq = dp[:, :Q_LORA]
    ckv = dp[:, Q_LORA:Q_LORA + KV_LORA]
    kr_a = dp[:, Q_LORA + KV_LORA:Q_LORA + KV_LORA + LANES]
    kr_b = dp[:, Q_LORA + KV_LORA + LANES:]
    cqn = (cq * lax.rsqrt(jnp.mean(cq * cq, axis=-1, keepdims=True) + RMS_EPS) * qn_ref[...]).astype(BF16)
    ckvn = (ckv * lax.rsqrt(jnp.mean(ckv * ckv, axis=-1, keepdims=True) + RMS_EPS) * kvn_ref[...]).astype(BF16)
    cos = cos_ref[...]
    sin = sin_ref[...]
    cos_r, sin_r = cos[:, LANES:], sin[:, LANES:]
    q_scale = MLA_SCALE * LOG2_E
    qm = _dot(cqn, wq_ref[...])
    qs = _dot(cqn, wqs_ref[...])
    pieces = []
    for hh in range(MLA_HEADS):
        nope = qm[:, hh * MLA_HW:hh * MLA_HW + LANES] * q_scale
        rope = (qm[:, hh * MLA_HW + LANES:(hh + 1) * MLA_HW] * (cos_r * q_scale)
                + qs[:, hh * LANES:(hh + 1) * LANES] * (sin_r * q_scale))
        pieces += [nope.astype(BF16), rope.astype(BF16)]
    q_ref[...] = jnp.concatenate(pieces, axis=1)
    kr = (kr_a * cos_r + kr_b * sin_r).astype(BF16)
    kn = _dot(ckvn, wk_ref[...]).astype(BF16)
    pieces = []
    for hh in range(MLA_HEADS):
        pieces += [kn[:, hh * MLA_NOPE:(hh + 1) * MLA_NOPE], kr]
    k_ref[...] = jnp.concatenate(pieces, axis=1)
    vt_ref[...] = _dot_nt(wvt_ref[...], ckvn).astype(BF16)


def _mla_proj_call(rows, hs, mod_l, g, wd, qn, kvn, wq, wqs, wk, wvt, cos_t, sin_t):
    lat_tiles, tps = rows.lat_tiles, rows.tiles_per_seq
    rope_spec = pl.BlockSpec((TM, MLA_HW), lambda i: (jnp.where(i < lat_tiles, i % tps, tps), 0))
    qk_w = MLA_HEADS * MLA_HW
    v_w = MLA_HEADS * MLA_V
    return pl.pallas_call(
        _mla_proj_kernel,
        grid=(rows.all_tiles,),
        in_specs=[_tile_spec(D_MODEL), rows.mod_spec(), _resident((1, D_MODEL)),
                  _resident((D_MODEL, MLA_DOWN_P)), _resident((1, Q_LORA)), _resident((1, KV_LORA)),
                  _resident((Q_LORA, qk_w)), _resident((Q_LORA, MLA_HEADS * LANES)),
                  _resident((KV_LORA, MLA_HEADS * MLA_NOPE)), _resident((v_w, KV_LORA)),
                  rope_spec, rope_spec],
        out_specs=[_tile_spec(qk_w), _tile_spec(qk_w), pl.BlockSpec((v_w, TM), lambda i: (0, i))],
        out_shape=[jax.ShapeDtypeStruct((rows.n_tok, qk_w), BF16),
                   jax.ShapeDtypeStruct((rows.n_tok, qk_w), BF16),
                   jax.ShapeDtypeStruct((v_w, rows.n_tok), BF16)],
        compiler_params=_cparams(("arbitrary",)),
        name="mla_proj",
    )(hs, mod_l, g, wd, qn, kvn, wq, wqs, wk, wvt, cos_t, sin_t)


def _hgrn_cumsum(lg, tri):
    width = lg.shape[1]
    hi = lg.astype(BF16)
    lo = (lg - hi.astype(F32)).astype(BF16)
    c2 = _dot(tri, jnp.concatenate([hi, lo], axis=1))
    return c2[:, :width] + c2[:, width:]


def _block_diag(a, b):
    zero = jnp.zeros_like(a)
    return jnp.concatenate([jnp.concatenate([a, zero], axis=1), jnp.concatenate([zero, b], axis=1)], axis=0)


def _hgrn_pair(qb, kb, cum, vt, st_a_ref, st_b_ref, rev):
    c = HG_C
    pair_w = 2 * LANES
    row = lax.broadcasted_iota(jnp.int32, (c, pair_w), 0)
    col = lax.broadcasted_iota(jnp.int32, (c, pair_w), 1) & (LANES - 1)
    x = row ^ col
    causal = (row <= col) if rev else (row >= col)

    def halves(m):
        return m[:, :LANES], m[:, LANES:]

    def pair_scores(w_q, w_k):
        return _dot_nt(qb * w_q.astype(BF16), _block_diag(*halves(kb * w_k.astype(BF16))))

    edge = cum[0:1, :] if rev else cum[c - 1:c, :]
    st_a, st_b = st_a_ref[...], st_b_ref[...]
    o = _dot_nt(qb * jnp.exp2(cum).astype(BF16), _block_diag(st_a.astype(BF16), st_b.astype(BF16)))
    upd = _dot(vt, kb * jnp.exp2(edge - cum).astype(BF16))
    dec_a, dec_b = halves(jnp.exp2(edge))
    st_a_ref[...] = st_a * dec_a + upd[:LANES, :LANES]
    st_b_ref[...] = st_b * dec_b + upd[LANES:, LANES:]

    mid_row = SUBLANES // 2 if rev else SUBLANES // 2 - 1
    ref8 = jnp.concatenate(
        [jnp.broadcast_to(cum[b * SUBLANES + mid_row:b * SUBLANES + mid_row + 1, :], (SUBLANES, pair_w))
         for b in range(c // SUBLANES)], axis=0)
    d8 = cum - ref8
    a = pair_scores(jnp.exp2(d8), jnp.exp2(-d8))
    lvl = 3
    while (1 << lvl) < c:
        half = 1 << lvl
        parts = []
        for b in range(c // (2 * half)):
            lo = cum[2 * b * half:(2 * b + 1) * half, :]
            hi = cum[(2 * b + 1) * half:(2 * b + 2) * half, :]
            if rev:
                ref = hi[0:1, :]
                parts += [lo - ref, ref - hi]
            else:
                ref = lo[half - 1:half, :]
                parts += [ref - lo, hi - ref]
        w = jnp.exp2(jnp.concatenate(parts, axis=0))
        a = jnp.where(x < half, a, pair_scores(w, w))
        lvl += 1
    a = jnp.where(causal, a, 0.0)
    return o + _dot_nt(a.astype(BF16), _block_diag(vt[:LANES, :], vt[LANES:, :]))


def _hgrn_scan_kernel(qf_ref, kf_ref, vtf_ref, lgf_ref, qb_ref, kb_ref, vtb_ref, lgb_ref, tri_ref,
                      of_ref, ob_ref, st_ref):
    @pl.when(pl.program_id(1) == 0)
    def _():
        st_ref[...] = jnp.zeros_like(st_ref)

    for sub in range(HG_SUB):
        rf = slice(sub * HG_C, (sub + 1) * HG_C)
        rb = slice((HG_SUB - 1 - sub) * HG_C, (HG_SUB - sub) * HG_C)
        cum_f = _hgrn_cumsum(lgf_ref[rf, :], tri_ref[0])
        cum_b = _hgrn_cumsum(lgb_ref[rb, :], tri_ref[1])
        for p in range(HG_HEADS // 2):
            cols = slice(p * 2 * LANES, (p + 1) * 2 * LANES)
            of_ref[rf, cols] = _hgrn_pair(qf_ref[rf, cols], kf_ref[rf, cols], cum_f[:, cols], vtf_ref[cols, rf],
                                          st_ref.at[0, 2 * p], st_ref.at[0, 2 * p + 1], False)
            ob_ref[rb, cols] = _hgrn_pair(qb_ref[rb, cols], kb_ref[rb, cols], cum_b[:, cols], vtb_ref[cols, rb],
                                          st_ref.at[1, 2 * p], st_ref.at[1, 2 * p + 1], True)


def _hgrn_scan_call(rows, pb, pf, vt, tri):
    step_rows = HG_SUB * HG_C
    assert rows.ctx_len % step_rows == 0 and rows.seq % step_rows == 0
    c_chunks = rows.ctx_len // step_rows
    l_chunks = rows.seq // step_rows
    steps = c_chunks + l_chunks
    ctx0 = rows.n_lat // step_rows

    def fwd_blk(b, s):
        return jnp.where(s < c_chunks, ctx0 + b * c_chunks + s, b * l_chunks + s - c_chunks)

    def bwd_blk(b, s):
        return jnp.where(s < c_chunks, ctx0 + b * c_chunks + (c_chunks - 1 - s),
                         b * l_chunks + (l_chunks - 1 - (s - c_chunks)))

    def in_spec(blk, col):
        return pl.BlockSpec((step_rows, HG_HK), lambda b, s: (blk(b, s), col))

    def vt_spec(blk):
        return pl.BlockSpec((HG_HV, step_rows), lambda b, s: (0, blk(b, s)))

    out_sd = jax.ShapeDtypeStruct((rows.n_tok, HG_HV), F32)
    return pl.pallas_call(
        _hgrn_scan_kernel,
        grid=(rows.batch, steps),
        in_specs=[in_spec(fwd_blk, HG_Q), in_spec(fwd_blk, HG_KF), vt_spec(fwd_blk), in_spec(fwd_blk, HG_LGF),
                  in_spec(bwd_blk, HG_Q), in_spec(bwd_blk, HG_KB), vt_spec(bwd_blk), in_spec(bwd_blk, HG_LGB),
                  pl.BlockSpec((2, HG_C, HG_C), lambda b, s: (0, 0, 0))],
        out_specs=[pl.BlockSpec((step_rows, HG_HV), lambda b, s: (fwd_blk(b, s), 0)),
                   pl.BlockSpec((step_rows, HG_HV), lambda b, s: (bwd_blk(b, s), 0))],
        out_shape=[out_sd, out_sd],
        scratch_shapes=[pltpu.VMEM((2, HG_HEADS, HG_DV, HG_DK), F32)],
        compiler_params=_cparams(("arbitrary", "arbitrary")),
        name="hgrn_scan",
    )(pb, pb, vt, pf, pb, pb, vt, pf, tri)


def _attn_scores(q, kc_ref, kl_ref=None):
    scores = [_dot_nt(kc_ref[...], q)]
    if kl_ref is not None:
        scores += [_dot_nt(kl_ref[pl.ds(j * KEY_CHUNK, KEY_CHUNK), :], q)
                   for j in range(kl_ref.shape[0] // KEY_CHUNK)]
    m = jnp.max(scores[0], axis=0, keepdims=True)
    for s in scores[1:]:
        m = jnp.maximum(m, jnp.max(s, axis=0, keepdims=True))
    return scores, m


def _attn_values(scores, m, vtc_ref, vtl_ref=None):
    den, acc = None, None
    for j, s in enumerate(scores):
        p = jnp.exp2(s - m)
        vt = vtc_ref[...] if j == 0 else vtl_ref[:, pl.ds((j - 1) * KEY_CHUNK, KEY_CHUNK)]
        d_j = jnp.sum(p, axis=0, keepdims=True)
        a_j = _dot(vt, p.astype(BF16))
        den, acc = (d_j, a_j) if den is None else (den + d_j, acc + a_j)
    return (acc / den).T


def _attn_lat_kernel(q_ref, kl_ref, kc_ref, vtl_ref, vtc_ref, o_ref):
    nxt = _attn_scores(q_ref[pl.ds(0, TQ), :], kc_ref, kl_ref)
    for j in range(Q_SUB):
        cur = nxt
        if j + 1 < Q_SUB:
            nxt = _attn_scores(q_ref[pl.ds((j + 1) * TQ, TQ), :], kc_ref, kl_ref)
        o_ref[pl.ds(j * TQ, TQ), :] = _attn_values(*cur, vtc_ref, vtl_ref).astype(o_ref.dtype)


def _attn_ctx_kernel(q_ref, kc_ref, vtc_ref, lat_ref, o_ref):
    del lat_ref
    scores, m = _attn_scores(q_ref[...], kc_ref)
    o_ref[...] = _attn_values(scores, m, vtc_ref).astype(o_ref.dtype)


def _attn_call(rows, q, k, vt, with_ctx):
    tq = Q_SUB * TQ
    assert rows.ctx_len == TQ and rows.seq % tq == 0
    n_q = rows.seq // tq
    ctx0 = rows.n_lat // TQ
    out_sd = jax.ShapeDtypeStruct((rows.n_tok, MLA_HEADS * MLA_V), BF16)
    ctx_k = pl.BlockSpec((TQ, MLA_HW), lambda b, h, *_: (ctx0 + b, h))
    ctx_vt = pl.BlockSpec((MLA_V, TQ), lambda b, h, *_: (h, ctx0 + b))
    out = pl.pallas_call(
        _attn_lat_kernel,
        grid=(rows.batch, MLA_HEADS, n_q),
        in_specs=[pl.BlockSpec((tq, MLA_HW), lambda b, h, qi: (b * n_q + qi, h)),
                  pl.BlockSpec((rows.seq, MLA_HW), lambda b, h, qi: (b, h)), ctx_k,
                  pl.BlockSpec((MLA_V, rows.seq), lambda b, h, qi: (h, b)), ctx_vt],
        out_specs=pl.BlockSpec((tq, MLA_V), lambda b, h, qi: (b * n_q + qi, h)),
        out_shape=out_sd,
        compiler_params=_cparams(("arbitrary", "arbitrary", "arbitrary")),
        name="mla_attention",
    )(q, k, k, vt, vt)
    if not with_ctx:
        return out
    return pl.pallas_call(
        _attn_ctx_kernel,
        grid=(rows.batch, MLA_HEADS),
        in_specs=[ctx_k, ctx_k, ctx_vt, pl.BlockSpec(memory_space=pl.ANY)],
        out_specs=pl.BlockSpec((TQ, MLA_V), lambda b, h: (ctx0 + b, h)),
        out_shape=out_sd,
        input_output_aliases={3: 0},
        compiler_params=_cparams(("arbitrary", "arbitrary")),
        name="mla_attention_ctx",
    )(q, k, vt, out)


def _rope_swap_cols():
    idx = np.arange(MLA_ROPE).reshape(2, 2, MLA_ROPE // 4)
    return idx[:, ::-1, :].reshape(-1)


def _mla_weights(w_down, w_uq, w_ukv):
    swap = _rope_swap_cols()
    kr = w_down[:, Q_LORA + KV_LORA:]
    zpad = jnp.zeros((D_MODEL, LANES - MLA_ROPE), w_down.dtype)
    wd = jnp.concatenate([w_down[:, :Q_LORA + KV_LORA], kr, zpad, kr[:, swap], zpad], axis=1)
    uq = w_uq.reshape(Q_LORA, MLA_HEADS, MLA_QK)
    nope, rope = uq[..., :MLA_NOPE], uq[..., MLA_NOPE:]
    z64 = jnp.zeros((Q_LORA, MLA_HEADS, MLA_HW - MLA_QK), w_uq.dtype)
    wq = jnp.concatenate([nope, rope, z64], axis=-1).reshape(Q_LORA, MLA_HEADS * MLA_HW)
    wqs = jnp.concatenate([rope[..., swap], z64], axis=-1).reshape(Q_LORA, MLA_HEADS * LANES)
    ukv = w_ukv.reshape(KV_LORA, MLA_HEADS, MLA_NOPE + MLA_V)
    wk = ukv[..., :MLA_NOPE].reshape(KV_LORA, MLA_HEADS * MLA_NOPE)
    wvt = ukv[..., MLA_NOPE:].reshape(KV_LORA, MLA_HEADS * MLA_V).T
    return wd.astype(BF16), wq.astype(BF16), wqs.astype(BF16), wk.astype(BF16), wvt.astype(BF16)


def _rope_tables(seq):
    n_rows = seq // GRID_W
    row = jnp.repeat(jnp.arange(n_rows), GRID_W).astype(F32)
    col = jnp.tile(jnp.arange(GRID_W), n_rows).astype(F32)
    axis_dims = MLA_ROPE // 2
    inv = 1.0 / (ROPE_BASE ** (jnp.arange(0, axis_dims, 2, dtype=F32) / axis_dims))
    ang_r, ang_c = row[:, None] * inv, col[:, None] * inv
    cos64 = jnp.concatenate([jnp.cos(ang_r)] * 2 + [jnp.cos(ang_c)] * 2, axis=1)
    sin64 = jnp.concatenate([-jnp.sin(ang_r), jnp.sin(ang_r), -jnp.sin(ang_c), jnp.sin(ang_c)], axis=1)
    cos64 = jnp.concatenate([cos64, jnp.ones((TM, MLA_ROPE), F32)], axis=0)
    sin64 = jnp.concatenate([sin64, jnp.zeros((TM, MLA_ROPE), F32)], axis=0)
    n = seq + TM
    cos_t = jnp.concatenate([jnp.ones((n, MLA_NOPE), F32), cos64, jnp.zeros((n, MLA_HW - MLA_QK), F32)], axis=1)
    sin_t = jnp.concatenate([jnp.zeros((n, MLA_NOPE), F32), sin64, jnp.zeros((n, MLA_HW - MLA_QK), F32)], axis=1)
    return cos_t, sin_t


def kernel(x, c, ctx, c_ctx, mod_w, mod_b, norm_g, ffn_w_gate, ffn_w_up, ffn_w_down, hg_w_in, hg_w_out,
           hg_gn, hg_lb_logits, mla_w_down, mla_q_norm, mla_w_uq, mla_kv_norm, mla_w_ukv, mla_w_o, final_g):
    batch, seq, _ = x.shape
    rows = _Rows(batch, seq, ctx.shape[1])
    hs = (x.reshape(rows.n_lat, D_MODEL), ctx.reshape(rows.n_ctx, D_MODEL))

    mod_rows = -(-(batch + 1) // SUBLANES) * SUBLANES
    cond = jnp.concatenate([c, c_ctx[None], jnp.zeros((mod_rows - batch - 1, D_MODEL), F32)], axis=0)
    mod = _mod_call(cond, mod_w, mod_b)

    p_lb = jax.nn.softmax(hg_lb_logits.astype(F32), axis=0)
    lbs = jnp.maximum(jnp.cumsum(p_lb, axis=0) - p_lb[0:1], 0.0)
    tri = jnp.stack([jnp.tril(jnp.ones((HG_C, HG_C), F32)), jnp.triu(jnp.ones((HG_C, HG_C), F32))]).astype(BF16)
    head_id = np.arange(HG_HV) // HG_DV
    bd = jnp.asarray(head_id[:, None] == head_id[None, :], BF16)
    cos_t, sin_t = _rope_tables(seq)

    ffn_w = (ffn_w_gate.astype(BF16), ffn_w_up.astype(BF16), ffn_w_down.astype(BF16))

    for i in range(DEPTH):
        last = i == DEPTH - 1
        j = i // 2
        mod_l = mod[i].reshape(mod_rows, 1, N_MOD * D_MODEL)
        g = norm_g[i].reshape(3, 1, D_MODEL)
        hs = _ffn_call(rows, hs, mod_l, g[0], ffn_w, i, 0, 0, rows.all_tiles)
        n_out = rows.lat_tiles if last else rows.all_tiles
        if i % 2 == 0:
            w_in = hg_w_in[j].astype(BF16)
            w_vt = w_in[:, HG_I * D_MODEL:(HG_I + 1) * D_MODEL].T
            pb, pf, vt = _hgrn_in_call(rows, hs, mod_l, g[1], w_in, w_vt, lbs[j])
            o_f, o_b = _hgrn_scan_call(rows, pb, pf, vt, tri)
            gn = jnp.tile(hg_gn[j], HG_HEADS).reshape(1, HG_HV)
            mixer, mixer_args = "hgrn", (o_f, o_b, pf, gn, bd, hg_w_out[j].astype(BF16))
        else:
            w_dn, w_q, w_qs, w_k, w_vt = _mla_weights(mla_w_down[j], mla_w_uq[j], mla_w_ukv[j])
            q, k, vt = _mla_proj_call(rows, hs, mod_l, g[1], w_dn, mla_q_norm[j].reshape(1, Q_LORA),
                                      mla_kv_norm[j].reshape(1, KV_LORA), w_q, w_qs, w_k, w_vt, cos_t, sin_t)
            mixer, mixer_args = "mla", (_attn_call(rows, q, k, vt, with_ctx=not last), mla_w_o[j].astype(BF16))
        hs = _ffn_call(rows, hs, mod_l, g[2], ffn_w, i, 1, 6, n_out,
                       final_g=final_g.reshape(1, D_MODEL) if last else None, mixer=mixer, mixer_args=mixer_args)
    return hs.reshape(batch, seq, D_MODEL)
```

```python
import functools

import jax
import jax.numpy as jnp
import numpy as np
from jax import lax
from jax.experimental import pallas as pl
from jax.experimental.pallas import tpu as pltpu

F32 = jnp.float32
BF16 = jnp.bfloat16

D_MODEL = 1024
DEPTH = 4
GRID_W = 64
N_MOD = 9
FFN_HIDDEN = 2816
RMS_EPS = 1e-6
HG_HEADS = 8
HG_DK = 128
HG_DV = 128
HG_HK = HG_HEADS * HG_DK
HG_HV = HG_HEADS * HG_DV
HG_IN = 3 * HG_HK + HG_HV + D_MODEL
F_MIN = 1e-6
MLA_HEADS = 8
MLA_NOPE = 128
MLA_ROPE = 64
MLA_V = 128
Q_LORA = 384
KV_LORA = 256
MLA_QK = MLA_NOPE + MLA_ROPE
MLA_SCALE = MLA_QK ** -0.5
ROPE_BASE = 10000.0
LOG2_E = 1.4426950408889634

LANES = 128
SUBLANES = 8
MXU_DIM = 256
VMEM_LIMIT = 56 * 1024 * 1024

TM = 512
FFN_CHUNK = MXU_DIM
HG_C = 128
HG_SUB = 2
TQ = 256
Q_SUB = 16
KEY_CHUNK = 1024
MLA_HW = 2 * LANES
MLA_DOWN_P = Q_LORA + KV_LORA + 2 * LANES


def _cparams(sem):
    return pltpu.CompilerParams(dimension_semantics=sem, vmem_limit_bytes=VMEM_LIMIT)


def _resident(shape):
    nd = len(shape)
    return pl.BlockSpec(shape, lambda *_: (0,) * nd, pipeline_mode=pl.Buffered(1))


def _dot(a, b):
    return jnp.dot(a, b, preferred_element_type=F32)


def _dot_nt(a, b):
    return lax.dot_general(a, b, (((1,), (1,)), ((), ())), preferred_element_type=F32)


def _rms_modulate(h, g, shift, scale):
    r = lax.rsqrt(jnp.mean(h * h, axis=-1, keepdims=True) + RMS_EPS)
    gs = g * (1.0 + scale)
    pieces = []
    for c0 in range(0, h.shape[1], MXU_DIM):
        cols = slice(c0, c0 + MXU_DIM)
        pieces.append(((h[:, cols] * r) * gs[:, cols] + shift[:, cols]).astype(BF16))
    return jnp.concatenate(pieces, axis=1)


def _mod_slices(mod_ref, base, n):
    m = mod_ref[0]
    return [m[:, (base + j) * D_MODEL:(base + j + 1) * D_MODEL] for j in range(n)]


def _mod_kernel(c_ref, w_ref, b_ref, o_ref):
    c = c_ref[...]
    s = c * jax.nn.sigmoid(c)
    s_hi = s.astype(BF16)
    s_lo = (s - s_hi.astype(F32)).astype(BF16)
    w = w_ref[0]
    w_hi = w.astype(BF16)
    w_lo = (w - w_hi.astype(F32)).astype(BF16)
    acc = _dot(s_hi, w_hi) + (_dot(s_lo, w_hi) + _dot(s_hi, w_lo))
    o_ref[0] = acc + b_ref[0]


def _mod_call(cond, mod_w, mod_b):
    rows = cond.shape[0]
    tn = D_MODEL
    n_out = N_MOD * D_MODEL
    return pl.pallas_call(
        _mod_kernel,
        grid=(DEPTH, n_out // tn),
        in_specs=[
            pl.BlockSpec((rows, D_MODEL), lambda l, n: (0, 0)),
            pl.BlockSpec((1, D_MODEL, tn), lambda l, n: (l, 0, n)),
            pl.BlockSpec((1, 1, tn), lambda l, n: (l, 0, n)),
        ],
        out_specs=pl.BlockSpec((1, rows, tn), lambda l, n: (l, 0, n)),
        out_shape=jax.ShapeDtypeStruct((DEPTH, rows, n_out), F32),
        compiler_params=_cparams(("arbitrary", "arbitrary")),
        name="mod_vectors",
    )(cond, mod_w, mod_b.reshape(DEPTH, 1, n_out))


class _Rows:
    def __init__(self, batch, seq, ctx_len):
        self.batch, self.seq, self.ctx_len = batch, seq, ctx_len
        self.n_lat = batch * seq
        self.n_ctx = batch * ctx_len
        self.n_tok = self.n_lat + self.n_ctx
        assert seq % TM == 0 and self.n_ctx % TM == 0
        self.lat_tiles = self.n_lat // TM
        self.tiles_per_seq = seq // TM
        self.all_tiles = self.n_tok // TM

    def mod_spec(self):
        lat_tiles, tps, ctx_row = self.lat_tiles, self.tiles_per_seq, self.batch
        return pl.BlockSpec(
            (1, 1, N_MOD * D_MODEL),
            lambda i: (jnp.where(i < lat_tiles, i // tps, ctx_row), 0, 0))


def _tile_spec(width, col=0):
    return pl.BlockSpec((TM, width), lambda i: (i, col))


def _hgrn_readout(of_ref, ob_ref, gate_ref, gn_ref, bd_ref, w_ref):
    o = of_ref[...] + ob_ref[...]
    ss = _dot((o * o).astype(BF16), bd_ref[...])
    on = o * lax.rsqrt(ss * (1.0 / HG_DV) + RMS_EPS) * gn_ref[...]
    gt = gate_ref[...]
    return _dot((on * (gt * jax.nn.sigmoid(gt))).astype(BF16), w_ref[...])


def _ffn_kernel(*refs, base, final, mixer, lat_tiles):
    rest = list(refs)
    if lat_tiles is None:
        o_ref = rest.pop()
        h = rest.pop(0)[...]
    else:
        h_scr = rest.pop()
        o_ref = rest.pop()
        x_ref, c_ref = rest.pop(0), rest.pop(0)

        @pl.when(pl.program_id(0) < lat_tiles)
        def _():
            h_scr[...] = x_ref[...]

        @pl.when(pl.program_id(0) >= lat_tiles)
        def _():
            h_scr[...] = c_ref[...]

        h = h_scr[...]
    mod_ref, g_ref = rest.pop(0), rest.pop(0)
    if mixer is not None:
        (res_gate,) = _mod_slices(mod_ref, 5, 1)
        if mixer == "hgrn":
            y = _hgrn_readout(*rest[:6])
            rest = rest[6:]
        else:
            y = _dot(rest[0][...], rest[1][...])
            rest = rest[2:]
        h = h + res_gate * y
    wg_ref, wu_ref, wd_ref = rest[:3]
    shift, scale, gate = _mod_slices(mod_ref, base, 3)
    a = _rms_modulate(h, g_ref[...], shift, scale)
    acc = jnp.zeros((TM, D_MODEL), F32)
    for j in range(FFN_HIDDEN // FFN_CHUNK):
        sl = slice(j * FFN_CHUNK, (j + 1) * FFN_CHUNK)
        gg = _dot(a, wg_ref[:, sl])
        uu = _dot(a, wu_ref[:, sl])
        hm = (gg * jax.nn.sigmoid(gg) * uu).astype(BF16)
        acc = acc + _dot(hm, wd_ref[sl, :])
    out = h + (0.5 * gate) * acc
    if final:
        fg_ref = rest[3]
        ms = jnp.mean(out * out, axis=-1, keepdims=True)
        out = out * lax.rsqrt(ms + RMS_EPS) * fg_ref[...]
    o_ref[...] = out


def _ffn_call(rows, hs, mod_l, g, ffn_w, layer, half, base, n_tiles, final_g=None, mixer=None, mixer_args=()):
    final = final_g is not None
    split_in = isinstance(hs, tuple)
    lat_tiles = rows.lat_tiles

    def stacked(shape):
        return pl.BlockSpec((None, None) + shape, lambda i: (layer, half, 0, 0), pipeline_mode=pl.Buffered(1))

    if split_in:
        in_specs = [pl.BlockSpec((TM, D_MODEL), lambda i: (jnp.minimum(i, lat_tiles - 1), 0)),
                    pl.BlockSpec((TM, D_MODEL), lambda i: (jnp.maximum(i - lat_tiles, 0), 0))]
        args = list(hs)
    else:
        in_specs, args = [_tile_spec(D_MODEL)], [hs]
    in_specs += [rows.mod_spec(), _resident((1, D_MODEL))]
    if mixer == "hgrn":
        in_specs += [_tile_spec(HG_HV), _tile_spec(HG_HV), _tile_spec(D_MODEL, col=HG_GATE),
                     _resident((1, HG_HV)), _resident((HG_HV, HG_HV)), _resident((HG_HV, D_MODEL))]
    elif mixer == "mla":
        in_specs += [_tile_spec(MLA_HEADS * MLA_V), _resident((MLA_HEADS * MLA_V, D_MODEL))]
    in_specs += [stacked((D_MODEL, FFN_HIDDEN)), stacked((D_MODEL, FFN_HIDDEN)), stacked((FFN_HIDDEN, D_MODEL))]
    args += [mod_l, g, *mixer_args, *ffn_w]
    if final:
        in_specs.append(_resident((1, D_MODEL)))
        args.append(final_g)
    return pl.pallas_call(
        functools.partial(_ffn_kernel, base=base, final=final, mixer=mixer,
                          lat_tiles=lat_tiles if split_in else None),
        grid=(n_tiles,),
        in_specs=in_specs,
        out_specs=_tile_spec(D_MODEL),
        out_shape=jax.ShapeDtypeStruct((n_tiles * TM, D_MODEL), F32),
        scratch_shapes=[pltpu.VMEM((TM, D_MODEL), F32)] if split_in else [],
        compiler_params=_cparams(("arbitrary",)),
        name="ffn_half",
    )(*args)


HG_Q, HG_KF, HG_KB = range(3)
HG_LGF, HG_LGB, HG_GATE = range(3)
HG_ZF, HG_ZB, HG_I, HG_G = 1, 2, 3, 4


def _hgrn_in_kernel(h_ref, mod_ref, g_ref, w_ref, wvt_ref, lb_ref, pb_ref, pf_ref, vt_ref):
    shift, scale = _mod_slices(mod_ref, 3, 2)
    a = _rms_modulate(h_ref[...], g_ref[...], shift, scale)

    for piece in range(D_MODEL // MXU_DIM):
        def proj(j):
            return _dot(a, w_ref[:, pl.ds(j * D_MODEL + piece * MXU_DIM, MXU_DIM)])

        def put(ref, blk, val):
            ref[:, pl.ds(blk * D_MODEL + piece * MXU_DIM, MXU_DIM)] = val.astype(ref.dtype)

        put(pb_ref, HG_Q, proj(0))
        vt_rows = pl.ds(piece * MXU_DIM, MXU_DIM)
        vt_ref[vt_rows, :] = _dot_nt(wvt_ref[vt_rows, :], a).astype(BF16)
        for d, (k_blk, lg_blk) in enumerate(((HG_KF, HG_LGF), (HG_KB, HG_LGB))):
            lb = lb_ref[d:d + 1, pl.ds(piece * MXU_DIM, MXU_DIM)]
            f = lb + (1.0 - lb) * jax.nn.sigmoid(proj(HG_ZF + d))
            put(pb_ref, k_blk, 1.0 - f)
            put(pf_ref, lg_blk, jnp.log2(jnp.maximum(f, F_MIN)))
        put(pf_ref, HG_GATE, proj(HG_G))


def _hgrn_in_call(rows, hs, mod_l, g, w_in, w_vt, lb):
    return pl.pallas_call(
        _hgrn_in_kernel,
        grid=(rows.all_tiles,),
        in_specs=[_tile_spec(D_MODEL), rows.mod_spec(), _resident((1, D_MODEL)),
                  _resident((D_MODEL, HG_IN)), _resident((HG_HV, D_MODEL)), _resident((2, HG_HK))],
        out_specs=[_tile_spec(3 * D_MODEL), _tile_spec(3 * D_MODEL), pl.BlockSpec((HG_HV, TM), lambda i: (0, i))],
        out_shape=[jax.ShapeDtypeStruct((rows.n_tok, 3 * D_MODEL), BF16),
                   jax.ShapeDtypeStruct((rows.n_tok, 3 * D_MODEL), F32),
                   jax.ShapeDtypeStruct((HG_HV, rows.n_tok), BF16)],
        compiler_params=_cparams(("arbitrary",)),
        name="hgrn_in_proj",
    )(hs, mod_l, g, w_in, w_vt, lb)


def _mla_proj_kernel(h_ref, mod_ref, g_ref, wd_ref, qn_ref, kvn_ref, wq_ref, wqs_ref, wk_ref, wvt_ref,
                     cos_ref, sin_ref, q_ref, k_ref, vt_ref):
    shift, scale = _mod_slices(mod_ref, 3, 2)
    a = _rms_modulate(h_ref[...], g_ref[...], shift, scale)
    dp = _dot(a, wd_ref[...])
    cq = dp[:, :Q_LORA]
    ckv = dp[:, Q_LORA:Q_LORA + KV_LORA]
    kr_a = dp[:, Q_LORA + KV_LORA:Q_LORA + KV_LORA + LANES]
    kr_b = dp[:, Q_LORA + KV_LORA + LANES:]
    cqn = (cq * lax.rsqrt(jnp.mean(cq * cq, axis=-1, keepdims=True) + RMS_EPS) * qn_ref[...]).astype(BF16)
    ckvn = (ckv * lax.rsqrt(jnp.mean(ckv * ckv, axis=-1, keepdims=True) + RMS_EPS) * kvn_ref[...]).astype(BF16)
    cos = cos_ref[...]
    sin = sin_ref[...]
    cos_r, sin_r = cos[:, LANES:], sin[:, LANES:]
    q_scale = MLA_SCALE * LOG2_E
    qm = _dot(cqn, wq_ref[...])
    qs = _dot(cqn, wqs_ref[...])
    pieces = []
    for hh in range(MLA_HEADS):
        nope = qm[:, hh * MLA_HW:hh * MLA_HW + LANES] * q_scale
        rope = (qm[:, hh * MLA_HW + LANES:(hh + 1) * MLA_HW] * (cos_r * q_scale)
                + qs[:, hh * LANES:(hh + 1) * LANES] * (sin_r * q_scale))
        pieces += [nope.astype(BF16), rope.astype(BF16)]
    q_ref[...] = jnp.concatenate(pieces, axis=1)
    kr = (kr_a * cos_r + kr_b * sin_r).astype(BF16)
    kn = _dot(ckvn, wk_ref[...]).astype(BF16)
    pieces = []
    for hh in range(MLA_HEADS):
        pieces += [kn[:, hh * MLA_NOPE:(hh + 1) * MLA_NOPE], kr]
    k_ref[...] = jnp.concatenate(pieces, axis=1)
    vt_ref[...] = _dot_nt(wvt_ref[...], ckvn).astype(BF16)


def _mla_proj_call(rows, hs, mod_l, g, wd, qn, kvn, wq, wqs, wk, wvt, cos_t, sin_t):
    lat_tiles, tps = rows.lat_tiles, rows.tiles_per_seq
    rope_spec = pl.BlockSpec((TM, MLA_HW), lambda i: (jnp.where(i < lat_tiles, i % tps, tps), 0))
    qk_w = MLA_HEADS * MLA_HW
    v_w = MLA_HEADS * MLA_V
    return pl.pallas_call(
        _mla_proj_kernel,
        grid=(rows.all_tiles,),
        in_specs=[_tile_spec(D_MODEL), rows.mod_spec(), _resident((1, D_MODEL)),
                  _resident((D_MODEL, MLA_DOWN_P)), _resident((1, Q_LORA)), _resident((1, KV_LORA)),
                  _resident((Q_LORA, qk_w)), _resident((Q_LORA, MLA_HEADS * LANES)),
                  _resident((KV_LORA, MLA_HEADS * MLA_NOPE)), _resident((v_w, KV_LORA)),
                  rope_spec, rope_spec],
        out_specs=[_tile_spec(qk_w), _tile_spec(qk_w), pl.BlockSpec((v_w, TM), lambda i: (0, i))],
        out_shape=[jax.ShapeDtypeStruct((rows.n_tok, qk_w), BF16),
                   jax.ShapeDtypeStruct((rows.n_tok, qk_w), BF16),
                   jax.ShapeDtypeStruct((v_w, rows.n_tok), BF16)],
        compiler_params=_cparams(("arbitrary",)),
        name="mla_proj",
    )(hs, mod_l, g, wd, qn, kvn, wq, wqs, wk, wvt, cos_t, sin_t)


def _hgrn_cumsum(lg, tri):
    width = lg.shape[1]
    hi = lg.astype(BF16)
    lo = (lg - hi.astype(F32)).astype(BF16)
    c2 = _dot(tri, jnp.concatenate([hi, lo], axis=1))
    return c2[:, :width] + c2[:, width:]


def _block_diag(a, b):
    zero = jnp.zeros_like(a)
    return jnp.concatenate([jnp.concatenate([a, zero], axis=1), jnp.concatenate([zero, b], axis=1)], axis=0)


def _hgrn_pair(qb, kb, cum, vt, st_a_ref, st_b_ref, rev):
    c = HG_C
    pair_w = 2 * LANES
    row = lax.broadcasted_iota(jnp.int32, (c, pair_w), 0)
    col = lax.broadcasted_iota(jnp.int32, (c, pair_w), 1) & (LANES - 1)
    x = row ^ col
    causal = (row <= col) if rev else (row >= col)

    def halves(m):
        return m[:, :LANES], m[:, LANES:]

    def pair_scores(w_q, w_k):
        return _dot_nt(qb * w_q.astype(BF16), _block_diag(*halves(kb * w_k.astype(BF16))))

    edge = cum[0:1, :] if rev else cum[c - 1:c, :]
    st_a, st_b = st_a_ref[...], st_b_ref[...]
    o = _dot_nt(qb * jnp.exp2(cum).astype(BF16), _block_diag(st_a.astype(BF16), st_b.astype(BF16)))
    upd = _dot(vt, kb * jnp.exp2(edge - cum).astype(BF16))
    dec_a, dec_b = halves(jnp.exp2(edge))
    st_a_ref[...] = st_a * dec_a + upd[:LANES, :LANES]
    st_b_ref[...] = st_b * dec_b + upd[LANES:, LANES:]

    mid_row = SUBLANES // 2 if rev else SUBLANES // 2 - 1
    ref8 = jnp.concatenate(
        [jnp.broadcast_to(cum[b * SUBLANES + mid_row:b * SUBLANES + mid_row + 1, :], (SUBLANES, pair_w))
         for b in range(c // SUBLANES)], axis=0)
    d8 = cum - ref8
    a = pair_scores(jnp.exp2(d8), jnp.exp2(-d8))
    lvl = 3
    while (1 << lvl) < c:
        half = 1 << lvl
        parts = []
        for b in range(c // (2 * half)):
            lo = cum[2 * b * half:(2 * b + 1) * half, :]
            hi = cum[(2 * b + 1) * half:(2 * b + 2) * half, :]
            if rev:
                ref = hi[0:1, :]
                parts += [lo - ref, ref - hi]
            else:
                ref = lo[half - 1:half, :]
                parts += [ref - lo, hi - ref]
        w = jnp.exp2(jnp.concatenate(parts, axis=0))
        a = jnp.where(x < half, a, pair_scores(w, w))
        lvl += 1
    a = jnp.where(causal, a, 0.0)
    return o + _dot_nt(a.astype(BF16), _block_diag(vt[:LANES, :], vt[LANES:, :]))


def _hgrn_scan_kernel(qf_ref, kf_ref, vtf_ref, lgf_ref, qb_ref, kb_ref, vtb_ref, lgb_ref, tri_ref,
                      of_ref, ob_ref, st_ref):
    @pl.when(pl.program_id(1) == 0)
    def _():
        st_ref[...] = jnp.zeros_like(st_ref)

    for sub in range(HG_SUB):
        rf = slice(sub * HG_C, (sub + 1) * HG_C)
        rb = slice((HG_SUB - 1 - sub) * HG_C, (HG_SUB - sub) * HG_C)
        cum_f = _hgrn_cumsum(lgf_ref[rf, :], tri_ref[0])
        cum_b = _hgrn_cumsum(lgb_ref[rb, :], tri_ref[1])
        for p in range(HG_HEADS // 2):
            cols = slice(p * 2 * LANES, (p + 1) * 2 * LANES)
            of_ref[rf, cols] = _hgrn_pair(qf_ref[rf, cols], kf_ref[rf, cols], cum_f[:, cols], vtf_ref[cols, rf],
                                          st_ref.at[0, 2 * p], st_ref.at[0, 2 * p + 1], False)
            ob_ref[rb, cols] = _hgrn_pair(qb_ref[rb, cols], kb_ref[rb, cols], cum_b[:, cols], vtb_ref[cols, rb],
                                          st_ref.at[1, 2 * p], st_ref.at[1, 2 * p + 1], True)


def _hgrn_scan_call(rows, pb, pf, vt, tri):
    step_rows = HG_SUB * HG_C
    assert rows.ctx_len % step_rows == 0 and rows.seq % step_rows == 0
    c_chunks = rows.ctx_len // step_rows
    l_chunks = rows.seq // step_rows
    steps = c_chunks + l_chunks
    ctx0 = rows.n_lat // step_rows

    def fwd_blk(b, s):
        return jnp.where(s < c_chunks, ctx0 + b * c_chunks + s, b * l_chunks + s - c_chunks)

    def bwd_blk(b, s):
        return jnp.where(s < c_chunks, ctx0 + b * c_chunks + (c_chunks - 1 - s),
                         b * l_chunks + (l_chunks - 1 - (s - c_chunks)))

    def in_spec(blk, col):
        return pl.BlockSpec((step_rows, HG_HK), lambda b, s: (blk(b, s), col))

    def vt_spec(blk):
        return pl.BlockSpec((HG_HV, step_rows), lambda b, s: (0, blk(b, s)))

    out_sd = jax.ShapeDtypeStruct((rows.n_tok, HG_HV), F32)
    return pl.pallas_call(
        _hgrn_scan_kernel,
        grid=(rows.batch, steps),
        in_specs=[in_spec(fwd_blk, HG_Q), in_spec(fwd_blk, HG_KF), vt_spec(fwd_blk), in_spec(fwd_blk, HG_LGF),
                  in_spec(bwd_blk, HG_Q), in_spec(bwd_blk, HG_KB), vt_spec(bwd_blk), in_spec(bwd_blk, HG_LGB),
                  pl.BlockSpec((2, HG_C, HG_C), lambda b, s: (0, 0, 0))],
        out_specs=[pl.BlockSpec((step_rows, HG_HV), lambda b, s: (fwd_blk(b, s), 0)),
                   pl.BlockSpec((step_rows, HG_HV), lambda b, s: (bwd_blk(b, s), 0))],
        out_shape=[out_sd, out_sd],
        scratch_shapes=[pltpu.VMEM((2, HG_HEADS, HG_DV, HG_DK), F32)],
        compiler_params=_cparams(("arbitrary", "arbitrary")),
        name="hgrn_scan",
    )(pb, pb, vt, pf, pb, pb, vt, pf, tri)


def _attn_scores(q, kc_ref, kl_ref=None):
    scores = [_dot_nt(kc_ref[...], q)]
    if kl_ref is not None:
        scores += [_dot_nt(kl_ref[pl.ds(j * KEY_CHUNK, KEY_CHUNK), :], q)
                   for j in range(kl_ref.shape[0] // KEY_CHUNK)]
    m = jnp.max(scores[0], axis=0, keepdims=True)
    for s in scores[1:]:
        m = jnp.maximum(m, jnp.max(s, axis=0, keepdims=True))
    return scores, m


def _attn_values(scores, m, vtc_ref, vtl_ref=None):
    den, acc = None, None
    for j, s in enumerate(scores):
        p = jnp.exp2(s - m)
        vt = vtc_ref[...] if j == 0 else vtl_ref[:, pl.ds((j - 1) * KEY_CHUNK, KEY_CHUNK)]
        d_j = jnp.sum(p, axis=0, keepdims=True)
        a_j = _dot(vt, p.astype(BF16))
        den, acc = (d_j, a_j) if den is None else (den + d_j, acc + a_j)
    return (acc / den).T


def _attn_lat_kernel(q_ref, kl_ref, kc_ref, vtl_ref, vtc_ref, o_ref):
    nxt = _attn_scores(q_ref[pl.ds(0, TQ), :], kc_ref, kl_ref)
    for j in range(Q_SUB):
        cur = nxt
        if j + 1 < Q_SUB:
            nxt = _attn_scores(q_ref[pl.ds((j + 1) * TQ, TQ), :], kc_ref, kl_ref)
        o_ref[pl.ds(j * TQ, TQ), :] = _attn_values(*cur, vtc_ref, vtl_ref).astype(o_ref.dtype)


def _attn_ctx_kernel(q_ref, kc_ref, vtc_ref, lat_ref, o_ref):
    del lat_ref
    scores, m = _attn_scores(q_ref[...], kc_ref)
    o_ref[...] = _attn_values(scores, m, vtc_ref).astype(o_ref.dtype)


def _attn_call(rows, q, k, vt, with_ctx):
    tq = Q_SUB * TQ
    assert rows.ctx_len == TQ and rows.seq % tq == 0
    n_q = rows.seq // tq
    ctx0 = rows.n_lat // TQ
    out_sd = jax.ShapeDtypeStruct((rows.n_tok, MLA_HEADS * MLA_V), BF16)
    ctx_k = pl.BlockSpec((TQ, MLA_HW), lambda b, h, *_: (ctx0 + b, h))
    ctx_vt = pl.BlockSpec((MLA_V, TQ), lambda b, h, *_: (h, ctx0 + b))
    out = pl.pallas_call(
        _attn_lat_kernel,
        grid=(rows.batch, MLA_HEADS, n_q),
        in_specs=[pl.BlockSpec((tq, MLA_HW), lambda b, h, qi: (b * n_q + qi, h)),
                  pl.BlockSpec((rows.seq, MLA_HW), lambda b, h, qi: (b, h)), ctx_k,
                  pl.BlockSpec((MLA_V, rows.seq), lambda b, h, qi: (h, b)), ctx_vt],
        out_specs=pl.BlockSpec((tq, MLA_V), lambda b, h, qi: (b * n_q + qi, h)),
        out_shape=out_sd,
        compiler_params=_cparams(("arbitrary", "arbitrary", "arbitrary")),
        name="mla_attention",
    )(q, k, k, vt, vt)
    if not with_ctx:
        return out
    return pl.pallas_call(
        _attn_ctx_kernel,
        grid=(rows.batch, MLA_HEADS),
        in_specs=[ctx_k, ctx_k, ctx_vt, pl.BlockSpec(memory_space=pl.ANY)],
        out_specs=pl.BlockSpec((TQ, MLA_V), lambda b, h: (ctx0 + b, h)),
        out_shape=out_sd,
        input_output_aliases={3: 0},
        compiler_params=_cparams(("arbitrary", "arbitrary")),
        name="mla_attention_ctx",
    )(q, k, vt, out)


def _rope_swap_cols():
    idx = np.arange(MLA_ROPE).reshape(2, 2, MLA_ROPE // 4)
    return idx[:, ::-1, :].reshape(-1)


def _mla_weights(w_down, w_uq, w_ukv):
    swap = _rope_swap_cols()
    kr = w_down[:, Q_LORA + KV_LORA:]
    zpad = jnp.zeros((D_MODEL, LANES - MLA_ROPE), w_down.dtype)
    wd = jnp.concatenate([w_down[:, :Q_LORA + KV_LORA], kr, zpad, kr[:, swap], zpad], axis=1)
    uq = w_uq.reshape(Q_LORA, MLA_HEADS, MLA_QK)
    nope, rope = uq[..., :MLA_NOPE], uq[..., MLA_NOPE:]
    z64 = jnp.zeros((Q_LORA, MLA_HEADS, MLA_HW - MLA_QK), w_uq.dtype)
    wq = jnp.concatenate([nope, rope, z64], axis=-1).reshape(Q_LORA, MLA_HEADS * MLA_HW)
    wqs = jnp.concatenate([rope[..., swap], z64], axis=-1).reshape(Q_LORA, MLA_HEADS * LANES)
    ukv = w_ukv.reshape(KV_LORA, MLA_HEADS, MLA_NOPE + MLA_V)
    wk = ukv[..., :MLA_NOPE].reshape(KV_LORA, MLA_HEADS * MLA_NOPE)
    wvt = ukv[..., MLA_NOPE:].reshape(KV_LORA, MLA_HEADS * MLA_V).T
    return wd.astype(BF16), wq.astype(BF16), wqs.astype(BF16), wk.astype(BF16), wvt.astype(BF16)


def _rope_tables(seq):
    n_rows = seq // GRID_W
    row = jnp.repeat(jnp.arange(n_rows), GRID_W).astype(F32)
    col = jnp.tile(jnp.arange(GRID_W), n_rows).astype(F32)
    axis_dims = MLA_ROPE // 2
    inv = 1.0 / (ROPE_BASE ** (jnp.arange(0, axis_dims, 2, dtype=F32) / axis_dims))
    ang_r, ang_c = row[:, None] * inv, col[:, None] * inv
    cos64 = jnp.concatenate([jnp.cos(ang_r)] * 2 + [jnp.cos(ang_c)] * 2, axis=1)
    sin64 = jnp.concatenate([-jnp.sin(ang_r), jnp.sin(ang_r), -jnp.sin(ang_c), jnp.sin(ang_c)], axis=1)
    cos64 = jnp.concatenate([cos64, jnp.ones((TM, MLA_ROPE), F32)], axis=0)
    sin64 = jnp.concatenate([sin64, jnp.zeros((TM, MLA_ROPE), F32)], axis=0)
    n = seq + TM
    cos_t = jnp.concatenate([jnp.ones((n, MLA_NOPE), F32), cos64, jnp.zeros((n, MLA_HW - MLA_QK), F32)], axis=1)
    sin_t = jnp.concatenate([jnp.zeros((n, MLA_NOPE), F32), sin64, jnp.zeros((n, MLA_HW - MLA_QK), F32)], axis=1)
    return cos_t, sin_t


def kernel(x, c, ctx, c_ctx, mod_w, mod_b, norm_g, ffn_w_gate, ffn_w_up, ffn_w_down, hg_w_in, hg_w_out,
           hg_gn, hg_lb_logits, mla_w_down, mla_q_norm, mla_w_uq, mla_kv_norm, mla_w_ukv, mla_w_o, final_g):
    batch, seq, _ = x.shape
    rows = _Rows(batch, seq, ctx.shape[1])
    hs = (x.reshape(rows.n_lat, D_MODEL), ctx.reshape(rows.n_ctx, D_MODEL))

    mod_rows = -(-(batch + 1) // SUBLANES) * SUBLANES
    cond = jnp.concatenate([c, c_ctx[None], jnp.zeros((mod_rows - batch - 1, D_MODEL), F32)], axis=0)
    mod = _mod_call(cond, mod_w, mod_b)

    p_lb = jax.nn.softmax(hg_lb_logits.astype(F32), axis=0)
    lbs = jnp.maximum(jnp.cumsum(p_lb, axis=0) - p_lb[0:1], 0.0)
    tri = jnp.stack([jnp.tril(jnp.ones((HG_C, HG_C), F32)), jnp.triu(jnp.ones((HG_C, HG_C), F32))]).astype(BF16)
    head_id = np.arange(HG_HV) // HG_DV
    bd = jnp.asarray(head_id[:, None] == head_id[None, :], BF16)
    cos_t, sin_t = _rope_tables(seq)

    ffn_w = (ffn_w_gate.astype(BF16), ffn_w_up.astype(BF16), ffn_w_down.astype(BF16))

    for i in range(DEPTH):
        last = i == DEPTH - 1
        j = i // 2
        mod_l = mod[i].reshape(mod_rows, 1, N_MOD * D_MODEL)
        g = norm_g[i].reshape(3, 1, D_MODEL)
        hs = _ffn_call(rows, hs, mod_l, g[0], ffn_w, i, 0, 0, rows.all_tiles)
        n_out = rows.lat_tiles if last else rows.all_tiles
        if i % 2 == 0:
            w_in = hg_w_in[j].astype(BF16)
            w_vt = w_in[:, HG_I * D_MODEL:(HG_I + 1) * D_MODEL].T
            pb, pf, vt = _hgrn_in_call(rows, hs, mod_l, g[1], w_in, w_vt, lbs[j])
            o_f, o_b = _hgrn_scan_call(rows, pb, pf, vt, tri)
            gn = jnp.tile(hg_gn[j], HG_HEADS).reshape(1, HG_HV)
            mixer, mixer_args = "hgrn", (o_f, o_b, pf, gn, bd, hg_w_out[j].astype(BF16))
        else:
            w_dn, w_q, w_qs, w_k, w_vt = _mla_weights(mla_w_down[j], mla_w_uq[j], mla_w_ukv[j])
            q, k, vt = _mla_proj_call(rows, hs, mod_l, g[1], w_dn, mla_q_norm[j].reshape(1, Q_LORA),
                                      mla_kv_norm[j].reshape(1, KV_LORA), w_q, w_qs, w_k, w_vt, cos_t, sin_t)
            mixer, mixer_args = "mla", (_attn_call(rows, q, k, vt, with_ctx=not last), mla_w_o[j].astype(BF16))
        hs = _ffn_call(rows, hs, mod_l, g[2], ffn_w, i, 1, 6, n_out,
                       final_g=final_g.reshape(1, D_MODEL) if last else None, mixer=mixer, mixer_args=mixer_args)
    return hs.reshape(batch, seq, D_MODEL)
```

```python
import functools

import jax
import jax.numpy as jnp
import numpy as np
from jax import lax
from jax.experimental import pallas as pl
from jax.experimental.pallas import tpu as pltpu

F32 = jnp.float32
BF16 = jnp.bfloat16

D_MODEL = 1024
DEPTH = 4
GRID_W = 64
N_MOD = 9
FFN_HIDDEN = 2816
RMS_EPS = 1e-6
HG_HEADS = 8
HG_DK = 128
HG_DV = 128
HG_HK = HG_HEADS * HG_DK
HG_HV = HG_HEADS * HG_DV
HG_IN = 3 * HG_HK + HG_HV + D_MODEL
F_MIN = 1e-6
MLA_HEADS = 8
MLA_NOPE = 128
MLA_ROPE = 64
MLA_V = 128
Q_LORA = 384
KV_LORA = 256
MLA_QK = MLA_NOPE + MLA_ROPE
MLA_SCALE = MLA_QK ** -0.5
ROPE_BASE = 10000.0
LOG2_E = 1.4426950408889634

LANES = 128
SUBLANES = 8
MXU_DIM = 256
VMEM_LIMIT = 56 * 1024 * 1024

TM = 512
FFN_CHUNK = MXU_DIM
HG_C = 128
HG_SUB = 2
TQ = 256
Q_SUB = 16
KEY_CHUNK = 1024
MLA_HW = 2 * LANES
MLA_DOWN_P = Q_LORA + KV_LORA + 2 * LANES


def _cparams(sem):
    return pltpu.CompilerParams(dimension_semantics=sem, vmem_limit_bytes=VMEM_LIMIT)


def _resident(shape):
    nd = len(shape)
    return pl.BlockSpec(shape, lambda *_: (0,) * nd, pipeline_mode=pl.Buffered(1))


def _dot(a, b):
    return jnp.dot(a, b, preferred_element_type=F32)


def _dot_nt(a, b):
    return lax.dot_general(a, b, (((1,), (1,)), ((), ())), preferred_element_type=F32)


def _rms_modulate(h, g, shift, scale):
    r = lax.rsqrt(jnp.mean(h * h, axis=-1, keepdims=True) + RMS_EPS)
    gs = g * (1.0 + scale)
    pieces = []
    for c0 in range(0, h.shape[1], MXU_DIM):
        cols = slice(c0, c0 + MXU_DIM)
        pieces.append(((h[:, cols] * r) * gs[:, cols] + shift[:, cols]).astype(BF16))
    return jnp.concatenate(pieces, axis=1)


def _mod_slices(mod_ref, base, n):
    m = mod_ref[0]
    return [m[:, (base + j) * D_MODEL:(base + j + 1) * D_MODEL] for j in range(n)]


def _mod_kernel(c_ref, w_ref, b_ref, o_ref):
    c = c_ref[...]
    s = c * jax.nn.sigmoid(c)
    s_hi = s.astype(BF16)
    s_lo = (s - s_hi.astype(F32)).astype(BF16)
    w = w_ref[0]
    w_hi = w.astype(BF16)
    w_lo = (w - w_hi.astype(F32)).astype(BF16)
    acc = _dot(s_hi, w_hi) + (_dot(s_lo, w_hi) + _dot(s_hi, w_lo))
    o_ref[0] = acc + b_ref[0]


def _mod_call(cond, mod_w, mod_b):
    rows = cond.shape[0]
    tn = D_MODEL
    n_out = N_MOD * D_MODEL
    return pl.pallas_call(
        _mod_kernel,
        grid=(DEPTH, n_out // tn),
        in_specs=[
            pl.BlockSpec((rows, D_MODEL), lambda l, n: (0, 0)),
            pl.BlockSpec((1, D_MODEL, tn), lambda l, n: (l, 0, n)),
            pl.BlockSpec((1, 1, tn), lambda l, n: (l, 0, n)),
        ],
        out_specs=pl.BlockSpec((1, rows, tn), lambda l, n: (l, 0, n)),
        out_shape=jax.ShapeDtypeStruct((DEPTH, rows, n_out), F32),
        compiler_params=_cparams(("arbitrary", "arbitrary")),
        name="mod_vectors",
    )(cond, mod_w, mod_b.reshape(DEPTH, 1, n_out))


class _Rows:
    def __init__(self, batch, seq, ctx_len):
        self.batch, self.seq, self.ctx_len = batch, seq, ctx_len
        self.n_lat = batch * seq
        self.n_ctx = batch * ctx_len
        self.n_tok = self.n_lat + self.n_ctx
        assert seq % TM == 0 and self.n_ctx % TM == 0
        self.lat_tiles = self.n_lat // TM
        self.tiles_per_seq = seq // TM
        self.all_tiles = self.n_tok // TM

    def mod_spec(self):
        lat_tiles, tps, ctx_row = self.lat_tiles, self.tiles_per_seq, self.batch
        return pl.BlockSpec(
            (1, 1, N_MOD * D_MODEL),
            lambda i: (jnp.where(i < lat_tiles, i // tps, ctx_row), 0, 0))


def _tile_spec(width, col=0):
    return pl.BlockSpec((TM, width), lambda i: (i, col))


def _hgrn_readout(of_ref, ob_ref, gate_ref, gn_ref, bd_ref, w_ref):
    o = of_ref[...] + ob_ref[...]
    ss = _dot((o * o).astype(BF16), bd_ref[...])
    on = o * lax.rsqrt(ss * (1.0 / HG_DV) + RMS_EPS) * gn_ref[...]
    gt = gate_ref[...]
    return _dot((on * (gt * jax.nn.sigmoid(gt))).astype(BF16), w_ref[...])


def _pick_tile(lat_ref, ctx_ref, scr_ref, lat_tiles):
    @pl.when(pl.program_id(0) < lat_tiles)
    def _():
        scr_ref[...] = lat_ref[...]

    @pl.when(pl.program_id(0) >= lat_tiles)
    def _():
        scr_ref[...] = ctx_ref[...]

    return scr_ref[...]


def _ffn_kernel(*refs, base, final, mixer, lat_tiles, split_h, split_attn):
    rest = list(refs)
    attn_scr = rest.pop() if split_attn else None
    h_scr = rest.pop() if split_h else None
    o_ref = rest.pop()
    h = _pick_tile(rest.pop(0), rest.pop(0), h_scr, lat_tiles) if split_h else rest.pop(0)[...]
    mod_ref, g_ref = rest.pop(0), rest.pop(0)
    if mixer is not None:
        (res_gate,) = _mod_slices(mod_ref, 5, 1)
        if mixer == "hgrn":
            y = _hgrn_readout(*rest[:6])
            rest = rest[6:]
        else:
            attn = (_pick_tile(rest.pop(0), rest.pop(0), attn_scr, lat_tiles) if split_attn
                    else rest.pop(0)[...])
            y = _dot(attn, rest.pop(0)[...])
        h = h + res_gate * y
    wg_ref, wu_ref, wd_ref = rest[:3]
    shift, scale, gate = _mod_slices(mod_ref, base, 3)
    a = _rms_modulate(h, g_ref[...], shift, scale)
    acc = jnp.zeros((TM, D_MODEL), F32)
    for j in range(FFN_HIDDEN // FFN_CHUNK):
        sl = slice(j * FFN_CHUNK, (j + 1) * FFN_CHUNK)
        gg = _dot(a, wg_ref[:, sl])
        uu = _dot(a, wu_ref[:, sl])
        hm = (gg * jax.nn.sigmoid(gg) * uu).astype(BF16)
        acc = acc + _dot(hm, wd_ref[sl, :])
    out = h + (0.5 * gate) * acc
    if final:
        fg_ref = rest[3]
        ms = jnp.mean(out * out, axis=-1, keepdims=True)
        out = out * lax.rsqrt(ms + RMS_EPS) * fg_ref[...]
    o_ref[...] = out


def _ffn_call(rows, hs, mod_l, g, ffn_w, layer, half, base, n_tiles, final_g=None, mixer=None, mixer_args=()):
    final = final_g is not None
    split_h = isinstance(hs, tuple)
    split_attn = mixer == "mla" and mixer_args[1] is not None
    lat_tiles = rows.lat_tiles

    def stacked(shape):
        return pl.BlockSpec((None, None) + shape, lambda i: (layer, half, 0, 0), pipeline_mode=pl.Buffered(1))

    def split_specs(width):
        return [pl.BlockSpec((TM, width), lambda i: (jnp.minimum(i, lat_tiles - 1), 0)),
                pl.BlockSpec((TM, width), lambda i: (jnp.maximum(i - lat_tiles, 0), 0))]

    in_specs, args = (split_specs(D_MODEL), list(hs)) if split_h else ([_tile_spec(D_MODEL)], [hs])
    in_specs += [rows.mod_spec(), _resident((1, D_MODEL))]
    args += [mod_l, g]
    if mixer == "hgrn":
        in_specs += [_tile_spec(HG_HV), _tile_spec(HG_HV), _tile_spec(D_MODEL, col=HG_GATE),
                     _resident((1, HG_HV)), _resident((HG_HV, HG_HV)), _resident((HG_HV, D_MODEL))]
        args += list(mixer_args)
    elif mixer == "mla":
        attn_lat, attn_ctx, w_o = mixer_args
        v_w = MLA_HEADS * MLA_V
        in_specs += (split_specs(v_w) if split_attn else [_tile_spec(v_w)]) + [_resident((v_w, D_MODEL))]
        args += ([attn_lat, attn_ctx] if split_attn else [attn_lat]) + [w_o]
    in_specs += [stacked((D_MODEL, FFN_HIDDEN)), stacked((D_MODEL, FFN_HIDDEN)), stacked((FFN_HIDDEN, D_MODEL))]
    args += list(ffn_w)
    if final:
        in_specs.append(_resident((1, D_MODEL)))
        args.append(final_g)
    scratch = ([pltpu.VMEM((TM, D_MODEL), F32)] if split_h else []) + (
        [pltpu.VMEM((TM, MLA_HEADS * MLA_V), BF16)] if split_attn else [])
    return pl.pallas_call(
        functools.partial(_ffn_kernel, base=base, final=final, mixer=mixer, lat_tiles=lat_tiles,
                          split_h=split_h, split_attn=split_attn),
        grid=(n_tiles,),
        in_specs=in_specs,
        out_specs=_tile_spec(D_MODEL),
        out_shape=jax.ShapeDtypeStruct((n_tiles * TM, D_MODEL), F32),
        scratch_shapes=scratch,
        compiler_params=_cparams(("arbitrary",)),
        name="ffn_half",
    )(*args)


HG_Q, HG_KF, HG_KB = range(3)
HG_LGF, HG_LGB, HG_GATE = range(3)
HG_ZF, HG_ZB, HG_I, HG_G = 1, 2, 3, 4


def _hgrn_in_kernel(h_ref, mod_ref, g_ref, w_ref, wvt_ref, lb_ref, pb_ref, pf_ref, vt_ref):
    shift, scale = _mod_slices(mod_ref, 3, 2)
    a = _rms_modulate(h_ref[...], g_ref[...], shift, scale)

    for piece in range(D_MODEL // MXU_DIM):
        def proj(j):
            return _dot(a, w_ref[:, pl.ds(j * D_MODEL + piece * MXU_DIM, MXU_DIM)])

        def put(ref, blk, val):
            ref[:, pl.ds(blk * D_MODEL + piece * MXU_DIM, MXU_DIM)] = val.astype(ref.dtype)

        put(pb_ref, HG_Q, proj(0))
        vt_rows = pl.ds(piece * MXU_DIM, MXU_DIM)
        vt_ref[vt_rows, :] = _dot_nt(wvt_ref[vt_rows, :], a).astype(BF16)
        for d, (k_blk, lg_blk) in enumerate(((HG_KF, HG_LGF), (HG_KB, HG_LGB))):
            lb = lb_ref[d:d + 1, pl.ds(piece * MXU_DIM, MXU_DIM)]
            f = lb + (1.0 - lb) * jax.nn.sigmoid(proj(HG_ZF + d))
            put(pb_ref, k_blk, 1.0 - f)
            put(pf_ref, lg_blk, jnp.log2(jnp.maximum(f, F_MIN)))
        put(pf_ref, HG_GATE, proj(HG_G))


def _hgrn_in_call(rows, hs, mod_l, g, w_in, w_vt, lb):
    return pl.pallas_call(
        _hgrn_in_kernel,
        grid=(rows.all_tiles,),
        in_specs=[_tile_spec(D_MODEL), rows.mod_spec(), _resident((1, D_MODEL)),
                  _resident((D_MODEL, HG_IN)), _resident((HG_HV, D_MODEL)), _resident((2, HG_HK))],
        out_specs=[_tile_spec(3 * D_MODEL), _tile_spec(3 * D_MODEL), pl.BlockSpec((HG_HV, TM), lambda i: (0, i))],
        out_shape=[jax.ShapeDtypeStruct((rows.n_tok, 3 * D_MODEL), BF16),
                   jax.ShapeDtypeStruct((rows.n_tok, 3 * D_MODEL), F32),
                   jax.ShapeDtypeStruct((HG_HV, rows.n_tok), BF16)],
        compiler_params=_cparams(("arbitrary",)),
        name="hgrn_in_proj",
    )(hs, mod_l, g, w_in, w_vt, lb)


def _mla_proj_kernel(h_ref, mod_ref, g_ref, wd_ref, qn_ref, kvn_ref, wq_ref, wqs_ref, wk_ref, wvt_ref,
                     cos_ref, sin_ref, q_ref, k_ref, vt_ref):
    shift, scale = _mod_slices(mod_ref, 3, 2)
    a = _rms_modulate(h_ref[...], g_ref[...], shift, scale)
    dp = _dot(a, wd_ref[...])
    cq = dp[:, :Q_LORA]
    ckv = dp[:, Q_LORA:Q_LORA + KV_LORA]
    kr_a = dp[:, Q_LORA + KV_LORA:Q_LORA + KV_LORA + LANES]
    kr_b = dp[:, Q_LORA + KV_LORA + LANES:]
    cqn = (cq * lax.rsqrt(jnp.mean(cq * cq, axis=-1, keepdims=True) + RMS_EPS) * qn_ref[...]).astype(BF16)
    ckvn = (ckv * lax.rsqrt(jnp.mean(ckv * ckv, axis=-1, keepdims=True) + RMS_EPS) * kvn_ref[...]).astype(BF16)
    cos = cos_ref[...]
    sin = sin_ref[...]
    cos_r, sin_r = cos[:, LANES:], sin[:, LANES:]
    q_scale = MLA_SCALE * LOG2_E
    qm = _dot(cqn, wq_ref[...])
    qs = _dot(cqn, wqs_ref[...])
    pieces = []
    for hh in range(MLA_HEADS):
        nope = qm[:, hh * MLA_HW:hh * MLA_HW + LANES] * q_scale
        rope = (qm[:, hh * MLA_HW + LANES:(hh + 1) * MLA_HW] * (cos_r * q_scale)
                + qs[:, hh * LANES:(hh + 1) * LANES] * (sin_r * q_scale))
        pieces += [nope.astype(BF16), rope.astype(BF16)]
    q_ref[...] = jnp.concatenate(pieces, axis=1)
    kr = (kr_a * cos_r + kr_b * sin_r).astype(BF16)
    kn = _dot(ckvn, wk_ref[...]).astype(BF16)
    pieces = []
    for hh in range(MLA_HEADS):
        pieces += [kn[:, hh * MLA_NOPE:(hh + 1) * MLA_NOPE], kr]
    k_ref[...] = jnp.concatenate(pieces, axis=1)
    vt_ref[...] = _dot_nt(wvt_ref[...], ckvn).astype(BF16)


def _mla_proj_call(rows, hs, mod_l, g, wd, qn, kvn, wq, wqs, wk, wvt, cos_t, sin_t):
    lat_tiles, tps = rows.lat_tiles, rows.tiles_per_seq
    rope_spec = pl.BlockSpec((TM, MLA_HW), lambda i: (jnp.where(i < lat_tiles, i % tps, tps), 0))
    qk_w = MLA_HEADS * MLA_HW
    v_w = MLA_HEADS * MLA_V
    return pl.pallas_call(
        _mla_proj_kernel,
        grid=(rows.all_tiles,),
        in_specs=[_tile_spec(D_MODEL), rows.mod_spec(), _resident((1, D_MODEL)),
                  _resident((D_MODEL, MLA_DOWN_P)), _resident((1, Q_LORA)), _resident((1, KV_LORA)),
                  _resident((Q_LORA, qk_w)), _resident((Q_LORA, MLA_HEADS * LANES)),
                  _resident((KV_LORA, MLA_HEADS * MLA_NOPE)), _resident((v_w, KV_LORA)),
                  rope_spec, rope_spec],
        out_specs=[_tile_spec(qk_w), _tile_spec(qk_w), pl.BlockSpec((v_w, TM), lambda i: (0, i))],
        out_shape=[jax.ShapeDtypeStruct((rows.n_tok, qk_w), BF16),
                   jax.ShapeDtypeStruct((rows.n_tok, qk_w), BF16),
                   jax.ShapeDtypeStruct((v_w, rows.n_tok), BF16)],
        compiler_params=_cparams(("arbitrary",)),
        name="mla_proj",
    )(hs, mod_l, g, wd, qn, kvn, wq, wqs, wk, wvt, cos_t, sin_t)


def _hgrn_cumsum(lg, tri):
    width = lg.shape[1]
    hi = lg.astype(BF16)
    lo = (lg - hi.astype(F32)).astype(BF16)
    c2 = _dot(tri, jnp.concatenate([hi, lo], axis=1))
    return c2[:, :width] + c2[:, width:]


def _block_diag(a, b):
    zero = jnp.zeros_like(a)
    return jnp.concatenate([jnp.concatenate([a, zero], axis=1), jnp.concatenate([zero, b], axis=1)], axis=0)


def _hgrn_pair(qb, kb, cum, vt, st_a_ref, st_b_ref, rev):
    c = HG_C
    pair_w = 2 * LANES
    row = lax.broadcasted_iota(jnp.int32, (c, pair_w), 0)
    col = lax.broadcasted_iota(jnp.int32, (c, pair_w), 1) & (LANES - 1)
    x = row ^ col
    causal = (row <= col) if rev else (row >= col)

    def halves(m):
        return m[:, :LANES], m[:, LANES:]

    def pair_scores(w_q, w_k):
        return _dot_nt(qb * w_q.astype(BF16), _block_diag(*halves(kb * w_k.astype(BF16))))

    edge = cum[0:1, :] if rev else cum[c - 1:c, :]
    st_a, st_b = st_a_ref[...], st_b_ref[...]
    o = _dot_nt(qb * jnp.exp2(cum).astype(BF16), _block_diag(st_a.astype(BF16), st_b.astype(BF16)))
    upd = _dot(vt, kb * jnp.exp2(edge - cum).astype(BF16))
    dec_a, dec_b = halves(jnp.exp2(edge))
    st_a_ref[...] = st_a * dec_a + upd[:LANES, :LANES]
    st_b_ref[...] = st_b * dec_b + upd[LANES:, LANES:]

    mid_row = SUBLANES // 2 if rev else SUBLANES // 2 - 1
    ref8 = jnp.concatenate(
        [jnp.broadcast_to(cum[b * SUBLANES + mid_row:b * SUBLANES + mid_row + 1, :], (SUBLANES, pair_w))
         for b in range(c // SUBLANES)], axis=0)
    d8 = cum - ref8
    a = pair_scores(jnp.exp2(d8), jnp.exp2(-d8))
    lvl = 3
    while (1 << lvl) < c:
        half = 1 << lvl
        parts = []
        for b in range(c // (2 * half)):
            lo = cum[2 * b * half:(2 * b + 1) * half, :]
            hi = cum[(2 * b + 1) * half:(2 * b + 2) * half, :]
            if rev:
                ref = hi[0:1, :]
                parts += [lo - ref, ref - hi]
            else:
                ref = lo[half - 1:half, :]
                parts += [ref - lo, hi - ref]
        w = jnp.exp2(jnp.concatenate(parts, axis=0))
        a = jnp.where(x < half, a, pair_scores(w, w))
        lvl += 1
    a = jnp.where(causal, a, 0.0)
    return o + _dot_nt(a.astype(BF16), _block_diag(vt[:LANES, :], vt[LANES:, :]))


def _hgrn_scan_kernel(qf_ref, kf_ref, vtf_ref, lgf_ref, qb_ref, kb_ref, vtb_ref, lgb_ref, tri_ref,
                      of_ref, ob_ref, st_ref):
    @pl.when(pl.program_id(1) == 0)
    def _():
        st_ref[...] = jnp.zeros_like(st_ref)

    for sub in range(HG_SUB):
        rf = slice(sub * HG_C, (sub + 1) * HG_C)
        rb = slice((HG_SUB - 1 - sub) * HG_C, (HG_SUB - sub) * HG_C)
        cum_f = _hgrn_cumsum(lgf_ref[rf, :], tri_ref[0])
        cum_b = _hgrn_cumsum(lgb_ref[rb, :], tri_ref[1])
        for p in range(HG_HEADS // 2):
            cols = slice(p * 2 * LANES, (p + 1) * 2 * LANES)
            of_ref[rf, cols] = _hgrn_pair(qf_ref[rf, cols], kf_ref[rf, cols], cum_f[:, cols], vtf_ref[cols, rf],
                                          st_ref.at[0, 2 * p], st_ref.at[0, 2 * p + 1], False)
            ob_ref[rb, cols] = _hgrn_pair(qb_ref[rb, cols], kb_ref[rb, cols], cum_b[:, cols], vtb_ref[cols, rb],
                                          st_ref.at[1, 2 * p], st_ref.at[1, 2 * p + 1], True)


def _hgrn_scan_call(rows, pb, pf, vt, tri):
    step_rows = HG_SUB * HG_C
    assert rows.ctx_len % step_rows == 0 and rows.seq % step_rows == 0
    c_chunks = rows.ctx_len // step_rows
    l_chunks = rows.seq // step_rows
    steps = c_chunks + l_chunks
    ctx0 = rows.n_lat // step_rows

    def fwd_blk(b, s):
        return jnp.where(s < c_chunks, ctx0 + b * c_chunks + s, b * l_chunks + s - c_chunks)

    def bwd_blk(b, s):
        return jnp.where(s < c_chunks, ctx0 + b * c_chunks + (c_chunks - 1 - s),
                         b * l_chunks + (l_chunks - 1 - (s - c_chunks)))

    def in_spec(blk, col):
        return pl.BlockSpec((step_rows, HG_HK), lambda b, s: (blk(b, s), col))

    def vt_spec(blk):
        return pl.BlockSpec((HG_HV, step_rows), lambda b, s: (0, blk(b, s)))

    out_sd = jax.ShapeDtypeStruct((rows.n_tok, HG_HV), F32)
    return pl.pallas_call(
        _hgrn_scan_kernel,
        grid=(rows.batch, steps),
        in_specs=[in_spec(fwd_blk, HG_Q), in_spec(fwd_blk, HG_KF), vt_spec(fwd_blk), in_spec(fwd_blk, HG_LGF),
                  in_spec(bwd_blk, HG_Q), in_spec(bwd_blk, HG_KB), vt_spec(bwd_blk), in_spec(bwd_blk, HG_LGB),
                  pl.BlockSpec((2, HG_C, HG_C), lambda b, s: (0, 0, 0))],
        out_specs=[pl.BlockSpec((step_rows, HG_HV), lambda b, s: (fwd_blk(b, s), 0)),
                   pl.BlockSpec((step_rows, HG_HV), lambda b, s: (bwd_blk(b, s), 0))],
        out_shape=[out_sd, out_sd],
        scratch_shapes=[pltpu.VMEM((2, HG_HEADS, HG_DV, HG_DK), F32)],
        compiler_params=_cparams(("arbitrary", "arbitrary")),
        name="hgrn_scan",
    )(pb, pb, vt, pf, pb, pb, vt, pf, tri)


def _attn_scores(q, kc_ref, kl_ref=None):
    scores = [_dot_nt(kc_ref[...], q)]
    if kl_ref is not None:
        scores += [_dot_nt(kl_ref[pl.ds(j * KEY_CHUNK, KEY_CHUNK), :], q)
                   for j in range(kl_ref.shape[0] // KEY_CHUNK)]
    m = jnp.max(scores[0], axis=0, keepdims=True)
    for s in scores[1:]:
        m = jnp.maximum(m, jnp.max(s, axis=0, keepdims=True))
    return scores, m


def _attn_values(scores, m, vtc_ref, vtl_ref=None):
    den, acc = None, None
    for j, s in enumerate(scores):
        p = jnp.exp2(s - m)
        vt = vtc_ref[...] if j == 0 else vtl_ref[:, pl.ds((j - 1) * KEY_CHUNK, KEY_CHUNK)]
        d_j = jnp.sum(p, axis=0, keepdims=True)
        a_j = _dot(vt, p.astype(BF16))
        den, acc = (d_j, a_j) if den is None else (den + d_j, acc + a_j)
    return (acc / den).T


def _attn_kernel(*refs, with_ctx):
    if with_ctx:
        q_ref, qc_ref, kl_ref, kc_ref, vtl_ref, vtc_ref, o_ref, oc_ref = refs
    else:
        q_ref, kl_ref, kc_ref, vtl_ref, vtc_ref, o_ref = refs
    nxt = _attn_scores(q_ref[pl.ds(0, TQ), :], kc_ref, kl_ref)
    for j in range(Q_SUB):
        cur = nxt
        if j + 1 < Q_SUB:
            nxt = _attn_scores(q_ref[pl.ds((j + 1) * TQ, TQ), :], kc_ref, kl_ref)
        elif with_ctx:
            nxt = _attn_scores(qc_ref[...], kc_ref)
        o_ref[pl.ds(j * TQ, TQ), :] = _attn_values(*cur, vtc_ref, vtl_ref).astype(o_ref.dtype)
    if with_ctx:
        oc_ref[...] = _attn_values(*nxt, vtc_ref).astype(oc_ref.dtype)


def _attn_call(rows, q, k, vt, with_ctx):
    tq = Q_SUB * TQ
    assert rows.ctx_len == TQ and rows.seq % tq == 0
    n_q = rows.seq // tq
    ctx0 = rows.n_lat // TQ
    v_w = MLA_HEADS * MLA_V
    ctx_qk = pl.BlockSpec((TQ, MLA_HW), lambda b, h, qi: (ctx0 + b, h))
    ctx_vt = pl.BlockSpec((MLA_V, TQ), lambda b, h, qi: (h, ctx0 + b))
    lat_q = pl.BlockSpec((tq, MLA_HW), lambda b, h, qi: (b * n_q + qi, h))
    lat_k = pl.BlockSpec((rows.seq, MLA_HW), lambda b, h, qi: (b, h))
    lat_vt = pl.BlockSpec((MLA_V, rows.seq), lambda b, h, qi: (h, b))
    lat_o = pl.BlockSpec((tq, MLA_V), lambda b, h, qi: (b * n_q + qi, h))
    lat_sd = jax.ShapeDtypeStruct((rows.n_lat, v_w), BF16)
    if with_ctx:
        assert n_q == 1
        in_specs, args = [lat_q, ctx_qk, lat_k, ctx_qk, lat_vt, ctx_vt], (q, q, k, k, vt, vt)
        out_specs = [lat_o, pl.BlockSpec((TQ, MLA_V), lambda b, h, qi: (b, h))]
        out_shape = [lat_sd, jax.ShapeDtypeStruct((rows.n_ctx, v_w), BF16)]
    else:
        in_specs, args = [lat_q, lat_k, ctx_qk, lat_vt, ctx_vt], (q, k, k, vt, vt)
        out_specs, out_shape = lat_o, lat_sd
    out = pl.pallas_call(
        functools.partial(_attn_kernel, with_ctx=with_ctx),
        grid=(rows.batch, MLA_HEADS, n_q),
        in_specs=in_specs,
        out_specs=out_specs,
        out_shape=out_shape,
        compiler_params=_cparams(("arbitrary", "arbitrary", "arbitrary")),
        name="mla_attention",
    )(*args)
    return tuple(out) if with_ctx else (out, None)


def _rope_swap_cols():
    idx = np.arange(MLA_ROPE).reshape(2, 2, MLA_ROPE // 4)
    return idx[:, ::-1, :].reshape(-1)


def _mla_weights(w_down, w_uq, w_ukv):
    swap = _rope_swap_cols()
    kr = w_down[:, Q_LORA + KV_LORA:]
    zpad = jnp.zeros((D_MODEL, LANES - MLA_ROPE), w_down.dtype)
    wd = jnp.concatenate([w_down[:, :Q_LORA + KV_LORA], kr, zpad, kr[:, swap], zpad], axis=1)
    uq = w_uq.reshape(Q_LORA, MLA_HEADS, MLA_QK)
    nope, rope = uq[..., :MLA_NOPE], uq[..., MLA_NOPE:]
    z64 = jnp.zeros((Q_LORA, MLA_HEADS, MLA_HW - MLA_QK), w_uq.dtype)
    wq = jnp.concatenate([nope, rope, z64], axis=-1).reshape(Q_LORA, MLA_HEADS * MLA_HW)
    wqs = jnp.concatenate([rope[..., swap], z64], axis=-1).reshape(Q_LORA, MLA_HEADS * LANES)
    ukv = w_ukv.reshape(KV_LORA, MLA_HEADS, MLA_NOPE + MLA_V)
    wk = ukv[..., :MLA_NOPE].reshape(KV_LORA, MLA_HEADS * MLA_NOPE)
    wvt = ukv[..., MLA_NOPE:].reshape(KV_LORA, MLA_HEADS * MLA_V).T
    return wd.astype(BF16), wq.astype(BF16), wqs.astype(BF16), wk.astype(BF16), wvt.astype(BF16)


def _rope_tables(seq):
    n_rows = seq // GRID_W
    row = jnp.repeat(jnp.arange(n_rows), GRID_W).astype(F32)
    col = jnp.tile(jnp.arange(GRID_W), n_rows).astype(F32)
    axis_dims = MLA_ROPE // 2
    inv = 1.0 / (ROPE_BASE ** (jnp.arange(0, axis_dims, 2, dtype=F32) / axis_dims))
    ang_r, ang_c = row[:, None] * inv, col[:, None] * inv
    cos64 = jnp.concatenate([jnp.cos(ang_r)] * 2 + [jnp.cos(ang_c)] * 2, axis=1)
    sin64 = jnp.concatenate([-jnp.sin(ang_r), jnp.sin(ang_r), -jnp.sin(ang_c), jnp.sin(ang_c)], axis=1)
    cos64 = jnp.concatenate([cos64, jnp.ones((TM, MLA_ROPE), F32)], axis=0)
    sin64 = jnp.concatenate([sin64, jnp.zeros((TM, MLA_ROPE), F32)], axis=0)
    n = seq + TM
    cos_t = jnp.concatenate([jnp.ones((n, MLA_NOPE), F32), cos64, jnp.zeros((n, MLA_HW - MLA_QK), F32)], axis=1)
    sin_t = jnp.concatenate([jnp.zeros((n, MLA_NOPE), F32), sin64, jnp.zeros((n, MLA_HW - MLA_QK), F32)], axis=1)
    return cos_t, sin_t


def kernel(x, c, ctx, c_ctx, mod_w, mod_b, norm_g, ffn_w_gate, ffn_w_up, ffn_w_down, hg_w_in, hg_w_out,
           hg_gn, hg_lb_logits, mla_w_down, mla_q_norm, mla_w_uq, mla_kv_norm, mla_w_ukv, mla_w_o, final_g):
    batch, seq, _ = x.shape
    rows = _Rows(batch, seq, ctx.shape[1])
    hs = (x.reshape(rows.n_lat, D_MODEL), ctx.reshape(rows.n_ctx, D_MODEL))

    mod_rows = -(-(batch + 1) // SUBLANES) * SUBLANES
    cond = jnp.concatenate([c, c_ctx[None], jnp.zeros((mod_rows - batch - 1, D_MODEL), F32)], axis=0)
    mod = _mod_call(cond, mod_w, mod_b)

    p_lb = jax.nn.softmax(hg_lb_logits.astype(F32), axis=0)
    lbs = jnp.maximum(jnp.cumsum(p_lb, axis=0) - p_lb[0:1], 0.0)
    tri = jnp.stack([jnp.tril(jnp.ones((HG_C, HG_C), F32)), jnp.triu(jnp.ones((HG_C, HG_C), F32))]).astype(BF16)
    head_id = np.arange(HG_HV) // HG_DV
    bd = jnp.asarray(head_id[:, None] == head_id[None, :], BF16)
    cos_t, sin_t = _rope_tables(seq)

    ffn_w = (ffn_w_gate.astype(BF16), ffn_w_up.astype(BF16), ffn_w_down.astype(BF16))

    for i in range(DEPTH):
        last = i == DEPTH - 1
        j = i // 2
        mod_l = mod[i].reshape(mod_rows, 1, N_MOD * D_MODEL)
        g = norm_g[i].reshape(3, 1, D_MODEL)
        hs = _ffn_call(rows, hs, mod_l, g[0], ffn_w, i, 0, 0, rows.all_tiles)
        n_out = rows.lat_tiles if last else rows.all_tiles
        if i % 2 == 0:
            w_in = hg_w_in[j].astype(BF16)
            w_vt = w_in[:, HG_I * D_MODEL:(HG_I + 1) * D_MODEL].T
            pb, pf, vt = _hgrn_in_call(rows, hs, mod_l, g[1], w_in, w_vt, lbs[j])
            o_f, o_b = _hgrn_scan_call(rows, pb, pf, vt, tri)
            gn = jnp.tile(hg_gn[j], HG_HEADS).reshape(1, HG_HV)
            mixer, mixer_args = "hgrn", (o_f, o_b, pf, gn, bd, hg_w_out[j].astype(BF16))
        else:
            w_dn, w_q, w_qs, w_k, w_vt = _mla_weights(mla_w_down[j], mla_w_uq[j], mla_w_ukv[j])
            q, k, vt = _mla_proj_call(rows, hs, mod_l, g[1], w_dn, mla_q_norm[j].reshape(1, Q_LORA),
                                      mla_kv_norm[j].reshape(1, KV_LORA), w_q, w_qs, w_k, w_vt, cos_t, sin_t)
            attn_lat, attn_ctx = _attn_call(rows, q, k, vt, with_ctx=not last)
            mixer, mixer_args = "mla", (attn_lat, attn_ctx, mla_w_o[j].astype(BF16))
        hs = _ffn_call(rows, hs, mod_l, g[2], ffn_w, i, 1, 6, n_out,
                       final_g=final_g.reshape(1, D_MODEL) if last else None, mixer=mixer, mixer_args=mixer_args)
    return hs.reshape(batch, seq, D_MODEL)
```

```python
import functools

import jax
import jax.numpy as jnp
import numpy as np
from jax import lax
from jax.experimental import pallas as pl
from jax.experimental.pallas import tpu as pltpu

F32 = jnp.float32
BF16 = jnp.bfloat16

D_MODEL = 1024
DEPTH = 4
GRID_W = 64
N_MOD = 9
FFN_HIDDEN = 2816
RMS_EPS = 1e-6
HG_HEADS = 8
HG_DK = 128
HG_DV = 128
HG_HK = HG_HEADS * HG_DK
HG_HV = HG_HEADS * HG_DV
HG_IN = 3 * HG_HK + HG_HV + D_MODEL
F_MIN = 1e-6
MLA_HEADS = 8
MLA_NOPE = 128
MLA_ROPE = 64
MLA_V = 128
Q_LORA = 384
KV_LORA = 256
MLA_QK = MLA_NOPE + MLA_ROPE
MLA_SCALE = MLA_QK ** -0.5
ROPE_BASE = 10000.0
LOG2_E = 1.4426950408889634

LANES = 128
SUBLANES = 8
MXU_DIM = 256
VMEM_LIMIT = 56 * 1024 * 1024

TM = 512
CAST_ROWS = 512
FFN_CHUNK = MXU_DIM
HG_C = 128
HG_SUB = 2
TQ = 256
Q_SUB = 16
KEY_CHUNK = 1024
MLA_HW = 2 * LANES
MLA_DOWN_P = Q_LORA + KV_LORA + 2 * LANES


def _cparams(sem):
    return pltpu.CompilerParams(dimension_semantics=sem, vmem_limit_bytes=VMEM_LIMIT)


def _resident(shape):
    nd = len(shape)
    return pl.BlockSpec(shape, lambda *_: (0,) * nd, pipeline_mode=pl.Buffered(1))


def _dot(a, b):
    return jnp.dot(a, b, preferred_element_type=F32)


def _dot_nt(a, b):
    return lax.dot_general(a, b, (((1,), (1,)), ((), ())), preferred_element_type=F32)


def _rms_modulate(h, g, shift, scale):
    r = lax.rsqrt(jnp.mean(h * h, axis=-1, keepdims=True) + RMS_EPS)
    gs = g * (1.0 + scale)
    pieces = []
    for c0 in range(0, h.shape[1], MXU_DIM):
        cols = slice(c0, c0 + MXU_DIM)
        pieces.append(((h[:, cols] * r) * gs[:, cols] + shift[:, cols]).astype(BF16))
    return jnp.concatenate(pieces, axis=1)


def _mod_slices(mod_ref, base, n):
    m = mod_ref[0]
    return [m[:, (base + j) * D_MODEL:(base + j + 1) * D_MODEL] for j in range(n)]


def _mod_kernel(c_ref, w_ref, b_ref, o_ref):
    c = c_ref[...]
    s = c * jax.nn.sigmoid(c)
    s_hi = s.astype(BF16)
    s_lo = (s - s_hi.astype(F32)).astype(BF16)
    w = w_ref[0]
    w_hi = w.astype(BF16)
    w_lo = (w - w_hi.astype(F32)).astype(BF16)
    acc = _dot(s_hi, w_hi) + (_dot(s_lo, w_hi) + _dot(s_hi, w_lo))
    o_ref[0] = acc + b_ref[0]


def _cast_kernel(x_ref, o_ref):
    o_ref[...] = x_ref[...].astype(o_ref.dtype)


def _to_bf16(w):
    cols = w.shape[-1]
    flat = w.reshape(-1, cols)
    n_rows = flat.shape[0]
    assert n_rows % CAST_ROWS == 0 and cols % LANES == 0
    out = pl.pallas_call(
        _cast_kernel,
        grid=(n_rows // CAST_ROWS,),
        in_specs=[pl.BlockSpec((CAST_ROWS, cols), lambda i: (i, 0))],
        out_specs=pl.BlockSpec((CAST_ROWS, cols), lambda i: (i, 0)),
        out_shape=jax.ShapeDtypeStruct((n_rows, cols), BF16),
        compiler_params=_cparams(("arbitrary",)),
        name="weights_to_bf16",
    )(flat)
    return out.reshape(w.shape)


def _mod_call(cond, mod_w, mod_b):
    rows = cond.shape[0]
    tn = D_MODEL
    n_out = N_MOD * D_MODEL
    return pl.pallas_call(
        _mod_kernel,
        grid=(DEPTH, n_out // tn),
        in_specs=[
            pl.BlockSpec((rows, D_MODEL), lambda l, n: (0, 0)),
            pl.BlockSpec((1, D_MODEL, tn), lambda l, n: (l, 0, n)),
            pl.BlockSpec((1, 1, tn), lambda l, n: (l, 0, n)),
        ],
        out_specs=pl.BlockSpec((1, rows, tn), lambda l, n: (l, 0, n)),
        out_shape=jax.ShapeDtypeStruct((DEPTH, rows, n_out), F32),
        compiler_params=_cparams(("arbitrary", "arbitrary")),
        name="mod_vectors",
    )(cond, mod_w, mod_b.reshape(DEPTH, 1, n_out))


class _Rows:
    def __init__(self, batch, seq, ctx_len):
        self.batch, self.seq, self.ctx_len = batch, seq, ctx_len
        self.n_lat = batch * seq
        self.n_ctx = batch * ctx_len
        self.n_tok = self.n_lat + self.n_ctx
        assert seq % TM == 0 and self.n_ctx % TM == 0
        self.lat_tiles = self.n_lat // TM
        self.tiles_per_seq = seq // TM
        self.all_tiles = self.n_tok // TM

    def mod_spec(self):
        lat_tiles, tps, ctx_row = self.lat_tiles, self.tiles_per_seq, self.batch
        return pl.BlockSpec(
            (1, 1, N_MOD * D_MODEL),
            lambda i: (jnp.where(i < lat_tiles, i // tps, ctx_row), 0, 0))


def _tile_spec(width, col=0):
    return pl.BlockSpec((TM, width), lambda i: (i, col))


def _hgrn_readout(of_ref, ob_ref, gate_ref, gn_ref, bd_ref, w_ref):
    o = of_ref[...] + ob_ref[...]
    ss = _dot((o * o).astype(BF16), bd_ref[...])
    on = o * lax.rsqrt(ss * (1.0 / HG_DV) + RMS_EPS) * gn_ref[...]
    gt = gate_ref[...]
    return _dot((on * (gt * jax.nn.sigmoid(gt))).astype(BF16), w_ref[...])


def _pick_tile(lat_ref, ctx_ref, scr_ref, lat_tiles):
    @pl.when(pl.program_id(0) < lat_tiles)
    def _():
        scr_ref[...] = lat_ref[...]

    @pl.when(pl.program_id(0) >= lat_tiles)
    def _():
        scr_ref[...] = ctx_ref[...]

    return scr_ref[...]


def _ffn_kernel(*refs, base, final, mixer, lat_tiles, split_h, split_attn):
    rest = list(refs)
    attn_scr = rest.pop() if split_attn else None
    h_scr = rest.pop() if split_h else None
    o_ref = rest.pop()
    h = _pick_tile(rest.pop(0), rest.pop(0), h_scr, lat_tiles) if split_h else rest.pop(0)[...]
    mod_ref, g_ref = rest.pop(0), rest.pop(0)
    if mixer is not None:
        (res_gate,) = _mod_slices(mod_ref, 5, 1)
        if mixer == "hgrn":
            y = _hgrn_readout(*rest[:6])
            rest = rest[6:]
        else:
            attn = (_pick_tile(rest.pop(0), rest.pop(0), attn_scr, lat_tiles) if split_attn
                    else rest.pop(0)[...])
            y = _dot(attn, rest.pop(0)[...])
        h = h + res_gate * y
    wg_ref, wu_ref, wd_ref = rest[:3]
    shift, scale, gate = _mod_slices(mod_ref, base, 3)
    a = _rms_modulate(h, g_ref[...], shift, scale)
    acc = jnp.zeros((TM, D_MODEL), F32)
    for j in range(FFN_HIDDEN // FFN_CHUNK):
        sl = slice(j * FFN_CHUNK, (j + 1) * FFN_CHUNK)
        gg = _dot(a, wg_ref[:, sl])
        uu = _dot(a, wu_ref[:, sl])
        hm = (gg * jax.nn.sigmoid(gg) * uu).astype(BF16)
        acc = acc + _dot(hm, wd_ref[sl, :])
    out = h + (0.5 * gate) * acc
    if final:
        fg_ref = rest[3]
        ms = jnp.mean(out * out, axis=-1, keepdims=True)
        out = out * lax.rsqrt(ms + RMS_EPS) * fg_ref[...]
    o_ref[...] = out


def _ffn_call(rows, hs, mod_l, g, ffn_w, layer, half, base, n_tiles, final_g=None, mixer=None, mixer_args=()):
    final = final_g is not None
    split_h = isinstance(hs, tuple)
    split_attn = mixer == "mla" and mixer_args[1] is not None
    lat_tiles = rows.lat_tiles

    def stacked(shape):
        return pl.BlockSpec((None, None) + shape, lambda i: (layer, half, 0, 0), pipeline_mode=pl.Buffered(1))

    def split_specs(width):
        return [pl.BlockSpec((TM, width), lambda i: (jnp.minimum(i, lat_tiles - 1), 0)),
                pl.BlockSpec((TM, width), lambda i: (jnp.maximum(i - lat_tiles, 0), 0))]

    in_specs, args = (split_specs(D_MODEL), list(hs)) if split_h else ([_tile_spec(D_MODEL)], [hs])
    in_specs += [rows.mod_spec(), _resident((1, D_MODEL))]
    args += [mod_l, g]
    if mixer == "hgrn":
        in_specs += [_tile_spec(HG_HV), _tile_spec(HG_HV), _tile_spec(D_MODEL, col=HG_GATE),
                     _resident((1, HG_HV)), _resident((HG_HV, HG_HV)), _resident((HG_HV, D_MODEL))]
        args += list(mixer_args)
    elif mixer == "mla":
        attn_lat, attn_ctx, w_o = mixer_args
        v_w = MLA_HEADS * MLA_V
        in_specs += (split_specs(v_w) if split_attn else [_tile_spec(v_w)]) + [_resident((v_w, D_MODEL))]
        args += ([attn_lat, attn_ctx] if split_attn else [attn_lat]) + [w_o]
    in_specs += [stacked((D_MODEL, FFN_HIDDEN)), stacked((D_MODEL, FFN_HIDDEN)), stacked((FFN_HIDDEN, D_MODEL))]
    args += list(ffn_w)
    if final:
        in_specs.append(_resident((1, D_MODEL)))
        args.append(final_g)
    scratch = ([pltpu.VMEM((TM, D_MODEL), F32)] if split_h else []) + (
        [pltpu.VMEM((TM, MLA_HEADS * MLA_V), BF16)] if split_attn else [])
    return pl.pallas_call(
        functools.partial(_ffn_kernel, base=base, final=final, mixer=mixer, lat_tiles=lat_tiles,
                          split_h=split_h, split_attn=split_attn),
        grid=(n_tiles,),
        in_specs=in_specs,
        out_specs=_tile_spec(D_MODEL),
        out_shape=jax.ShapeDtypeStruct((n_tiles * TM, D_MODEL), F32),
        scratch_shapes=scratch,
        compiler_params=_cparams(("arbitrary",)),
        name="ffn_half",
    )(*args)


HG_Q, HG_KF, HG_KB = range(3)
HG_LGF, HG_LGB, HG_GATE = range(3)
HG_ZF, HG_ZB, HG_I, HG_G = 1, 2, 3, 4


def _hgrn_in_kernel(h_ref, mod_ref, g_ref, w_ref, wvt_ref, lb_ref, pb_ref, pf_ref, vt_ref):
    shift, scale = _mod_slices(mod_ref, 3, 2)
    a = _rms_modulate(h_ref[...], g_ref[...], shift, scale)

    for piece in range(D_MODEL // MXU_DIM):
        def proj(j):
            return _dot(a, w_ref[:, pl.ds(j * D_MODEL + piece * MXU_DIM, MXU_DIM)])

        def put(ref, blk, val):
            ref[:, pl.ds(blk * D_MODEL + piece * MXU_DIM, MXU_DIM)] = val.astype(ref.dtype)

        put(pb_ref, HG_Q, proj(0))
        vt_rows = pl.ds(piece * MXU_DIM, MXU_DIM)
        vt_ref[vt_rows, :] = _dot_nt(wvt_ref[vt_rows, :], a).astype(BF16)
        for d, (k_blk, lg_blk) in enumerate(((HG_KF, HG_LGF), (HG_KB, HG_LGB))):
            lb = lb_ref[d:d + 1, pl.ds(piece * MXU_DIM, MXU_DIM)]
            f = lb + (1.0 - lb) * jax.nn.sigmoid(proj(HG_ZF + d))
            put(pb_ref, k_blk, 1.0 - f)
            put(pf_ref, lg_blk, jnp.log2(jnp.maximum(f, F_MIN)))
        put(pf_ref, HG_GATE, proj(HG_G))


def _hgrn_in_call(rows, hs, mod_l, g, w_in, w_vt, lb):
    return pl.pallas_call(
        _hgrn_in_kernel,
        grid=(rows.all_tiles,),
        in_specs=[_tile_spec(D_MODEL), rows.mod_spec(), _resident((1, D_MODEL)),
                  _resident((D_MODEL, HG_IN)), _resident((HG_HV, D_MODEL)), _resident((2, HG_HK))],
        out_specs=[_tile_spec(3 * D_MODEL), _tile_spec(3 * D_MODEL), pl.BlockSpec((HG_HV, TM), lambda i: (0, i))],
        out_shape=[jax.ShapeDtypeStruct((rows.n_tok, 3 * D_MODEL), BF16),
                   jax.ShapeDtypeStruct((rows.n_tok, 3 * D_MODEL), F32),
                   jax.ShapeDtypeStruct((HG_HV, rows.n_tok), BF16)],
        compiler_params=_cparams(("arbitrary",)),
        name="hgrn_in_proj",
    )(hs, mod_l, g, w_in, w_vt, lb)


def _mla_proj_kernel(h_ref, mod_ref, g_ref, wd_ref, qn_ref, kvn_ref, wq_ref, wqs_ref, wk_ref, wvt_ref,
                     cos_ref, sin_ref, q_ref, k_ref, vt_ref):
    shift, scale = _mod_slices(mod_ref, 3, 2)
    a = _rms_modulate(h_ref[...], g_ref[...], shift, scale)
    dp = _dot(a, wd_ref[...])
    cq = dp[:, :Q_LORA]
    ckv = dp[:, Q_LORA:Q_LORA + KV_LORA]
    kr_a = dp[:, Q_LORA + KV_LORA:Q_LORA + KV_LORA + LANES]
    kr_b = dp[:, Q_LORA + KV_LORA + LANES:]
    cqn = (cq * lax.rsqrt(jnp.mean(cq * cq, axis=-1, keepdims=True) + RMS_EPS) * qn_ref[...]).astype(BF16)
    ckvn = (ckv * lax.rsqrt(jnp.mean(ckv * ckv, axis=-1, keepdims=True) + RMS_EPS) * kvn_ref[...]).astype(BF16)
    cos = cos_ref[...]
    sin = sin_ref[...]
    cos_r, sin_r = cos[:, LANES:], sin[:, LANES:]
    q_scale = MLA_SCALE * LOG2_E
    qm = _dot(cqn, wq_ref[...])
    qs = _dot(cqn, wqs_ref[...])
    pieces = []
    for hh in range(MLA_HEADS):
        nope = qm[:, hh * MLA_HW:hh * MLA_HW + LANES] * q_scale
        rope = (qm[:, hh * MLA_HW + LANES:(hh + 1) * MLA_HW] * (cos_r * q_scale)
                + qs[:, hh * LANES:(hh + 1) * LANES] * (sin_r * q_scale))
        pieces += [nope.astype(BF16), rope.astype(BF16)]
    q_ref[...] = jnp.concatenate(pieces, axis=1)
    kr = (kr_a * cos_r + kr_b * sin_r).astype(BF16)
    kn = _dot(ckvn, wk_ref[...]).astype(BF16)
    pieces = []
    for hh in range(MLA_HEADS):
        pieces += [kn[:, hh * MLA_NOPE:(hh + 1) * MLA_NOPE], kr]
    k_ref[...] = jnp.concatenate(pieces, axis=1)
    vt_ref[...] = _dot_nt(wvt_ref[...], ckvn).astype(BF16)


def _mla_proj_call(rows, hs, mod_l, g, wd, qn, kvn, wq, wqs, wk, wvt, cos_t, sin_t):
    lat_tiles, tps = rows.lat_tiles, rows.tiles_per_seq
    rope_spec = pl.BlockSpec((TM, MLA_HW), lambda i: (jnp.where(i < lat_tiles, i % tps, tps), 0))
    qk_w = MLA_HEADS * MLA_HW
    v_w = MLA_HEADS * MLA_V
    return pl.pallas_call(
        _mla_proj_kernel,
        grid=(rows.all_tiles,),
        in_specs=[_tile_spec(D_MODEL), rows.mod_spec(), _resident((1, D_MODEL)),
                  _resident((D_MODEL, MLA_DOWN_P)), _resident((1, Q_LORA)), _resident((1, KV_LORA)),
                  _resident((Q_LORA, qk_w)), _resident((Q_LORA, MLA_HEADS * LANES)),
                  _resident((KV_LORA, MLA_HEADS * MLA_NOPE)), _resident((v_w, KV_LORA)),
                  rope_spec, rope_spec],
        out_specs=[_tile_spec(qk_w), _tile_spec(qk_w), pl.BlockSpec((v_w, TM), lambda i: (0, i))],
        out_shape=[jax.ShapeDtypeStruct((rows.n_tok, qk_w), BF16),
                   jax.ShapeDtypeStruct((rows.n_tok, qk_w), BF16),
                   jax.ShapeDtypeStruct((v_w, rows.n_tok), BF16)],
        compiler_params=_cparams(("arbitrary",)),
        name="mla_proj",
    )(hs, mod_l, g, wd, qn, kvn, wq, wqs, wk, wvt, cos_t, sin_t)


def _hgrn_cumsum(lg, tri):
    width = lg.shape[1]
    hi = lg.astype(BF16)
    lo = (lg - hi.astype(F32)).astype(BF16)
    c2 = _dot(tri, jnp.concatenate([hi, lo], axis=1))
    return c2[:, :width] + c2[:, width:]


def _block_diag(a, b):
    zero = jnp.zeros_like(a)
    return jnp.concatenate([jnp.concatenate([a, zero], axis=1), jnp.concatenate([zero, b], axis=1)], axis=0)


def _hgrn_pair(qb, kb, cum, vt, st_a_ref, st_b_ref, rev):
    c = HG_C
    pair_w = 2 * LANES
    row = lax.broadcasted_iota(jnp.int32, (c, pair_w), 0)
    col = lax.broadcasted_iota(jnp.int32, (c, pair_w), 1) & (LANES - 1)
    x = row ^ col
    causal = (row <= col) if rev else (row >= col)

    def halves(m):
        return m[:, :LANES], m[:, LANES:]

    def pair_scores(w_q, w_k):
        return _dot_nt(qb * w_q.astype(BF16), _block_diag(*halves(kb * w_k.astype(BF16))))

    edge = cum[0:1, :] if rev else cum[c - 1:c, :]
    st_a, st_b = st_a_ref[...], st_b_ref[...]
    o = _dot_nt(qb * jnp.exp2(cum).astype(BF16), _block_diag(st_a.astype(BF16), st_b.astype(BF16)))
    upd = _dot(vt, kb * jnp.exp2(edge - cum).astype(BF16))
    dec_a, dec_b = halves(jnp.exp2(edge))
    st_a_ref[...] = st_a * dec_a + upd[:LANES, :LANES]
    st_b_ref[...] = st_b * dec_b + upd[LANES:, LANES:]

    mid_row = SUBLANES // 2 if rev else SUBLANES // 2 - 1
    ref8 = jnp.concatenate(
        [jnp.broadcast_to(cum[b * SUBLANES + mid_row:b * SUBLANES + mid_row + 1, :], (SUBLANES, pair_w))
         for b in range(c // SUBLANES)], axis=0)
    d8 = cum - ref8
    a = pair_scores(jnp.exp2(d8), jnp.exp2(-d8))
    lvl = 3
    while (1 << lvl) < c:
        half = 1 << lvl
        parts = []
        for b in range(c // (2 * half)):
            lo = cum[2 * b * half:(2 * b + 1) * half, :]
            hi = cum[(2 * b + 1) * half:(2 * b + 2) * half, :]
            if rev:
                ref = hi[0:1, :]
                parts += [lo - ref, ref - hi]
            else:
                ref = lo[half - 1:half, :]
                parts += [ref - lo, hi - ref]
        w = jnp.exp2(jnp.concatenate(parts, axis=0))
        a = jnp.where(x < half, a, pair_scores(w, w))
        lvl += 1
    a = jnp.where(causal, a, 0.0)
    return o + _dot_nt(a.astype(BF16), _block_diag(vt[:LANES, :], vt[LANES:, :]))


def _hgrn_scan_kernel(qf_ref, kf_ref, vtf_ref, lgf_ref, qb_ref, kb_ref, vtb_ref, lgb_ref, tri_ref,
                      of_ref, ob_ref, st_ref):
    @pl.when(pl.program_id(1) == 0)
    def _():
        st_ref[...] = jnp.zeros_like(st_ref)

    for sub in range(HG_SUB):
        rf = slice(sub * HG_C, (sub + 1) * HG_C)
        rb = slice((HG_SUB - 1 - sub) * HG_C, (HG_SUB - sub) * HG_C)
        cum_f = _hgrn_cumsum(lgf_ref[rf, :], tri_ref[0])
        cum_b = _hgrn_cumsum(lgb_ref[rb, :], tri_ref[1])
        for p in range(HG_HEADS // 2):
            cols = slice(p * 2 * LANES, (p + 1) * 2 * LANES)
            of_ref[rf, cols] = _hgrn_pair(qf_ref[rf, cols], kf_ref[rf, cols], cum_f[:, cols], vtf_ref[cols, rf],
                                          st_ref.at[0, 2 * p], st_ref.at[0, 2 * p + 1], False)
            ob_ref[rb, cols] = _hgrn_pair(qb_ref[rb, cols], kb_ref[rb, cols], cum_b[:, cols], vtb_ref[cols, rb],
                                          st_ref.at[1, 2 * p], st_ref.at[1, 2 * p + 1], True)


def _hgrn_scan_call(rows, pb, pf, vt, tri):
    step_rows = HG_SUB * HG_C
    assert rows.ctx_len % step_rows == 0 and rows.seq % step_rows == 0
    c_chunks = rows.ctx_len // step_rows
    l_chunks = rows.seq // step_rows
    steps = c_chunks + l_chunks
    ctx0 = rows.n_lat // step_rows

    def fwd_blk(b, s):
        return jnp.where(s < c_chunks, ctx0 + b * c_chunks + s, b * l_chunks + s - c_chunks)

    def bwd_blk(b, s):
        return jnp.where(s < c_chunks, ctx0 + b * c_chunks + (c_chunks - 1 - s),
                         b * l_chunks + (l_chunks - 1 - (s - c_chunks)))

    def in_spec(blk, col):
        return pl.BlockSpec((step_rows, HG_HK), lambda b, s: (blk(b, s), col))

    def vt_spec(blk):
        return pl.BlockSpec((HG_HV, step_rows), lambda b, s: (0, blk(b, s)))

    out_sd = jax.ShapeDtypeStruct((rows.n_tok, HG_HV), F32)
    return pl.pallas_call(
        _hgrn_scan_kernel,
        grid=(rows.batch, steps),
        in_specs=[in_spec(fwd_blk, HG_Q), in_spec(fwd_blk, HG_KF), vt_spec(fwd_blk), in_spec(fwd_blk, HG_LGF),
                  in_spec(bwd_blk, HG_Q), in_spec(bwd_blk, HG_KB), vt_spec(bwd_blk), in_spec(bwd_blk, HG_LGB),
                  pl.BlockSpec((2, HG_C, HG_C), lambda b, s: (0, 0, 0))],
        out_specs=[pl.BlockSpec((step_rows, HG_HV), lambda b, s: (fwd_blk(b, s), 0)),
                   pl.BlockSpec((step_rows, HG_HV), lambda b, s: (bwd_blk(b, s), 0))],
        out_shape=[out_sd, out_sd],
        scratch_shapes=[pltpu.VMEM((2, HG_HEADS, HG_DV, HG_DK), F32)],
        compiler_params=_cparams(("arbitrary", "arbitrary")),
        name="hgrn_scan",
    )(pb, pb, vt, pf, pb, pb, vt, pf, tri)


def _attn_scores(q, kc_ref, kl_ref=None):
    scores = [_dot_nt(kc_ref[...], q)]
    if kl_ref is not None:
        scores += [_dot_nt(kl_ref[pl.ds(j * KEY_CHUNK, KEY_CHUNK), :], q)
                   for j in range(kl_ref.shape[0] // KEY_CHUNK)]
    m = jnp.max(scores[0], axis=0, keepdims=True)
    for s in scores[1:]:
        m = jnp.maximum(m, jnp.max(s, axis=0, keepdims=True))
    return scores, m


def _attn_values(scores, m, vtc_ref, vtl_ref=None):
    den, acc = None, None
    for j, s in enumerate(scores):
        p = jnp.exp2(s - m)
        vt = vtc_ref[...] if j == 0 else vtl_ref[:, pl.ds((j - 1) * KEY_CHUNK, KEY_CHUNK)]
        d_j = jnp.sum(p, axis=0, keepdims=True)
        a_j = _dot(vt, p.astype(BF16))
        den, acc = (d_j, a_j) if den is None else (den + d_j, acc + a_j)
    return (acc / den).T


def _attn_kernel(*refs, with_ctx):
    if with_ctx:
        q_ref, qc_ref, kl_ref, kc_ref, vtl_ref, vtc_ref, o_ref, oc_ref = refs
    else:
        q_ref, kl_ref, kc_ref, vtl_ref, vtc_ref, o_ref = refs
    nxt = _attn_scores(q_ref[pl.ds(0, TQ), :], kc_ref, kl_ref)
    for j in range(Q_SUB):
        cur = nxt
        if j + 1 < Q_SUB:
            nxt = _attn_scores(q_ref[pl.ds((j + 1) * TQ, TQ), :], kc_ref, kl_ref)
        elif with_ctx:
            nxt = _attn_scores(qc_ref[...], kc_ref)
        o_ref[pl.ds(j * TQ, TQ), :] = _attn_values(*cur, vtc_ref, vtl_ref).astype(o_ref.dtype)
    if with_ctx:
        oc_ref[...] = _attn_values(*nxt, vtc_ref).astype(oc_ref.dtype)


def _attn_call(rows, q, k, vt, with_ctx):
    tq = Q_SUB * TQ
    assert rows.ctx_len == TQ and rows.seq % tq == 0
    n_q = rows.seq // tq
    ctx0 = rows.n_lat // TQ
    v_w = MLA_HEADS * MLA_V
    ctx_qk = pl.BlockSpec((TQ, MLA_HW), lambda b, h, qi: (ctx0 + b, h))
    ctx_vt = pl.BlockSpec((MLA_V, TQ), lambda b, h, qi: (h, ctx0 + b))
    lat_q = pl.BlockSpec((tq, MLA_HW), lambda b, h, qi: (b * n_q + qi, h))
    lat_k = pl.BlockSpec((rows.seq, MLA_HW), lambda b, h, qi: (b, h))
    lat_vt = pl.BlockSpec((MLA_V, rows.seq), lambda b, h, qi: (h, b))
    lat_o = pl.BlockSpec((tq, MLA_V), lambda b, h, qi: (b * n_q + qi, h))
    lat_sd = jax.ShapeDtypeStruct((rows.n_lat, v_w), BF16)
    if with_ctx:
        assert n_q == 1
        in_specs, args = [lat_q, ctx_qk, lat_k, ctx_qk, lat_vt, ctx_vt], (q, q, k, k, vt, vt)
        out_specs = [lat_o, pl.BlockSpec((TQ, MLA_V), lambda b, h, qi: (b, h))]
        out_shape = [lat_sd, jax.ShapeDtypeStruct((rows.n_ctx, v_w), BF16)]
    else:
        in_specs, args = [lat_q, lat_k, ctx_qk, lat_vt, ctx_vt], (q, k, k, vt, vt)
        out_specs, out_shape = lat_o, lat_sd
    out = pl.pallas_call(
        functools.partial(_attn_kernel, with_ctx=with_ctx),
        grid=(rows.batch, MLA_HEADS, n_q),
        in_specs=in_specs,
        out_specs=out_specs,
        out_shape=out_shape,
        compiler_params=_cparams(("arbitrary", "arbitrary", "arbitrary")),
        name="mla_attention",
    )(*args)
    return tuple(out) if with_ctx else (out, None)


def _rope_swap_cols():
    idx = np.arange(MLA_ROPE).reshape(2, 2, MLA_ROPE // 4)
    return idx[:, ::-1, :].reshape(-1)


def _mla_weights(w_down, w_uq, w_ukv):
    swap = _rope_swap_cols()
    kr = w_down[:, Q_LORA + KV_LORA:]
    zpad = jnp.zeros((D_MODEL, LANES - MLA_ROPE), w_down.dtype)
    wd = jnp.concatenate([w_down[:, :Q_LORA + KV_LORA], kr, zpad, kr[:, swap], zpad], axis=1)
    uq = w_uq.reshape(Q_LORA, MLA_HEADS, MLA_QK)
    nope, rope = uq[..., :MLA_NOPE], uq[..., MLA_NOPE:]
    z64 = jnp.zeros((Q_LORA, MLA_HEADS, MLA_HW - MLA_QK), w_uq.dtype)
    wq = jnp.concatenate([nope, rope, z64], axis=-1).reshape(Q_LORA, MLA_HEADS * MLA_HW)
    wqs = jnp.concatenate([rope[..., swap], z64], axis=-1).reshape(Q_LORA, MLA_HEADS * LANES)
    ukv = w_ukv.reshape(KV_LORA, MLA_HEADS, MLA_NOPE + MLA_V)
    wk = ukv[..., :MLA_NOPE].reshape(KV_LORA, MLA_HEADS * MLA_NOPE)
    wvt = ukv[..., MLA_NOPE:].reshape(KV_LORA, MLA_HEADS * MLA_V).T
    return wd.astype(BF16), wq.astype(BF16), wqs.astype(BF16), wk.astype(BF16), wvt.astype(BF16)


def _rope_tables(seq):
    n_rows = seq // GRID_W
    row = jnp.repeat(jnp.arange(n_rows), GRID_W).astype(F32)
    col = jnp.tile(jnp.arange(GRID_W), n_rows).astype(F32)
    axis_dims = MLA_ROPE // 2
    inv = 1.0 / (ROPE_BASE ** (jnp.arange(0, axis_dims, 2, dtype=F32) / axis_dims))
    ang_r, ang_c = row[:, None] * inv, col[:, None] * inv
    cos64 = jnp.concatenate([jnp.cos(ang_r)] * 2 + [jnp.cos(ang_c)] * 2, axis=1)
    sin64 = jnp.concatenate([-jnp.sin(ang_r), jnp.sin(ang_r), -jnp.sin(ang_c), jnp.sin(ang_c)], axis=1)
    cos64 = jnp.concatenate([cos64, jnp.ones((TM, MLA_ROPE), F32)], axis=0)
    sin64 = jnp.concatenate([sin64, jnp.zeros((TM, MLA_ROPE), F32)], axis=0)
    n = seq + TM
    cos_t = jnp.concatenate([jnp.ones((n, MLA_NOPE), F32), cos64, jnp.zeros((n, MLA_HW - MLA_QK), F32)], axis=1)
    sin_t = jnp.concatenate([jnp.zeros((n, MLA_NOPE), F32), sin64, jnp.zeros((n, MLA_HW - MLA_QK), F32)], axis=1)
    return cos_t, sin_t


def kernel(x, c, ctx, c_ctx, mod_w, mod_b, norm_g, ffn_w_gate, ffn_w_up, ffn_w_down, hg_w_in, hg_w_out,
           hg_gn, hg_lb_logits, mla_w_down, mla_q_norm, mla_w_uq, mla_kv_norm, mla_w_ukv, mla_w_o, final_g):
    batch, seq, _ = x.shape
    rows = _Rows(batch, seq, ctx.shape[1])
    hs = (x.reshape(rows.n_lat, D_MODEL), ctx.reshape(rows.n_ctx, D_MODEL))

    mod_rows = -(-(batch + 1) // SUBLANES) * SUBLANES
    cond = jnp.concatenate([c, c_ctx[None], jnp.zeros((mod_rows - batch - 1, D_MODEL), F32)], axis=0)
    mod = _mod_call(cond, mod_w, mod_b)

    p_lb = jax.nn.softmax(hg_lb_logits.astype(F32), axis=0)
    lbs = jnp.maximum(jnp.cumsum(p_lb, axis=0) - p_lb[0:1], 0.0)
    tri = jnp.stack([jnp.tril(jnp.ones((HG_C, HG_C), F32)), jnp.triu(jnp.ones((HG_C, HG_C), F32))]).astype(BF16)
    head_id = np.arange(HG_HV) // HG_DV
    bd = jnp.asarray(head_id[:, None] == head_id[None, :], BF16)
    cos_t, sin_t = _rope_tables(seq)

    ffn_w = (_to_bf16(ffn_w_gate), _to_bf16(ffn_w_up), _to_bf16(ffn_w_down))

    for i in range(DEPTH):
        last = i == DEPTH - 1
        j = i // 2
        mod_l = mod[i].reshape(mod_rows, 1, N_MOD * D_MODEL)
        g = norm_g[i].reshape(3, 1, D_MODEL)
        hs = _ffn_call(rows, hs, mod_l, g[0], ffn_w, i, 0, 0, rows.all_tiles)
        n_out = rows.lat_tiles if last else rows.all_tiles
        if i % 2 == 0:
            w_in = hg_w_in[j].astype(BF16)
            w_vt = hg_w_in[j, :, HG_I * D_MODEL:(HG_I + 1) * D_MODEL].T.astype(BF16)
            pb, pf, vt = _hgrn_in_call(rows, hs, mod_l, g[1], w_in, w_vt, lbs[j])
            o_f, o_b = _hgrn_scan_call(rows, pb, pf, vt, tri)
            gn = jnp.tile(hg_gn[j], HG_HEADS).reshape(1, HG_HV)
            mixer, mixer_args = "hgrn", (o_f, o_b, pf, gn, bd, hg_w_out[j].astype(BF16))
        else:
            w_dn, w_q, w_qs, w_k, w_vt = _mla_weights(mla_w_down[j], mla_w_uq[j], mla_w_ukv[j])
            q, k, vt = _mla_proj_call(rows, hs, mod_l, g[1], w_dn, mla_q_norm[j].reshape(1, Q_LORA),
                                      mla_kv_norm[j].reshape(1, KV_LORA), w_q, w_qs, w_k, w_vt, cos_t, sin_t)
            attn_lat, attn_ctx = _attn_call(rows, q, k, vt, with_ctx=not last)
            mixer, mixer_args = "mla", (attn_lat, attn_ctx, mla_w_o[j].astype(BF16))
        hs = _ffn_call(rows, hs, mod_l, g[2], ffn_w, i, 1, 6, n_out,
                       final_g=final_g.reshape(1, D_MODEL) if last else None, mixer=mixer, mixer_args=mixer_args)
    return hs.reshape(batch, seq, D_MODEL)
```

```python
import functools

import jax
import jax.numpy as jnp
import numpy as np
from jax import lax
from jax.experimental import pallas as pl
from jax.experimental.pallas import tpu as pltpu

F32 = jnp.float32
BF16 = jnp.bfloat16

D_MODEL = 1024
DEPTH = 4
GRID_W = 64
N_MOD = 9
FFN_HIDDEN = 2816
RMS_EPS = 1e-6
HG_HEADS = 8
HG_DK = 128
HG_DV = 128
HG_HK = HG_HEADS * HG_DK
HG_HV = HG_HEADS * HG_DV
HG_IN = 3 * HG_HK + HG_HV + D_MODEL
F_MIN = 1e-6
MLA_HEADS = 8
MLA_NOPE = 128
MLA_ROPE = 64
MLA_V = 128
Q_LORA = 384
KV_LORA = 256
MLA_QK = MLA_NOPE + MLA_ROPE
MLA_SCALE = MLA_QK ** -0.5
ROPE_BASE = 10000.0
LOG2_E = 1.4426950408889634

LANES = 128
SUBLANES = 8
MXU_DIM = 256
VMEM_LIMIT = 56 * 1024 * 1024

TM = 512
CAST_ROWS = 512
FFN_CHUNK = MXU_DIM
HG_C = 128
HG_SUB = 2
TQ = 256
Q_SUB = 16
KEY_CHUNK = 1024
MLA_HW = 2 * LANES
MLA_DOWN_P = Q_LORA + KV_LORA + 2 * LANES


def _cparams(sem):
    return pltpu.CompilerParams(dimension_semantics=sem, vmem_limit_bytes=VMEM_LIMIT)


def _resident(shape):
    nd = len(shape)
    return pl.BlockSpec(shape, lambda *_: (0,) * nd, pipeline_mode=pl.Buffered(1))


def _dot(a, b):
    return jnp.dot(a, b, preferred_element_type=F32)


def _dot_nt(a, b):
    return lax.dot_general(a, b, (((1,), (1,)), ((), ())), preferred_element_type=F32)


def _rms_modulate(h, g, shift, scale):
    r = lax.rsqrt(jnp.mean(h * h, axis=-1, keepdims=True) + RMS_EPS)
    gs = g * (1.0 + scale)
    pieces = []
    for c0 in range(0, h.shape[1], MXU_DIM):
        cols = slice(c0, c0 + MXU_DIM)
        pieces.append(((h[:, cols] * r) * gs[:, cols] + shift[:, cols]).astype(BF16))
    return jnp.concatenate(pieces, axis=1)


def _mod_slices(mod_ref, base, n):
    m = mod_ref[0]
    return [m[:, (base + j) * D_MODEL:(base + j + 1) * D_MODEL] for j in range(n)]


def _mod_kernel(c_ref, w_ref, b_ref, o_ref):
    c = c_ref[...]
    s = c * jax.nn.sigmoid(c)
    s_hi = s.astype(BF16)
    s_lo = (s - s_hi.astype(F32)).astype(BF16)
    w = w_ref[0]
    w_hi = w.astype(BF16)
    w_lo = (w - w_hi.astype(F32)).astype(BF16)
    acc = _dot(s_hi, w_hi) + (_dot(s_lo, w_hi) + _dot(s_hi, w_lo))
    o_ref[0] = acc + b_ref[0]


def _cast_kernel(x_ref, o_ref):
    o_ref[...] = x_ref[...].astype(o_ref.dtype)


def _to_bf16(w):
    cols = w.shape[-1]
    flat = w.reshape(-1, cols)
    n_rows = flat.shape[0]
    assert n_rows % CAST_ROWS == 0 and cols % LANES == 0
    out = pl.pallas_call(
        _cast_kernel,
        grid=(n_rows // CAST_ROWS,),
        in_specs=[pl.BlockSpec((CAST_ROWS, cols), lambda i: (i, 0))],
        out_specs=pl.BlockSpec((CAST_ROWS, cols), lambda i: (i, 0)),
        out_shape=jax.ShapeDtypeStruct((n_rows, cols), BF16),
        compiler_params=_cparams(("arbitrary",)),
        name="weights_to_bf16",
    )(flat)
    return out.reshape(w.shape)


def _mod_call(cond, mod_w, mod_b):
    rows = cond.shape[0]
    tn = D_MODEL
    n_out = N_MOD * D_MODEL
    return pl.pallas_call(
        _mod_kernel,
        grid=(DEPTH, n_out // tn),
        in_specs=[
            pl.BlockSpec((rows, D_MODEL), lambda l, n: (0, 0)),
            pl.BlockSpec((1, D_MODEL, tn), lambda l, n: (l, 0, n)),
            pl.BlockSpec((1, 1, tn), lambda l, n: (l, 0, n)),
        ],
        out_specs=pl.BlockSpec((1, rows, tn), lambda l, n: (l, 0, n)),
        out_shape=jax.ShapeDtypeStruct((DEPTH, rows, n_out), F32),
        compiler_params=_cparams(("arbitrary", "arbitrary")),
        name="mod_vectors",
    )(cond, mod_w, mod_b.reshape(DEPTH, 1, n_out))


class _Rows:
    def __init__(self, batch, seq, ctx_len):
        self.batch, self.seq, self.ctx_len = batch, seq, ctx_len
        self.n_lat = batch * seq
        self.n_ctx = batch * ctx_len
        self.n_tok = self.n_lat + self.n_ctx
        assert seq % TM == 0 and self.n_ctx % TM == 0
        self.lat_tiles = self.n_lat // TM
        self.tiles_per_seq = seq // TM
        self.all_tiles = self.n_tok // TM

    def mod_spec(self):
        lat_tiles, tps, ctx_row = self.lat_tiles, self.tiles_per_seq, self.batch
        return pl.BlockSpec(
            (1, 1, N_MOD * D_MODEL),
            lambda i: (jnp.where(i < lat_tiles, i // tps, ctx_row), 0, 0))


def _tile_spec(width, col=0):
    return pl.BlockSpec((TM, width), lambda i: (i, col))


def _hgrn_readout(of_ref, ob_ref, gate_ref, gn_ref, bd_ref, w_ref):
    o = of_ref[...] + ob_ref[...]
    ss = _dot((o * o).astype(BF16), bd_ref[...])
    on = o * lax.rsqrt(ss * (1.0 / HG_DV) + RMS_EPS) * gn_ref[...]
    gt = gate_ref[...]
    return _dot((on * (gt * jax.nn.sigmoid(gt))).astype(BF16), w_ref[...])


def _pick_tile(lat_ref, ctx_ref, scr_ref, lat_tiles):
    @pl.when(pl.program_id(0) < lat_tiles)
    def _():
        scr_ref[...] = lat_ref[...]

    @pl.when(pl.program_id(0) >= lat_tiles)
    def _():
        scr_ref[...] = ctx_ref[...]

    return scr_ref[...]


def _ffn_kernel(*refs, base, final, mixer, lat_tiles, split_h, split_attn):
    rest = list(refs)
    attn_scr = rest.pop() if split_attn else None
    h_scr = rest.pop() if split_h else None
    o_ref = rest.pop()
    h = _pick_tile(rest.pop(0), rest.pop(0), h_scr, lat_tiles) if split_h else rest.pop(0)[...]
    mod_ref, g_ref = rest.pop(0), rest.pop(0)
    if mixer is not None:
        (res_gate,) = _mod_slices(mod_ref, 5, 1)
        if mixer == "hgrn":
            y = _hgrn_readout(*rest[:6])
            rest = rest[6:]
        else:
            attn = (_pick_tile(rest.pop(0), rest.pop(0), attn_scr, lat_tiles) if split_attn
                    else rest.pop(0)[...])
            y = _dot(attn, rest.pop(0)[...])
        h = h + res_gate * y
    wg_ref, wu_ref, wd_ref = rest[:3]
    shift, scale, gate = _mod_slices(mod_ref, base, 3)
    a = _rms_modulate(h, g_ref[...], shift, scale)
    acc = jnp.zeros((TM, D_MODEL), F32)
    for j in range(FFN_HIDDEN // FFN_CHUNK):
        sl = slice(j * FFN_CHUNK, (j + 1) * FFN_CHUNK)
        gg = _dot(a, wg_ref[:, sl])
        uu = _dot(a, wu_ref[:, sl])
        hm = (gg * jax.nn.sigmoid(gg) * uu).astype(BF16)
        acc = acc + _dot(hm, wd_ref[sl, :])
    out = h + (0.5 * gate) * acc
    if final:
        fg_ref = rest[3]
        ms = jnp.mean(out * out, axis=-1, keepdims=True)
        out = out * lax.rsqrt(ms + RMS_EPS) * fg_ref[...]
    o_ref[...] = out


def _ffn_call(rows, hs, mod_l, g, ffn_w, layer, half, base, n_tiles, final_g=None, mixer=None, mixer_args=()):
    final = final_g is not None
    split_h = isinstance(hs, tuple)
    split_attn = mixer == "mla" and mixer_args[1] is not None
    lat_tiles = rows.lat_tiles

    def stacked(shape):
        return pl.BlockSpec((None, None) + shape, lambda i: (layer, half, 0, 0), pipeline_mode=pl.Buffered(1))

    def split_specs(width):
        return [pl.BlockSpec((TM, width), lambda i: (jnp.minimum(i, lat_tiles - 1), 0)),
                pl.BlockSpec((TM, width), lambda i: (jnp.maximum(i - lat_tiles, 0), 0))]

    in_specs, args = (split_specs(D_MODEL), list(hs)) if split_h else ([_tile_spec(D_MODEL)], [hs])
    in_specs += [rows.mod_spec(), _resident((1, D_MODEL))]
    args += [mod_l, g]
    if mixer == "hgrn":
        in_specs += [_tile_spec(HG_HV), _tile_spec(HG_HV), _tile_spec(D_MODEL, col=HG_GATE),
                     _resident((1, HG_HV)), _resident((HG_HV, HG_HV)), _resident((HG_HV, D_MODEL))]
        args += list(mixer_args)
    elif mixer == "mla":
        attn_lat, attn_ctx, w_o = mixer_args
        v_w = MLA_HEADS * MLA_V
        in_specs += (split_specs(v_w) if split_attn else [_tile_spec(v_w)]) + [_resident((v_w, D_MODEL))]
        args += ([attn_lat, attn_ctx] if split_attn else [attn_lat]) + [w_o]
    in_specs += [stacked((D_MODEL, FFN_HIDDEN)), stacked((D_MODEL, FFN_HIDDEN)), stacked((FFN_HIDDEN, D_MODEL))]
    args += list(ffn_w)
    if final:
        in_specs.append(_resident((1, D_MODEL)))
        args.append(final_g)
    scratch = ([pltpu.VMEM((TM, D_MODEL), F32)] if split_h else []) + (
        [pltpu.VMEM((TM, MLA_HEADS * MLA_V), BF16)] if split_attn else [])
    return pl.pallas_call(
        functools.partial(_ffn_kernel, base=base, final=final, mixer=mixer, lat_tiles=lat_tiles,
                          split_h=split_h, split_attn=split_attn),
        grid=(n_tiles,),
        in_specs=in_specs,
        out_specs=_tile_spec(D_MODEL),
        out_shape=jax.ShapeDtypeStruct((n_tiles * TM, D_MODEL), F32),
        scratch_shapes=scratch,
        compiler_params=_cparams(("arbitrary",)),
        name="ffn_half",
    )(*args)


HG_Q, HG_KF, HG_KB = range(3)
HG_LGF, HG_LGB, HG_GATE = range(3)
HG_ZF, HG_ZB, HG_I, HG_G = 1, 2, 3, 4


def _hgrn_in_kernel(h_ref, mod_ref, g_ref, w_ref, wvt_ref, lb_ref, pb_ref, pf_ref, vt_ref):
    shift, scale = _mod_slices(mod_ref, 3, 2)
    a = _rms_modulate(h_ref[...], g_ref[...], shift, scale)

    for piece in range(D_MODEL // MXU_DIM):
        def proj(j):
            return _dot(a, w_ref[:, pl.ds(j * D_MODEL + piece * MXU_DIM, MXU_DIM)])

        def put(ref, blk, val):
            ref[:, pl.ds(blk * D_MODEL + piece * MXU_DIM, MXU_DIM)] = val.astype(ref.dtype)

        put(pb_ref, HG_Q, proj(0))
        vt_rows = pl.ds(piece * MXU_DIM, MXU_DIM)
        vt_ref[vt_rows, :] = _dot_nt(wvt_ref[vt_rows, :], a).astype(BF16)
        for d, (k_blk, lg_blk) in enumerate(((HG_KF, HG_LGF), (HG_KB, HG_LGB))):
            lb = lb_ref[d:d + 1, pl.ds(piece * MXU_DIM, MXU_DIM)]
            f = lb + (1.0 - lb) * jax.nn.sigmoid(proj(HG_ZF + d))
            put(pb_ref, k_blk, 1.0 - f)
            put(pf_ref, lg_blk, jnp.log2(jnp.maximum(f, F_MIN)))
        put(pf_ref, HG_GATE, proj(HG_G))


def _transpose_cast_kernel(x_ref, o_ref):
    o_ref[...] = x_ref[...].T.astype(o_ref.dtype)


def _hgrn_value_weights_t(hg_w_in, layer):
    return pl.pallas_call(
        _transpose_cast_kernel,
        grid=(1,),
        in_specs=[pl.BlockSpec((None, D_MODEL, HG_HV), lambda i: (layer, 0, HG_I))],
        out_specs=pl.BlockSpec((HG_HV, D_MODEL), lambda i: (0, 0)),
        out_shape=jax.ShapeDtypeStruct((HG_HV, D_MODEL), BF16),
        compiler_params=_cparams(("arbitrary",)),
        name="hgrn_value_weights_t",
    )(hg_w_in)


def _hgrn_in_call(rows, hs, mod_l, g, w_in, w_vt, lb):
    return pl.pallas_call(
        _hgrn_in_kernel,
        grid=(rows.all_tiles,),
        in_specs=[_tile_spec(D_MODEL), rows.mod_spec(), _resident((1, D_MODEL)),
                  _resident((D_MODEL, HG_IN)), _resident((HG_HV, D_MODEL)), _resident((2, HG_HK))],
        out_specs=[_tile_spec(3 * D_MODEL), _tile_spec(3 * D_MODEL), pl.BlockSpec((HG_HV, TM), lambda i: (0, i))],
        out_shape=[jax.ShapeDtypeStruct((rows.n_tok, 3 * D_MODEL), BF16),
                   jax.ShapeDtypeStruct((rows.n_tok, 3 * D_MODEL), F32),
                   jax.ShapeDtypeStruct((HG_HV, rows.n_tok), BF16)],
        compiler_params=_cparams(("arbitrary",)),
        name="hgrn_in_proj",
    )(hs, mod_l, g, w_in, w_vt, lb)


def _mla_proj_kernel(h_ref, mod_ref, g_ref, wd_ref, qn_ref, kvn_ref, wq_ref, wqs_ref, wk_ref, wvt_ref,
                     cos_ref, sin_ref, q_ref, k_ref, vt_ref):
    shift, scale = _mod_slices(mod_ref, 3, 2)
    a = _rms_modulate(h_ref[...], g_ref[...], shift, scale)
    dp = _dot(a, wd_ref[...])
    cq = dp[:, :Q_LORA]
    ckv = dp[:, Q_LORA:Q_LORA + KV_LORA]
    kr_a = dp[:, Q_LORA + KV_LORA:Q_LORA + KV_LORA + LANES]
    kr_b = dp[:, Q_LORA + KV_LORA + LANES:]
    cqn = (cq * lax.rsqrt(jnp.mean(cq * cq, axis=-1, keepdims=True) + RMS_EPS) * qn_ref[...]).astype(BF16)
    ckvn = (ckv * lax.rsqrt(jnp.mean(ckv * ckv, axis=-1, keepdims=True) + RMS_EPS) * kvn_ref[...]).astype(BF16)
    cos = cos_ref[...]
    sin = sin_ref[...]
    cos_r, sin_r = cos[:, LANES:], sin[:, LANES:]
    q_scale = MLA_SCALE * LOG2_E
    qm = _dot(cqn, wq_ref[...])
    qs = _dot(cqn, wqs_ref[...])
    pieces = []
    for hh in range(MLA_HEADS):
        nope = qm[:, hh * MLA_HW:hh * MLA_HW + LANES] * q_scale
        rope = (qm[:, hh * MLA_HW + LANES:(hh + 1) * MLA_HW] * (cos_r * q_scale)
                + qs[:, hh * LANES:(hh + 1) * LANES] * (sin_r * q_scale))
        pieces += [nope.astype(BF16), rope.astype(BF16)]
    q_ref[...] = jnp.concatenate(pieces, axis=1)
    kr = (kr_a * cos_r + kr_b * sin_r).astype(BF16)
    kn = _dot(ckvn, wk_ref[...]).astype(BF16)
    pieces = []
    for hh in range(MLA_HEADS):
        pieces += [kn[:, hh * MLA_NOPE:(hh + 1) * MLA_NOPE], kr]
    k_ref[...] = jnp.concatenate(pieces, axis=1)
    vt_ref[...] = _dot_nt(wvt_ref[...], ckvn).astype(BF16)


def _mla_proj_call(rows, hs, mod_l, g, wd, qn, kvn, wq, wqs, wk, wvt, cos_t, sin_t):
    lat_tiles, tps = rows.lat_tiles, rows.tiles_per_seq
    rope_spec = pl.BlockSpec((TM, MLA_HW), lambda i: (jnp.where(i < lat_tiles, i % tps, tps), 0))
    qk_w = MLA_HEADS * MLA_HW
    v_w = MLA_HEADS * MLA_V
    return pl.pallas_call(
        _mla_proj_kernel,
        grid=(rows.all_tiles,),
        in_specs=[_tile_spec(D_MODEL), rows.mod_spec(), _resident((1, D_MODEL)),
                  _resident((D_MODEL, MLA_DOWN_P)), _resident((1, Q_LORA)), _resident((1, KV_LORA)),
                  _resident((Q_LORA, qk_w)), _resident((Q_LORA, MLA_HEADS * LANES)),
                  _resident((KV_LORA, MLA_HEADS * MLA_NOPE)), _resident((v_w, KV_LORA)),
                  rope_spec, rope_spec],
        out_specs=[_tile_spec(qk_w), _tile_spec(qk_w), pl.BlockSpec((v_w, TM), lambda i: (0, i))],
        out_shape=[jax.ShapeDtypeStruct((rows.n_tok, qk_w), BF16),
                   jax.ShapeDtypeStruct((rows.n_tok, qk_w), BF16),
                   jax.ShapeDtypeStruct((v_w, rows.n_tok), BF16)],
        compiler_params=_cparams(("arbitrary",)),
        name="mla_proj",
    )(hs, mod_l, g, wd, qn, kvn, wq, wqs, wk, wvt, cos_t, sin_t)


def _hgrn_cumsum(lg, tri):
    width = lg.shape[1]
    hi = lg.astype(BF16)
    lo = (lg - hi.astype(F32)).astype(BF16)
    c2 = _dot(tri, jnp.concatenate([hi, lo], axis=1))
    return c2[:, :width] + c2[:, width:]


def _block_diag(a, b):
    zero = jnp.zeros_like(a)
    return jnp.concatenate([jnp.concatenate([a, zero], axis=1), jnp.concatenate([zero, b], axis=1)], axis=0)


def _hgrn_pair(qb, kb, cum, vt, st_a_ref, st_b_ref, rev):
    c = HG_C
    pair_w = 2 * LANES
    row = lax.broadcasted_iota(jnp.int32, (c, pair_w), 0)
    col = lax.broadcasted_iota(jnp.int32, (c, pair_w), 1) & (LANES - 1)
    x = row ^ col
    causal = (row <= col) if rev else (row >= col)

    def halves(m):
        return m[:, :LANES], m[:, LANES:]

    def pair_scores(w_q, w_k):
        return _dot_nt(qb * w_q.astype(BF16), _block_diag(*halves(kb * w_k.astype(BF16))))

    edge = cum[0:1, :] if rev else cum[c - 1:c, :]
    st_a, st_b = st_a_ref[...], st_b_ref[...]
    o = _dot_nt(qb * jnp.exp2(cum).astype(BF16), _block_diag(st_a.astype(BF16), st_b.astype(BF16)))
    upd = _dot(vt, kb * jnp.exp2(edge - cum).astype(BF16))
    dec_a, dec_b = halves(jnp.exp2(edge))
    st_a_ref[...] = st_a * dec_a + upd[:LANES, :LANES]
    st_b_ref[...] = st_b * dec_b + upd[LANES:, LANES:]

    mid_row = SUBLANES // 2 if rev else SUBLANES // 2 - 1
    ref8 = jnp.concatenate(
        [jnp.broadcast_to(cum[b * SUBLANES + mid_row:b * SUBLANES + mid_row + 1, :], (SUBLANES, pair_w))
         for b in range(c // SUBLANES)], axis=0)
    d8 = cum - ref8
    a = pair_scores(jnp.exp2(d8), jnp.exp2(-d8))
    lvl = 3
    while (1 << lvl) < c:
        half = 1 << lvl
        parts = []
        for b in range(c // (2 * half)):
            lo = cum[2 * b * half:(2 * b + 1) * half, :]
            hi = cum[(2 * b + 1) * half:(2 * b + 2) * half, :]
            if rev:
                ref = hi[0:1, :]
                parts += [lo - ref, ref - hi]
            else:
                ref = lo[half - 1:half, :]
                parts += [ref - lo, hi - ref]
        w = jnp.exp2(jnp.concatenate(parts, axis=0))
        a = jnp.where(x < half, a, pair_scores(w, w))
        lvl += 1
    a = jnp.where(causal, a, 0.0)
    return o + _dot_nt(a.astype(BF16), _block_diag(vt[:LANES, :], vt[LANES:, :]))


def _hgrn_scan_kernel(qf_ref, kf_ref, vtf_ref, lgf_ref, qb_ref, kb_ref, vtb_ref, lgb_ref, tri_ref,
                      of_ref, ob_ref, st_ref):
    @pl.when(pl.program_id(1) == 0)
    def _():
        st_ref[...] = jnp.zeros_like(st_ref)

    for sub in range(HG_SUB):
        rf = slice(sub * HG_C, (sub + 1) * HG_C)
        rb = slice((HG_SUB - 1 - sub) * HG_C, (HG_SUB - sub) * HG_C)
        cum_f = _hgrn_cumsum(lgf_ref[rf, :], tri_ref[0])
        cum_b = _hgrn_cumsum(lgb_ref[rb, :], tri_ref[1])
        for p in range(HG_HEADS // 2):
            cols = slice(p * 2 * LANES, (p + 1) * 2 * LANES)
            of_ref[rf, cols] = _hgrn_pair(qf_ref[rf, cols], kf_ref[rf, cols], cum_f[:, cols], vtf_ref[cols, rf],
                                          st_ref.at[0, 2 * p], st_ref.at[0, 2 * p + 1], False)
            ob_ref[rb, cols] = _hgrn_pair(qb_ref[rb, cols], kb_ref[rb, cols], cum_b[:, cols], vtb_ref[cols, rb],
                                          st_ref.at[1, 2 * p], st_ref.at[1, 2 * p + 1], True)


def _hgrn_scan_call(rows, pb, pf, vt, tri):
    step_rows = HG_SUB * HG_C
    assert rows.ctx_len % step_rows == 0 and rows.seq % step_rows == 0
    c_chunks = rows.ctx_len // step_rows
    l_chunks = rows.seq // step_rows
    steps = c_chunks + l_chunks
    ctx0 = rows.n_lat // step_rows

    def fwd_blk(b, s):
        return jnp.where(s < c_chunks, ctx0 + b * c_chunks + s, b * l_chunks + s - c_chunks)

    def bwd_blk(b, s):
        return jnp.where(s < c_chunks, ctx0 + b * c_chunks + (c_chunks - 1 - s),
                         b * l_chunks + (l_chunks - 1 - (s - c_chunks)))

    def in_spec(blk, col):
        return pl.BlockSpec((step_rows, HG_HK), lambda b, s: (blk(b, s), col))

    def vt_spec(blk):
        return pl.BlockSpec((HG_HV, step_rows), lambda b, s: (0, blk(b, s)))

    out_sd = jax.ShapeDtypeStruct((rows.n_tok, HG_HV), F32)
    return pl.pallas_call(
        _hgrn_scan_kernel,
        grid=(rows.batch, steps),
        in_specs=[in_spec(fwd_blk, HG_Q), in_spec(fwd_blk, HG_KF), vt_spec(fwd_blk), in_spec(fwd_blk, HG_LGF),
                  in_spec(bwd_blk, HG_Q), in_spec(bwd_blk, HG_KB), vt_spec(bwd_blk), in_spec(bwd_blk, HG_LGB),
                  pl.BlockSpec((2, HG_C, HG_C), lambda b, s: (0, 0, 0))],
        out_specs=[pl.BlockSpec((step_rows, HG_HV), lambda b, s: (fwd_blk(b, s), 0)),
                   pl.BlockSpec((step_rows, HG_HV), lambda b, s: (bwd_blk(b, s), 0))],
        out_shape=[out_sd, out_sd],
        scratch_shapes=[pltpu.VMEM((2, HG_HEADS, HG_DV, HG_DK), F32)],
        compiler_params=_cparams(("arbitrary", "arbitrary")),
        name="hgrn_scan",
    )(pb, pb, vt, pf, pb, pb, vt, pf, tri)


def _attn_scores(q, kc_ref, kl_ref=None):
    scores = [_dot_nt(kc_ref[...], q)]
    if kl_ref is not None:
        scores += [_dot_nt(kl_ref[pl.ds(j * KEY_CHUNK, KEY_CHUNK), :], q)
                   for j in range(kl_ref.shape[0] // KEY_CHUNK)]
    m = jnp.max(scores[0], axis=0, keepdims=True)
    for s in scores[1:]:
        m = jnp.maximum(m, jnp.max(s, axis=0, keepdims=True))
    return scores, m


def _attn_values(scores, m, vtc_ref, vtl_ref=None):
    den, acc = None, None
    for j, s in enumerate(scores):
        p = jnp.exp2(s - m)
        vt = vtc_ref[...] if j == 0 else vtl_ref[:, pl.ds((j - 1) * KEY_CHUNK, KEY_CHUNK)]
        d_j = jnp.sum(p, axis=0, keepdims=True)
        a_j = _dot(vt, p.astype(BF16))
        den, acc = (d_j, a_j) if den is None else (den + d_j, acc + a_j)
    return (acc / den).T


def _attn_kernel(*refs, with_ctx):
    if with_ctx:
        q_ref, qc_ref, kl_ref, kc_ref, vtl_ref, vtc_ref, o_ref, oc_ref = refs
    else:
        q_ref, kl_ref, kc_ref, vtl_ref, vtc_ref, o_ref = refs
    nxt = _attn_scores(q_ref[pl.ds(0, TQ), :], kc_ref, kl_ref)
    for j in range(Q_SUB):
        cur = nxt
        if j + 1 < Q_SUB:
            nxt = _attn_scores(q_ref[pl.ds((j + 1) * TQ, TQ), :], kc_ref, kl_ref)
        elif with_ctx:
            nxt = _attn_scores(qc_ref[...], kc_ref)
        o_ref[pl.ds(j * TQ, TQ), :] = _attn_values(*cur, vtc_ref, vtl_ref).astype(o_ref.dtype)
    if with_ctx:
        oc_ref[...] = _attn_values(*nxt, vtc_ref).astype(oc_ref.dtype)


def _attn_call(rows, q, k, vt, with_ctx):
    tq = Q_SUB * TQ
    assert rows.ctx_len == TQ and rows.seq % tq == 0
    n_q = rows.seq // tq
    ctx0 = rows.n_lat // TQ
    v_w = MLA_HEADS * MLA_V
    ctx_qk = pl.BlockSpec((TQ, MLA_HW), lambda b, h, qi: (ctx0 + b, h))
    ctx_vt = pl.BlockSpec((MLA_V, TQ), lambda b, h, qi: (h, ctx0 + b))
    lat_q = pl.BlockSpec((tq, MLA_HW), lambda b, h, qi: (b * n_q + qi, h))
    lat_k = pl.BlockSpec((rows.seq, MLA_HW), lambda b, h, qi: (b, h))
    lat_vt = pl.BlockSpec((MLA_V, rows.seq), lambda b, h, qi: (h, b))
    lat_o = pl.BlockSpec((tq, MLA_V), lambda b, h, qi: (b * n_q + qi, h))
    lat_sd = jax.ShapeDtypeStruct((rows.n_lat, v_w), BF16)
    if with_ctx:
        assert n_q == 1
        in_specs, args = [lat_q, ctx_qk, lat_k, ctx_qk, lat_vt, ctx_vt], (q, q, k, k, vt, vt)
        out_specs = [lat_o, pl.BlockSpec((TQ, MLA_V), lambda b, h, qi: (b, h))]
        out_shape = [lat_sd, jax.ShapeDtypeStruct((rows.n_ctx, v_w), BF16)]
    else:
        in_specs, args = [lat_q, lat_k, ctx_qk, lat_vt, ctx_vt], (q, k, k, vt, vt)
        out_specs, out_shape = lat_o, lat_sd
    out = pl.pallas_call(
        functools.partial(_attn_kernel, with_ctx=with_ctx),
        grid=(rows.batch, MLA_HEADS, n_q),
        in_specs=in_specs,
        out_specs=out_specs,
        out_shape=out_shape,
        compiler_params=_cparams(("arbitrary", "arbitrary", "arbitrary")),
        name="mla_attention",
    )(*args)
    return tuple(out) if with_ctx else (out, None)


def _rope_swap_cols():
    idx = np.arange(MLA_ROPE).reshape(2, 2, MLA_ROPE // 4)
    return idx[:, ::-1, :].reshape(-1)


def _mla_weights(w_down, w_uq, w_ukv):
    swap = _rope_swap_cols()
    kr = w_down[:, Q_LORA + KV_LORA:]
    zpad = jnp.zeros((D_MODEL, LANES - MLA_ROPE), w_down.dtype)
    wd = jnp.concatenate([w_down[:, :Q_LORA + KV_LORA], kr, zpad, kr[:, swap], zpad], axis=1)
    uq = w_uq.reshape(Q_LORA, MLA_HEADS, MLA_QK)
    nope, rope = uq[..., :MLA_NOPE], uq[..., MLA_NOPE:]
    z64 = jnp.zeros((Q_LORA, MLA_HEADS, MLA_HW - MLA_QK), w_uq.dtype)
    wq = jnp.concatenate([nope, rope, z64], axis=-1).reshape(Q_LORA, MLA_HEADS * MLA_HW)
    wqs = jnp.concatenate([rope[..., swap], z64], axis=-1).reshape(Q_LORA, MLA_HEADS * LANES)
    ukv = w_ukv.reshape(KV_LORA, MLA_HEADS, MLA_NOPE + MLA_V)
    wk = ukv[..., :MLA_NOPE].reshape(KV_LORA, MLA_HEADS * MLA_NOPE)
    wvt = ukv[..., MLA_NOPE:].reshape(KV_LORA, MLA_HEADS * MLA_V).T
    return wd.astype(BF16), wq.astype(BF16), wqs.astype(BF16), wk.astype(BF16), wvt.astype(BF16)


def _rope_tables(seq):
    n_rows = seq // GRID_W
    row = jnp.repeat(jnp.arange(n_rows), GRID_W).astype(F32)
    col = jnp.tile(jnp.arange(GRID_W), n_rows).astype(F32)
    axis_dims = MLA_ROPE // 2
    inv = 1.0 / (ROPE_BASE ** (jnp.arange(0, axis_dims, 2, dtype=F32) / axis_dims))
    ang_r, ang_c = row[:, None] * inv, col[:, None] * inv
    cos64 = jnp.concatenate([jnp.cos(ang_r)] * 2 + [jnp.cos(ang_c)] * 2, axis=1)
    sin64 = jnp.concatenate([-jnp.sin(ang_r), jnp.sin(ang_r), -jnp.sin(ang_c), jnp.sin(ang_c)], axis=1)
    cos64 = jnp.concatenate([cos64, jnp.ones((TM, MLA_ROPE), F32)], axis=0)
    sin64 = jnp.concatenate([sin64, jnp.zeros((TM, MLA_ROPE), F32)], axis=0)
    n = seq + TM
    cos_t = jnp.concatenate([jnp.ones((n, MLA_NOPE), F32), cos64, jnp.zeros((n, MLA_HW - MLA_QK), F32)], axis=1)
    sin_t = jnp.concatenate([jnp.zeros((n, MLA_NOPE), F32), sin64, jnp.zeros((n, MLA_HW - MLA_QK), F32)], axis=1)
    return cos_t, sin_t


def kernel(x, c, ctx, c_ctx, mod_w, mod_b, norm_g, ffn_w_gate, ffn_w_up, ffn_w_down, hg_w_in, hg_w_out,
           hg_gn, hg_lb_logits, mla_w_down, mla_q_norm, mla_w_uq, mla_kv_norm, mla_w_ukv, mla_w_o, final_g):
    batch, seq, _ = x.shape
    rows = _Rows(batch, seq, ctx.shape[1])
    hs = (x.reshape(rows.n_lat, D_MODEL), ctx.reshape(rows.n_ctx, D_MODEL))

    mod_rows = -(-(batch + 1) // SUBLANES) * SUBLANES
    cond = jnp.concatenate([c, c_ctx[None], jnp.zeros((mod_rows - batch - 1, D_MODEL), F32)], axis=0)
    mod = _mod_call(cond, mod_w, mod_b)

    p_lb = jax.nn.softmax(hg_lb_logits.astype(F32), axis=0)
    lbs = jnp.maximum(jnp.cumsum(p_lb, axis=0) - p_lb[0:1], 0.0)
    tri = jnp.stack([jnp.tril(jnp.ones((HG_C, HG_C), F32)), jnp.triu(jnp.ones((HG_C, HG_C), F32))]).astype(BF16)
    head_id = np.arange(HG_HV) // HG_DV
    bd = jnp.asarray(head_id[:, None] == head_id[None, :], BF16)
    cos_t, sin_t = _rope_tables(seq)

    ffn_w = (_to_bf16(ffn_w_gate), _to_bf16(ffn_w_up), _to_bf16(ffn_w_down))

    for i in range(DEPTH):
        last = i == DEPTH - 1
        j = i // 2
        mod_l = mod[i].reshape(mod_rows, 1, N_MOD * D_MODEL)
        g = norm_g[i].reshape(3, 1, D_MODEL)
        hs = _ffn_call(rows, hs, mod_l, g[0], ffn_w, i, 0, 0, rows.all_tiles)
        n_out = rows.lat_tiles if last else rows.all_tiles
        if i % 2 == 0:
            w_in = hg_w_in[j].astype(BF16)
            w_vt = _hgrn_value_weights_t(hg_w_in, j)
            pb, pf, vt = _hgrn_in_call(rows, hs, mod_l, g[1], w_in, w_vt, lbs[j])
            o_f, o_b = _hgrn_scan_call(rows, pb, pf, vt, tri)
            gn = jnp.tile(hg_gn[j], HG_HEADS).reshape(1, HG_HV)
            mixer, mixer_args = "hgrn", (o_f, o_b, pf, gn, bd, hg_w_out[j].astype(BF16))
        else:
            w_dn, w_q, w_qs, w_k, w_vt = _mla_weights(mla_w_down[j], mla_w_uq[j], mla_w_ukv[j])
            q, k, vt = _mla_proj_call(rows, hs, mod_l, g[1], w_dn, mla_q_norm[j].reshape(1, Q_LORA),
                                      mla_kv_norm[j].reshape(1, KV_LORA), w_q, w_qs, w_k, w_vt, cos_t, sin_t)
            attn_lat, attn_ctx = _attn_call(rows, q, k, vt, with_ctx=not last)
            mixer, mixer_args = "mla", (attn_lat, attn_ctx, mla_w_o[j].astype(BF16))
        hs = _ffn_call(rows, hs, mod_l, g[2], ffn_w, i, 1, 6, n_out,
                       final_g=final_g.reshape(1, D_MODEL) if last else None, mixer=mixer, mixer_args=mixer_args)
    return hs.reshape(batch, seq, D_MODEL)
```
